```python
import math
import jax, jax.numpy as jnp
from jax import lax
import numpy as np

D_MODEL = 1024
BATCH = 2
SEQ = 8192
DEPTH = 2

CONV_DIM = 512
CONV_WIDTH = 3
FOX_HEADS = 8
FOX_HEAD_DIM = 64
FOX_DIM = FOX_HEADS * FOX_HEAD_DIM
Q_BLOCK = 128
EVEN_IN = 3 * CONV_DIM + 3 * FOX_DIM + FOX_HEADS
EVEN_MIX = CONV_DIM + FOX_DIM
SSM_INNER = 2 * D_MODEL
SSM_HEAD_DIM = 64
SSM_HEADS = SSM_INNER // SSM_HEAD_DIM
SSM_GROUPS = 4
SSM_STATE = 128
SSM_CONV_WIDTH = 4
SSM_CHUNK = 128
SSM_CONV_CH = SSM_INNER + 2 * SSM_GROUPS * SSM_STATE
ODD_IN = SSM_INNER + SSM_CONV_CH + SSM_HEADS
D_FF = 2816
FFN_CONV_WIDTH = 3
PLE_DIM = 256
N_EVEN = (DEPTH + 1) // 2
N_ODD = DEPTH // 2
LN_EPS = 1e-5
RMS_EPS = 1e-5

kernel_name = "hybrid_shortconv_fox_mamba2_deepnorm_convffn_ple"


def layer_norm(x, g, b):
    xf = x.astype(jnp.float32)
    mu = jnp.mean(xf, axis=-1, keepdims=True)
    var = jnp.mean(jnp.square(xf - mu), axis=-1, keepdims=True)
    return ((xf - mu) * lax.rsqrt(var + LN_EPS) * g + b).astype(x.dtype)


def causal_dwconv(x, w, b=None):
    K, C = w.shape
    out = lax.conv_general_dilated(
        x, w[:, None, :].astype(x.dtype), window_strides=(1,), padding=[(K - 1, 0)],
        dimension_numbers=('NWC', 'WIO', 'NWC'), feature_group_count=C)
    if b is not None:
        out = out + b
    return out


def forgetting_attention(q, k, v, f_logit):
    Bsz, L, _ = q.shape
    H, Dh = FOX_HEADS, FOX_HEAD_DIM
    def heads(t):
        return t.reshape(Bsz, L, H, Dh).transpose(0, 2, 1, 3)
    q, k, v = heads(q), heads(k), heads(v)
    log_f = jax.nn.log_sigmoid(f_logit.astype(jnp.float32))
    F = jnp.cumsum(log_f, axis=1).transpose(0, 2, 1)
    n_blk = L // Q_BLOCK
    qb = q.reshape(Bsz, H, n_blk, Q_BLOCK, Dh).transpose(2, 0, 1, 3, 4)
    Fb = F.reshape(Bsz, H, n_blk, Q_BLOCK).transpose(2, 0, 1, 3)
    k_pos = jnp.arange(L)
    scale = FOX_HEAD_DIM ** -0.5

    def one_block(args):
        i, q_i, F_i = args
        s = jnp.einsum('bhqd,bhkd->bhqk', q_i, k, preferred_element_type=jnp.float32) * scale
        s = s + F_i[..., None] - F[:, :, None, :]
        q_pos = i * Q_BLOCK + jnp.arange(Q_BLOCK)
        s = jnp.where(k_pos[None, :] <= q_pos[:, None], s, -jnp.inf)
        pr = jax.nn.softmax(s, axis=-1)
        return jnp.einsum('bhqk,bhkd->bhqd', pr.astype(v.dtype), v)

    out = lax.map(one_block, (jnp.arange(n_blk), qb, Fb))
    return out.transpose(1, 0, 3, 2, 4).reshape(Bsz, L, H * Dh)


def shortconv_fox_mixer(x, w_in, b_f, w_conv, w_out):
    sizes = [CONV_DIM] * 3 + [FOX_DIM] * 3 + [FOX_HEADS]
    splits = [int(s) for s in np.cumsum(sizes)[:-1]]
    gB, gC, h, q, k, v, f_logit = jnp.split(x @ w_in, splits, axis=-1)
    y_a = gB * causal_dwconv(gC * h, w_conv)
    y_b = forgetting_attention(q, k, v, f_logit + b_f)
    return jnp.concatenate([y_a, y_b], axis=-1) @ w_out


def ssd_chunked(x, dt, A, Bm, Cm):
    Bsz, L, H, P = x.shape
    G, N, Q = SSM_GROUPS, SSM_STATE, SSM_CHUNK
    R = H // G
    nc = L // Q
    xc = (x * dt[..., None]).reshape(Bsz, nc, Q, G, R, P)
    acs = jnp.cumsum((dt * A).reshape(Bsz, nc, Q, G, R), axis=2)
    Bc = Bm.reshape(Bsz, nc, Q, G, N)
    Cc = Cm.reshape(Bsz, nc, Q, G, N)
    seg = acs[:, :, :, None] - acs[:, :, None, :]
    causal = jnp.tril(jnp.ones((Q, Q), dtype=bool))[:, :, None, None]
    Lmat = jnp.exp(jnp.where(causal, seg, -jnp.inf))
    CB = jnp.einsum('bclgn,bcsgn->bclsg', Cc, Bc)
    y_diag = jnp.einsum('bclsg,bclsgr,bcsgrp->bclgrp', CB, Lmat, xc)
    decay_to_end = jnp.exp(acs[:, :, -1:] - acs)
    states = jnp.einsum('bcsgn,bcsgr,bcsgrp->bcgrpn', Bc, decay_to_end, xc)
    chunk_decay = jnp.exp(acs[:, :, -1])

    def step(hs, inp):
        s_c, d_c = inp
        return hs * d_c[..., None, None] + s_c, hs

    h0 = jnp.zeros((Bsz, G, R, P, N), jnp.float32)
    _, prev = lax.scan(step, h0, (states.astype(jnp.float32).transpose(1, 0, 2, 3, 4, 5),
                                  chunk_decay.transpose(1, 0, 2, 3)))
    prev = prev.transpose(1, 0, 2, 3, 4, 5)
    y_off = jnp.einsum('bclgn,bcgrpn,bclgr->bclgrp', Cc, prev, jnp.exp(acs))
    return (y_diag + y_off).reshape(Bsz, L, H, P)


def gated_group_rmsnorm(y, z, g):
    Bsz, L, Dn = y.shape
    u = (y * jax.nn.silu(z)).astype(jnp.float32).reshape(Bsz, L, SSM_GROUPS, Dn // SSM_GROUPS)
    u = u * lax.rsqrt(jnp.mean(jnp.square(u), axis=-1, keepdims=True) + RMS_EPS)
    return (u.reshape(Bsz, L, Dn) * g).astype(y.dtype)


def mamba2_mixer(x, w_in, conv_w, conv_b, dt_bias, a_log, d_skip, norm_g, w_out):
    Bsz, L, _ = x.shape
    z, xBC, dt = jnp.split(x @ w_in, [SSM_INNER, SSM_INNER + SSM_CONV_CH], axis=-1)
    xBC = jax.nn.silu(causal_dwconv(xBC, conv_w, conv_b))
    xs, Bm, Cm = jnp.split(xBC, [SSM_INNER, SSM_INNER + SSM_GROUPS * SSM_STATE], axis=-1)
    dt = jax.nn.softplus(dt.astype(jnp.float32) + dt_bias)
    A = -jnp.exp(a_log.astype(jnp.float32))
    xh = xs.reshape(Bsz, L, SSM_HEADS, SSM_HEAD_DIM)
    y = ssd_chunked(xh, dt, A,
                    Bm.reshape(Bsz, L, SSM_GROUPS, SSM_STATE),
                    Cm.reshape(Bsz, L, SSM_GROUPS, SSM_STATE))
    y = (y + d_skip[:, None] * xh).astype(x.dtype).reshape(Bsz, L, SSM_INNER)
    return gated_group_rmsnorm(y, z, norm_g) @ w_out


def conv_ffn(x, w_up, conv_w, conv_b, w_down):
    u = causal_dwconv(x @ w_up, conv_w, conv_b)
    g, v = jnp.split(u, 2, axis=-1)
    return (jax.nn.silu(g) * v) @ w_down


def per_layer_embed(x, p_i, w_proj, w_gate, b_gate):
    gate = jax.nn.sigmoid(x @ w_gate + b_gate)
    return x + gate * (p_i @ w_proj)


def setup_inputs(seed: int = 0) -> dict:
    key = jax.random.key(seed)
    ks = iter(jax.random.split(key, 32))
    beta = (8.0 * DEPTH) ** -0.25

    def nrm(shape, scale):
        return jax.random.normal(next(ks), shape, jnp.float32) * scale

    x = nrm((BATCH, SEQ, D_MODEL), 1.0)
    p = nrm((DEPTH, BATCH, SEQ, PLE_DIM), 1.0)
    even_w_in = nrm((N_EVEN, D_MODEL, EVEN_IN), D_MODEL ** -0.5)
    even_b_f = 2.0 + nrm((N_EVEN, FOX_HEADS), 0.5)
    even_conv_w = nrm((N_EVEN, CONV_WIDTH, CONV_DIM), CONV_WIDTH ** -0.5)
    even_w_out = nrm((N_EVEN, EVEN_MIX, D_MODEL), beta * EVEN_MIX ** -0.5)
    odd_w_in = nrm((N_ODD, D_MODEL, ODD_IN), D_MODEL ** -0.5)
    odd_conv_w = nrm((N_ODD, SSM_CONV_WIDTH, SSM_CONV_CH), SSM_CONV_WIDTH ** -0.5)
    odd_conv_b = nrm((N_ODD, SSM_CONV_CH), 0.02)
    u = jax.random.uniform(next(ks), (N_ODD, SSM_HEADS), jnp.float32)
    dt0 = jnp.exp(u * (math.log(0.1) - math.log(0.001)) + math.log(0.001))
    odd_dt_bias = dt0 + jnp.log(-jnp.expm1(-dt0))
    odd_a_log = jnp.log(jax.random.uniform(next(ks), (N_ODD, SSM_HEADS), jnp.float32, 1.0, 16.0))
    odd_d_skip = 1.0 + nrm((N_ODD, SSM_HEADS), 0.1)
    odd_norm_g = 1.0 + nrm((N_ODD, SSM_INNER), 0.02)
    odd_w_out = nrm((N_ODD, SSM_INNER, D_MODEL), beta * SSM_INNER ** -0.5)
    ln_mix_g = 1.0 + nrm((DEPTH, D_MODEL), 0.02)
    ln_mix_b = nrm((DEPTH, D_MODEL), 0.02)
    ffn_w_up = nrm((DEPTH, D_MODEL, 2 * D_FF), D_MODEL ** -0.5)
    ffn_conv_w = nrm((DEPTH, FFN_CONV_WIDTH, 2 * D_FF), FFN_CONV_WIDTH ** -0.5)
    ffn_conv_b = nrm((DEPTH, 2 * D_FF), 0.02)
    ffn_w_down = nrm((DEPTH, D_FF, D_MODEL), beta * D_FF ** -0.5)
    ln_ffn_g = 1.0 + nrm((DEPTH, D_MODEL), 0.02)
    ln_ffn_b = nrm((DEPTH, D_MODEL), 0.02)
    ple_w_proj = nrm((DEPTH, PLE_DIM, D_MODEL), PLE_DIM ** -0.5)
    ple_w_gate = nrm((DEPTH, D_MODEL, D_MODEL), D_MODEL ** -0.5)
    ple_b_gate = nrm((DEPTH, D_MODEL), 0.02)
    return {"x": x, "p": p,
            "even_w_in": even_w_in, "even_b_f": even_b_f, "even_conv_w": even_conv_w,
            "even_w_out": even_w_out,
            "odd_w_in": odd_w_in, "odd_conv_w": odd_conv_w, "odd_conv_b": odd_conv_b,
            "odd_dt_bias": odd_dt_bias, "odd_a_log": odd_a_log, "odd_d_skip": odd_d_skip,
            "odd_norm_g": odd_norm_g, "odd_w_out": odd_w_out,
            "ln_mix_g": ln_mix_g, "ln_mix_b": ln_mix_b,
            "ffn_w_up": ffn_w_up, "ffn_conv_w": ffn_conv_w, "ffn_conv_b": ffn_conv_b,
            "ffn_w_down": ffn_w_down, "ln_ffn_g": ln_ffn_g, "ln_ffn_b": ln_ffn_b,
            "ple_w_proj": ple_w_proj, "ple_w_gate": ple_w_gate, "ple_b_gate": ple_b_gate}


def reference(x, p, even_w_in, even_b_f, even_conv_w, even_w_out,
              odd_w_in, odd_conv_w, odd_conv_b, odd_dt_bias, odd_a_log, odd_d_skip,
              odd_norm_g, odd_w_out, ln_mix_g, ln_mix_b, ffn_w_up, ffn_conv_w, ffn_conv_b,
              ffn_w_down, ln_ffn_g, ln_ffn_b, ple_w_proj, ple_w_gate, ple_b_gate):
    alpha = (2.0 * DEPTH) ** 0.25
    h = x
    for i in range(DEPTH):
        j = i // 2
        if i % 2 == 0:
            mix = shortconv_fox_mixer(h, even_w_in[j], even_b_f[j], even_conv_w[j], even_w_out[j])
        else:
            mix = mamba2_mixer(h, odd_w_in[j], odd_conv_w[j], odd_conv_b[j], odd_dt_bias[j],
                               odd_a_log[j], odd_d_skip[j], odd_norm_g[j], odd_w_out[j])
        h = layer_norm(alpha * h + mix, ln_mix_g[i], ln_mix_b[i])
        ffn = conv_ffn(h, ffn_w_up[i], ffn_conv_w[i], ffn_conv_b[i], ffn_w_down[i])
        h = layer_norm(alpha * h + ffn, ln_ffn_g[i], ln_ffn_b[i])
        h = per_layer_embed(h, p[i], ple_w_proj[i], ple_w_gate[i], ple_b_gate[i])
    return h
```

```python
import functools

import jax
import jax.numpy as jnp
from jax import lax
from jax.experimental import pallas as pl
from jax.experimental.pallas import tpu as pltpu

F32 = jnp.float32
BF16 = jnp.bfloat16

D_MODEL = 1024
BATCH = 2
SEQ = 8192
DEPTH = 2
TOKENS = BATCH * SEQ

CONV_DIM = 512
CONV_WIDTH = 3
FOX_HEADS = 8
FOX_HEAD_DIM = 64
FOX_DIM = FOX_HEADS * FOX_HEAD_DIM
SSM_INNER = 2 * D_MODEL
SSM_HEAD_DIM = 64
SSM_HEADS = SSM_INNER // SSM_HEAD_DIM
SSM_GROUPS = 4
SSM_STATE = 128
SSM_CONV_WIDTH = 4
SSM_CHUNK = 128
SSM_CONV_CH = SSM_INNER + 2 * SSM_GROUPS * SSM_STATE
D_FF = 2816
FFN_CONV_WIDTH = 3
PLE_DIM = 256
LN_EPS = 1e-5
RMS_EPS = 1e-5
ALPHA = (2.0 * DEPTH) ** 0.25

LANES = 128
HALO = 8
NEG = -1e30
VMEM_LIMIT = 56 * 1024 * 1024

ROW_TILE = 512
FFN_CHUNK = 256
ATTN_TILE = 256


def _resident(shape):
    nd = len(shape)
    return pl.BlockSpec(shape, lambda *_: (0,) * nd, pipeline_mode=pl.Buffered(1))


def _params(sem):
    return pltpu.CompilerParams(dimension_semantics=sem, vmem_limit_bytes=VMEM_LIMIT)


def _silu(x):
    return x * (1.0 / (1.0 + jnp.exp(-x)))


def _softplus(x):
    return jnp.maximum(x, 0.0) + jnp.log1p(jnp.exp(-jnp.abs(x)))


def _log_sigmoid(x):
    return jnp.minimum(x, 0.0) - jnp.log1p(jnp.exp(-jnp.abs(x)))


def _layer_norm(r, g, b):
    mu = jnp.mean(r, axis=-1, keepdims=True)
    d = r - mu
    var = jnp.mean(d * d, axis=-1, keepdims=True)
    return d * lax.rsqrt(var + LN_EPS) * g + b


def _causal_taps(ext_ref, cur, w, rows):
    k_taps = w.shape[0]
    out = w[k_taps - 1:k_taps, :] * cur
    for k in range(k_taps - 1):
        off = HALO - (k_taps - 1) + k
        out = out + w[k:k + 1, :] * ext_ref[off:off + rows, :]
    return out


def _proj_kernel(x_ref, *refs, n_out, chunk):
    w_refs, o_refs = refs[:n_out], refs[n_out:]
    x = x_ref[...]
    xb = x.astype(BF16)
    for w_ref, o_ref in zip(w_refs, o_refs):
        n = w_ref.shape[1]
        for c0 in range(0, n, chunk):
            c1 = min(c0 + chunk, n)
            if w_ref.dtype == F32:
                acc = jnp.dot(x, w_ref[:, c0:c1], preferred_element_type=F32,
                              precision=lax.Precision.HIGHEST)
            else:
                acc = jnp.dot(xb, w_ref[:, c0:c1], preferred_element_type=F32)
            o_ref[:, c0:c1] = acc.astype(o_ref.dtype)


def _proj(x, ws, out_dtypes, name):
    m, k = x.shape
    tm = ROW_TILE
    return pl.pallas_call(
        functools.partial(_proj_kernel, n_out=len(ws), chunk=512),
        grid=(m // tm,),
        in_specs=[pl.BlockSpec((tm, k), lambda i: (i, 0))] + [_resident(w.shape) for w in ws],
        out_specs=[pl.BlockSpec((tm, w.shape[1]), lambda i: (i, 0)) for w in ws],
        out_shape=[jax.ShapeDtypeStruct((m, w.shape[1]), dt) for w, dt in zip(ws, out_dtypes)],
        compiler_params=_params(("arbitrary",)),
        name=name,
    )(x, *ws)


def _fcum_kernel(f_ref, b_ref, col_ref, row_ref, carry_ref):
    j = pl.program_id(1)

    @pl.when(j == 0)
    def _():
        carry_ref[...] = jnp.zeros_like(carry_ref)

    lf = _log_sigmoid(f_ref[...] + b_ref[...])
    r = lax.broadcasted_iota(jnp.int32, (LANES, LANES), 0)
    c = lax.broadcasted_iota(jnp.int32, (LANES, LANES), 1)
    tri = (r >= c).astype(F32)
    cs = jnp.dot(tri, lf, preferred_element_type=F32, precision=lax.Precision.HIGHEST)
    cum = cs + carry_ref[0:1, :]
    col_ref[...] = cum
    row_ref[0] = cum.T[0:FOX_HEADS, :]
    carry_ref[...] = jnp.broadcast_to(cum[LANES - 1:LANES, :], carry_ref.shape)


def _fcum(f_logit, b_f):
    nblk = SEQ // LANES
    return pl.pallas_call(
        _fcum_kernel,
        grid=(BATCH, nblk),
        in_specs=[pl.BlockSpec((LANES, LANES), lambda b, j: (b * nblk + j, 0)),
                  pl.BlockSpec((1, LANES), lambda b, j: (0, 0))],
        out_specs=[pl.BlockSpec((LANES, LANES), lambda b, j: (b * nblk + j, 0)),
                   pl.BlockSpec((1, FOX_HEADS, LANES), lambda b, j: (b, 0, j))],
        out_shape=[jax.ShapeDtypeStruct((TOKENS, LANES), F32),
                   jax.ShapeDtypeStruct((BATCH, FOX_HEADS, SEQ), F32)],
        scratch_shapes=[pltpu.VMEM((HALO, LANES), F32)],
        compiler_params=_params(("arbitrary", "arbitrary")),
        name="fox_forget_cumsum",
    )(f_logit, b_f)


def _fox_kernel(q_ref, k_ref, v_ref, fc_ref, fr_ref, o_ref, *, t):
    hp = pl.program_id(1)
    qi = pl.program_id(2)
    q2 = q_ref[...]
    lane = lax.broadcasted_iota(jnp.int32, (t, LANES), 1)
    lo = lane < FOX_HEAD_DIM
    zero = jnp.zeros_like(q2)
    qh = (jnp.where(lo, q2, zero), jnp.where(lo, zero, q2))
    fc = fc_ref[...]
    f_q = [jnp.sum(jnp.where(lane == 2 * hp + e, fc, 0.0), axis=1, keepdims=True) for e in (0, 1)]
    row = lax.broadcasted_iota(jnp.int32, (t, t), 0)
    col = lax.broadcasted_iota(jnp.int32, (t, t), 1)
    causal = col <= row

    def block(j, carry, masked):
        start = pl.multiple_of(j * t, t)
        k = k_ref[pl.ds(start, t), :]
        v = v_ref[pl.ds(start, t), :]
        out = []
        for e in (0, 1):
            m_prev, l_prev, acc_prev = carry[e]
            s = lax.dot_general(qh[e], k, (((1,), (1,)), ((), ())), preferred_element_type=F32)
            f_k = fr_ref[0, pl.ds(2 * hp + e, 1), pl.ds(start, t)]
            s = s + (f_q[e] - f_k)
            if masked:
                s = jnp.where(causal, s, NEG)
            m_new = jnp.maximum(m_prev, jnp.max(s, axis=1, keepdims=True))
            a = jnp.exp(m_prev - m_new)
            p = jnp.exp(s - m_new)
            l_new = a * l_prev + jnp.sum(p, axis=1, keepdims=True)
            acc_new = a * acc_prev + jnp.dot(p.astype(BF16), v, preferred_element_type=F32)
            out.append((m_new, l_new, acc_new))
        return tuple(out)

    init = tuple((jnp.full((t, 1), NEG, F32), jnp.zeros((t, 1), F32), jnp.zeros((t, LANES), F32))
                 for _ in (0, 1))
    carry = lax.fori_loop(0, qi, lambda j, c: block(j, c, False), init)
    (_, l0, a0), (_, l1, a1) = block(qi, carry, True)
    o_ref[...] = jnp.where(lo, a0 / l0, a1 / l1).astype(o_ref.dtype)


def _fox_attention(qkv, f_col, f_row):
    t = ATTN_TILE
    nq = SEQ // t
    pairs = FOX_HEADS // 2
    return pl.pallas_call(
        functools.partial(_fox_kernel, t=t),
        grid=(BATCH, pairs, nq),
        in_specs=[pl.BlockSpec((t, LANES), lambda b, h, i: (b * nq + i, h)),
                  pl.BlockSpec((SEQ, LANES), lambda b, h, i: (b, pairs + h)),
                  pl.BlockSpec((SEQ, LANES), lambda b, h, i: (b, 2 * pairs + h)),
                  pl.BlockSpec((t, LANES), lambda b, h, i: (b * nq + i, 0)),
                  pl.BlockSpec((1, FOX_HEADS, SEQ), lambda b, h, i: (b, 0, 0))],
        out_specs=pl.BlockSpec((t, LANES), lambda b, h, i: (b * nq + i, h)),
        out_shape=jax.ShapeDtypeStruct((TOKENS, FOX_DIM), BF16),
        compiler_params=_params(("arbitrary", "arbitrary", "arbitrary")),
        name="fox_attention",
    )(qkv, qkv, qkv, f_col, f_row)


def _even_out_kernel(gb_ref, gc_ref, hh_ref, yb_ref, h_ref, wc_ref, wa_ref, wb_ref, g_ref, b_ref,
                     o_ref, ext_ref, *, tm, tiles_per_seq):
    i = pl.program_id(0)

    @pl.when(lax.rem(i, tiles_per_seq) == 0)
    def _():
        ext_ref[0:HALO, :] = jnp.zeros((HALO, CONV_DIM), F32)

    u = gc_ref[...].astype(F32) * hh_ref[...].astype(F32)
    ext_ref[HALO:, :] = u
    conv = _causal_taps(ext_ref, u, wc_ref[...], tm)
    ext_ref[0:HALO, :] = u[tm - HALO:, :]
    ya = (gb_ref[...].astype(F32) * conv).astype(BF16)
    mix = jnp.dot(ya, wa_ref[...], preferred_element_type=F32)
    mix = mix + jnp.dot(yb_ref[...], wb_ref[...], preferred_element_type=F32)
    o_ref[...] = _layer_norm(ALPHA * h_ref[...] + mix, g_ref[...], b_ref[...])


def _even_out(conv_in, yb, h, w_conv, w_a, w_b, g, b):
    tm = ROW_TILE
    return pl.pallas_call(
        functools.partial(_even_out_kernel, tm=tm, tiles_per_seq=SEQ // tm),
        grid=(TOKENS // tm,),
        in_specs=[pl.BlockSpec((tm, CONV_DIM), lambda i: (i, 0)),
                  pl.BlockSpec((tm, CONV_DIM), lambda i: (i, 1)),
                  pl.BlockSpec((tm, CONV_DIM), lambda i: (i, 2)),
                  pl.BlockSpec((tm, FOX_DIM), lambda i: (i, 0)),
                  pl.BlockSpec((tm, D_MODEL), lambda i: (i, 0)),
                  _resident(w_conv.shape), _resident(w_a.shape), _resident(w_b.shape),
                  _resident(g.shape), _resident(b.shape)],
        out_specs=pl.BlockSpec((tm, D_MODEL), lambda i: (i, 0)),
        out_shape=jax.ShapeDtypeStruct((TOKENS, D_MODEL), F32),
        scratch_shapes=[pltpu.VMEM((tm + HALO, CONV_DIM), F32)],
        compiler_params=_params(("arbitrary",)),
        name="even_out_proj_ln",
    )(conv_in, conv_in, conv_in, yb, h, w_conv, w_a, w_b, g, b)


def _odd_out_kernel(u_ref, h_ref, w_ref, g_ref, b_ref, o_ref):
    mix = jnp.dot(u_ref[...], w_ref[...], preferred_element_type=F32)
    o_ref[...] = _layer_norm(ALPHA * h_ref[...] + mix, g_ref[...], b_ref[...])


def _odd_out(u, h, w, g, b):
    tm = ROW_TILE
    return pl.pallas_call(
        _odd_out_kernel,
        grid=(TOKENS // tm,),
        in_specs=[pl.BlockSpec((tm, SSM_INNER), lambda i: (i, 0)),
                  pl.BlockSpec((tm, D_MODEL), lambda i: (i, 0)),
                  _resident(w.shape), _resident(g.shape), _resident(b.shape)],
        out_specs=pl.BlockSpec((tm, D_MODEL), lambda i: (i, 0)),
        out_shape=jax.ShapeDtypeStruct((TOKENS, D_MODEL), F32),
        compiler_params=_params(("arbitrary",)),
        name="odd_out_proj_ln",
    )(u, h, w, g, b)


def _ffn_kernel(x_ref, p_ref, wup_ref, cw_ref, cb_ref, wdn_ref, g_ref, b_ref, wproj_ref, wgate_ref,
                bgate_ref, o_ref, halo_ref, extg_ref, extv_ref, acc_ref, *, tm, tiles_per_seq):
    i = pl.program_id(0)

    @pl.when(lax.rem(i, tiles_per_seq) == 0)
    def _():
        halo_ref[...] = jnp.zeros_like(halo_ref)

    x = x_ref[...]
    xb = x.astype(BF16)
    tf = FFN_CHUNK
    for c in range(D_FF // tf):
        branches = []
        for part, ext_ref in ((0, extg_ref), (1, extv_ref)):
            c0 = part * D_FF + c * tf
            u = jnp.dot(xb, wup_ref[:, c0:c0 + tf], preferred_element_type=F32)
            ext_ref[0:HALO, :] = halo_ref[:, c0:c0 + tf]
            ext_ref[HALO:, :] = u
            halo_ref[:, c0:c0 + tf] = u[tm - HALO:, :]
            branches.append(_causal_taps(ext_ref, u, cw_ref[:, c0:c0 + tf], tm) + cb_ref[:, c0:c0 + tf])
        act = (_silu(branches[0]) * branches[1]).astype(BF16)
        d = jnp.dot(act, wdn_ref[c * tf:(c + 1) * tf, :], preferred_element_type=F32)
        if c == 0:
            acc_ref[...] = d
        else:
            acc_ref[...] += d
    h2 = _layer_norm(ALPHA * x + acc_ref[...], g_ref[...], b_ref[...])
    gate_logit = jnp.dot(h2.astype(BF16), wgate_ref[...], preferred_element_type=F32) + bgate_ref[...]
    gate = 1.0 / (1.0 + jnp.exp(-gate_logit))
    emb = jnp.dot(p_ref[...].astype(BF16), wproj_ref[...], preferred_element_type=F32)
    o_ref[...] = h2 + gate * emb


def _ffn_ple(x, p, w_up, conv_w, conv_b, w_down, g, b, w_proj, w_gate, b_gate):
    tm = ROW_TILE
    tf = FFN_CHUNK
    return pl.pallas_call(
        functools.partial(_ffn_kernel, tm=tm, tiles_per_seq=SEQ // tm),
        grid=(TOKENS // tm,),
        in_specs=[pl.BlockSpec((tm, D_MODEL), lambda i: (i, 0)),
                  pl.BlockSpec((tm, PLE_DIM), lambda i: (i, 0)),
                  _resident(w_up.shape), _resident(conv_w.shape), _resident(conv_b.shape),
                  _resident(w_down.shape), _resident(g.shape), _resident(b.shape),
                  _resident(w_proj.shape), _resident(w_gate.shape), _resident(b_gate.shape)],
        out_specs=pl.BlockSpec((tm, D_MODEL), lambda i: (i, 0)),
        out_shape=jax.ShapeDtypeStruct((TOKENS, D_MODEL), F32),
        scratch_shapes=[pltpu.VMEM((HALO, 2 * D_FF), F32),
                        pltpu.VMEM((tm + HALO, tf), F32),
                        pltpu.VMEM((tm + HALO, tf), F32),
                        pltpu.VMEM((tm, D_MODEL), F32)],
        compiler_params=_params(("arbitrary",)),
        name="conv_ffn_ln_ple",
    )(x, p, w_up, conv_w, conv_b, w_down, g, b, w_proj, w_gate, b_gate)


def _ssd_kernel(z_ref, xbc_ref, dt_ref, cw_ref, cb_ref, dtb_ref, alog_ref, dsk_ref, ng_ref, o_ref,
                ext_ref, xc_ref, s_ref, u_ref):
    q = SSM_CHUNK
    n = SSM_STATE
    c = pl.program_id(1)

    @pl.when(c == 0)
    def _():
        ext_ref[0:HALO, :] = jnp.zeros((HALO, SSM_CONV_CH), F32)
        s_ref[...] = jnp.zeros_like(s_ref)

    xr = xbc_ref[...].astype(F32)
    ext_ref[HALO:, :] = xr
    conv = _causal_taps(ext_ref, xr, cw_ref[...], q) + cb_ref[...]
    ext_ref[0:HALO, :] = xr[q - HALO:, :]
    xc_ref[...] = _silu(conv)

    dt = _softplus(dt_ref[...] + dtb_ref[...])
    a = dt * (-jnp.exp(alog_ref[...]))
    row = lax.broadcasted_iota(jnp.int32, (q, q), 0)
    col = lax.broadcasted_iota(jnp.int32, (q, q), 1)
    causal = row >= col
    acs = jnp.dot(causal.astype(F32), a, preferred_element_type=F32, precision=lax.Precision.HIGHEST)
    dt_t = dt.T
    acs_t = acs.T
    tot = acs_t[:, q - 1:q]
    w_t = dt_t * jnp.exp(tot - acs_t)
    eacs = jnp.exp(acs)
    dec = jnp.exp(tot)
    lo = lax.broadcasted_iota(jnp.int32, (q, LANES), 1) < SSM_HEAD_DIM

    heads_per_group = SSM_HEADS // SSM_GROUPS
    group_w = heads_per_group * SSM_HEAD_DIM
    for g in range(SSM_GROUPS):
        b0 = SSM_INNER + g * n
        c0 = SSM_INNER + SSM_GROUPS * n + g * n
        bg = xc_ref[:, b0:b0 + n]
        cg = xc_ref[:, c0:c0 + n]
        cb = lax.dot_general(cg.astype(BF16), bg.astype(BF16), (((1,), (1,)), ((), ())),
                             preferred_element_type=F32)
        bg_t = bg.T
        for pr in range(heads_per_group // 2):
            j = g * (heads_per_group // 2) + pr
            sl = slice(j * LANES, (j + 1) * LANES)
            x = xc_ref[:, sl]
            xb = x.astype(BF16)
            s_prev = s_ref[:, sl]
            rhs = jnp.concatenate([xb, s_prev.astype(BF16)], axis=0)
            ys, news, decs = [], [], []
            for e in (0, 1):
                h = 2 * j + e
                seg = acs[:, h:h + 1] - acs_t[h:h + 1, :]
                lmat = jnp.exp(jnp.where(causal, seg, NEG))
                m_h = (cb * lmat * dt_t[h:h + 1, :]).astype(BF16)
                w2 = (cg * eacs[:, h:h + 1]).astype(BF16)
                lhs = jnp.concatenate([m_h, w2], axis=1)
                ys.append(jnp.dot(lhs, rhs, preferred_element_type=F32))
                bw_t = (bg_t * w_t[h:h + 1, :]).astype(BF16)
                news.append(jnp.dot(bw_t, xb, preferred_element_type=F32))
                decs.append(jnp.broadcast_to(dec[h:h + 1, :], (n, LANES)))
            y = jnp.where(lo, ys[0], ys[1])
            s_ref[:, sl] = s_prev * jnp.where(lo, decs[0], decs[1]) + jnp.where(lo, news[0], news[1])
            y = y + dsk_ref[:, sl] * x
            u_ref[:, sl] = y * _silu(z_ref[:, sl].astype(F32))
        gs = slice(g * group_w, (g + 1) * group_w)
        ug = u_ref[:, gs]
        ms = jnp.mean(ug * ug, axis=-1, keepdims=True)
        o_ref[:, gs] = (ug * lax.rsqrt(ms + RMS_EPS) * ng_ref[:, gs]).astype(o_ref.dtype)


def _ssd(z, xbc, dt_raw, conv_w, conv_b, dt_bias, a_log, d_skip, norm_g):
    q = SSM_CHUNK
    nc = SEQ // q
    row = lambda b, c: (b * nc + c, 0)
    return pl.pallas_call(
        _ssd_kernel,
        grid=(BATCH, nc),
        in_specs=[pl.BlockSpec((q, SSM_INNER), row),
                  pl.BlockSpec((q, SSM_CONV_CH), row),
                  pl.BlockSpec((q, LANES), row),
                  _resident(conv_w.shape), _resident(conv_b.shape), _resident(dt_bias.shape),
                  _resident(a_log.shape), _resident(d_skip.shape), _resident(norm_g.shape)],
        out_specs=pl.BlockSpec((q, SSM_INNER), row),
        out_shape=jax.ShapeDtypeStruct((TOKENS, SSM_INNER), BF16),
        scratch_shapes=[pltpu.VMEM((q + HALO, SSM_CONV_CH), F32),
                        pltpu.VMEM((q, SSM_CONV_CH), F32),
                        pltpu.VMEM((SSM_STATE, SSM_INNER), F32),
                        pltpu.VMEM((q, SSM_INNER), F32)],
        compiler_params=_params(("arbitrary", "arbitrary")),
        name="mamba2_ssd",
    )(z, xbc, dt_raw, conv_w, conv_b, dt_bias, a_log, d_skip, norm_g)


def _pad_cols(w, n):
    return jnp.pad(w, ((0, 0), (0, n - w.shape[1])))


def _row(v):
    return v.reshape(1, -1)


def _even_mixer(h, w_in, b_f, w_conv, w_out, ln_g, ln_b):
    c_end = 3 * CONV_DIM
    q_end = c_end + FOX_DIM
    a_end = c_end + 3 * FOX_DIM
    w_c = w_in[:, :c_end].astype(BF16)
    w_q = w_in[:, c_end:q_end] * (FOX_HEAD_DIM ** -0.5)
    w_qkv = jnp.concatenate([w_q, w_in[:, q_end:a_end]], axis=1).astype(BF16)
    w_f = _pad_cols(w_in[:, a_end:], LANES)
    conv_in, qkv, f_logit = _proj(h, [w_c, w_qkv, w_f], [BF16, BF16, F32], "even_in_proj")
    f_col, f_row = _fcum(f_logit, _pad_cols(_row(b_f), LANES))
    yb = _fox_attention(qkv, f_col, f_row)
    return _even_out(conv_in, yb, h, w_conv, w_out[:CONV_DIM].astype(BF16),
                     w_out[CONV_DIM:].astype(BF16), _row(ln_g), _row(ln_b))


def _odd_mixer(h, w_in, conv_w, conv_b, dt_bias, a_log, d_skip, norm_g, w_out, ln_g, ln_b):
    x_end = SSM_INNER + SSM_CONV_CH
    w_z = w_in[:, :SSM_INNER].astype(BF16)
    w_x = w_in[:, SSM_INNER:x_end].astype(BF16)
    w_dt = _pad_cols(w_in[:, x_end:], LANES)
    z, xbc, dt_raw = _proj(h, [w_z, w_x, w_dt], [BF16, BF16, F32], "odd_in_proj")
    u = _ssd(z, xbc, dt_raw, conv_w, _row(conv_b), _pad_cols(_row(dt_bias), LANES),
             _pad_cols(_row(a_log), LANES), _row(jnp.repeat(d_skip, SSM_HEAD_DIM)), _row(norm_g))
    return _odd_out(u, h, w_out.astype(BF16), _row(ln_g), _row(ln_b))


def kernel(x, p, even_w_in, even_b_f, even_conv_w, even_w_out, odd_w_in, odd_conv_w, odd_conv_b,
           odd_dt_bias, odd_a_log, odd_d_skip, odd_norm_g, odd_w_out, ln_mix_g, ln_mix_b, ffn_w_up,
           ffn_conv_w, ffn_conv_b, ffn_w_down, ln_ffn_g, ln_ffn_b, ple_w_proj, ple_w_gate,
           ple_b_gate):
    h = x.reshape(TOKENS, D_MODEL)
    for i in range(DEPTH):
        j = i // 2
        if i % 2 == 0:
            h = _even_mixer(h, even_w_in[j], even_b_f[j], even_conv_w[j], even_w_out[j],
                            ln_mix_g[i], ln_mix_b[i])
        else:
            h = _odd_mixer(h, odd_w_in[j], odd_conv_w[j], odd_conv_b[j], odd_dt_bias[j],
                           odd_a_log[j], odd_d_skip[j], odd_norm_g[j], odd_w_out[j],
                           ln_mix_g[i], ln_mix_b[i])
        h = _ffn_ple(h, p[i].reshape(TOKENS, PLE_DIM), ffn_w_up[i].astype(BF16), ffn_conv_w[i],
                     _row(ffn_conv_b[i]), ffn_w_down[i].astype(BF16), _row(ln_ffn_g[i]),
                     _row(ln_ffn_b[i]), ple_w_proj[i].astype(BF16), ple_w_gate[i].astype(BF16),
                     _row(ple_b_gate[i]))
    return h.reshape(BATCH, SEQ, D_MODEL)
```

```python
import functools

import jax
import jax.numpy as jnp
from jax import lax
from jax.experimental import pallas as pl
from jax.experimental.pallas import tpu as pltpu

F32 = jnp.float32
BF16 = jnp.bfloat16

D_MODEL = 1024
BATCH = 2
SEQ = 8192
DEPTH = 2
TOKENS = BATCH * SEQ

CONV_DIM = 512
CONV_WIDTH = 3
FOX_HEADS = 8
FOX_HEAD_DIM = 64
FOX_DIM = FOX_HEADS * FOX_HEAD_DIM
SSM_INNER = 2 * D_MODEL
SSM_HEAD_DIM = 64
SSM_HEADS = SSM_INNER // SSM_HEAD_DIM
SSM_GROUPS = 4
SSM_STATE = 128
SSM_CONV_WIDTH = 4
SSM_CHUNK = 128
SSM_CONV_CH = SSM_INNER + 2 * SSM_GROUPS * SSM_STATE
D_FF = 2816
FFN_CONV_WIDTH = 3
PLE_DIM = 256
LN_EPS = 1e-5
RMS_EPS = 1e-5
ALPHA = (2.0 * DEPTH) ** 0.25

LANES = 128
HALO = 8
NEG = -1e30
VMEM_LIMIT = 56 * 1024 * 1024

ROW_TILE = 512
FFN_CHUNK = 256
ATTN_TILE = 512
LOG2E = 1.4426950408889634


def _resident(shape):
    nd = len(shape)
    return pl.BlockSpec(shape, lambda *_: (0,) * nd, pipeline_mode=pl.Buffered(1))


def _params(sem):
    return pltpu.CompilerParams(dimension_semantics=sem, vmem_limit_bytes=VMEM_LIMIT)


def _silu(x):
    return x * (1.0 / (1.0 + jnp.exp(-x)))


def _softplus(x):
    return jnp.maximum(x, 0.0) + jnp.log1p(jnp.exp(-jnp.abs(x)))


def _log_sigmoid(x):
    return jnp.minimum(x, 0.0) - jnp.log1p(jnp.exp(-jnp.abs(x)))


def _layer_norm(r, g, b):
    mu = jnp.mean(r, axis=-1, keepdims=True)
    d = r - mu
    var = jnp.mean(d * d, axis=-1, keepdims=True)
    return d * lax.rsqrt(var + LN_EPS) * g + b


def _causal_taps(ext_ref, cur, w, rows):
    k_taps = w.shape[0]
    out = w[k_taps - 1:k_taps, :] * cur
    for k in range(k_taps - 1):
        off = HALO - (k_taps - 1) + k
        out = out + w[k:k + 1, :] * ext_ref[off:off + rows, :]
    return out


def _proj_kernel(x_ref, *refs, n_out, chunk, lead_scales):
    w_refs, o_refs = refs[:n_out], refs[n_out:]
    x = x_ref[...]
    xb = x.astype(BF16)
    for w_ref, o_ref, (lead_cols, lead_scale) in zip(w_refs, o_refs, lead_scales):
        n = w_ref.shape[1]
        for c0 in range(0, n, chunk):
            c1 = min(c0 + chunk, n)
            if w_ref.dtype == F32:
                acc = jnp.dot(x, w_ref[:, c0:c1], preferred_element_type=F32,
                              precision=lax.Precision.HIGHEST)
            else:
                acc = jnp.dot(xb, w_ref[:, c0:c1], preferred_element_type=F32)
            if c1 <= lead_cols:
                acc = acc * lead_scale
            o_ref[:, c0:c1] = acc.astype(o_ref.dtype)


def _proj(x, ws, out_dtypes, name, lead_scales=None):
    m, k = x.shape
    tm = ROW_TILE
    chunk = 512
    if lead_scales is None:
        lead_scales = [(0, 1.0)] * len(ws)
    assert all(cols % chunk == 0 for cols, _ in lead_scales)
    return pl.pallas_call(
        functools.partial(_proj_kernel, n_out=len(ws), chunk=chunk, lead_scales=tuple(lead_scales)),
        grid=(m // tm,),
        in_specs=[pl.BlockSpec((tm, k), lambda i: (i, 0))] + [_resident(w.shape) for w in ws],
        out_specs=[pl.BlockSpec((tm, w.shape[1]), lambda i: (i, 0)) for w in ws],
        out_shape=[jax.ShapeDtypeStruct((m, w.shape[1]), dt) for w, dt in zip(ws, out_dtypes)],
        compiler_params=_params(("arbitrary",)),
        name=name,
    )(x, *ws)


def _fcum_kernel(f_ref, b_ref, col_ref, row_ref, carry_ref):
    j = pl.program_id(1)

    @pl.when(j == 0)
    def _():
        carry_ref[...] = jnp.zeros_like(carry_ref)

    lf = _log_sigmoid(f_ref[...] + b_ref[...])
    r = lax.broadcasted_iota(jnp.int32, (LANES, LANES), 0)
    c = lax.broadcasted_iota(jnp.int32, (LANES, LANES), 1)
    tri = (r >= c).astype(F32)
    cs = jnp.dot(tri, lf, preferred_element_type=F32, precision=lax.Precision.HIGHEST)
    cum = cs + carry_ref[0:1, :]
    carry_ref[...] = jnp.broadcast_to(cum[LANES - 1:LANES, :], carry_ref.shape)

    f2 = cum * LOG2E
    hi = f2.astype(BF16).astype(F32)
    mid = (f2 - hi).astype(BF16).astype(F32)
    lo = (f2 - hi - mid).astype(BF16).astype(F32)
    nh = FOX_HEADS
    packed = jnp.where(c < nh, hi,
                       jnp.where(c < 2 * nh, pltpu.roll(mid, nh, axis=1),
                                 jnp.where(c < 3 * nh, pltpu.roll(lo, 2 * nh, axis=1), 0.0)))
    col_ref[...] = packed.astype(BF16)
    row_ref[0] = jnp.concatenate([hi.T[0:nh, :], mid.T[0:nh, :], lo.T[0:nh, :],
                                  jnp.zeros((nh, LANES), F32)], axis=0)


def _fcum(f_logit, b_f):
    nblk = SEQ // LANES
    return pl.pallas_call(
        _fcum_kernel,
        grid=(BATCH, nblk),
        in_specs=[pl.BlockSpec((LANES, LANES), lambda b, j: (b * nblk + j, 0)),
                  pl.BlockSpec((1, LANES), lambda b, j: (0, 0))],
        out_specs=[pl.BlockSpec((LANES, LANES), lambda b, j: (b * nblk + j, 0)),
                   pl.BlockSpec((1, 4 * FOX_HEADS, LANES), lambda b, j: (b, 0, j))],
        out_shape=[jax.ShapeDtypeStruct((TOKENS, LANES), BF16),
                   jax.ShapeDtypeStruct((BATCH, 4 * FOX_HEADS, SEQ), F32)],
        scratch_shapes=[pltpu.VMEM((HALO, LANES), F32)],
        compiler_params=_params(("arbitrary", "arbitrary")),
        name="fox_forget_cumsum",
    )(f_logit, b_f)


def _fox_kernel(q_ref, k_ref, v_ref, fpc_ref, fpr_ref, o_ref, kaug_ref, vt_ref, *, t):
    hp = pl.program_id(1)
    qi = pl.program_id(2)
    d = FOX_HEAD_DIM
    nh = FOX_HEADS
    prep_rows = 1024

    @pl.when(qi == 0)
    def _():
        r = lax.broadcasted_iota(jnp.int32, (LANES, LANES), 0)
        c = lax.broadcasted_iota(jnp.int32, (LANES, LANES), 1)
        lane = lax.broadcasted_iota(jnp.int32, (1, LANES), 1)
        ones_lanes = jnp.where(lane < 3, 1.0, 0.0)
        sub = lax.broadcasted_iota(jnp.int32, (LANES, prep_rows), 0)
        for e in (0, 1):
            h = 2 * hp + e
            pick = jnp.where((c >= 3) & (c < 6) & (r == (c - 3) * nh + h), -1.0, 0.0).astype(BF16)
            for blk in range(SEQ // prep_rows):
                rows = slice(blk * prep_rows, (blk + 1) * prep_rows)
                aug = jnp.dot(fpc_ref[rows, :], pick, preferred_element_type=F32) + ones_lanes
                kaug_ref[e, rows, :] = aug.astype(BF16)
                v_t = v_ref[rows, :].astype(F32).T
                own = (sub < d) if e == 0 else (sub >= d)
                vt_ref[e, :, rows] = jnp.where(own, v_t, 1.0).astype(BF16)

    q_t = q_ref[...].astype(F32).T
    sub_q = lax.broadcasted_iota(jnp.int32, (LANES, t), 0)
    sub8 = lax.broadcasted_iota(jnp.int32, (HALO, t), 0)
    q_aug = []
    for e in (0, 1):
        h = 2 * hp + e
        own = (sub_q < d) if e == 0 else (sub_q >= d)
        f_hi = fpr_ref[0, pl.ds(h, 1), :]
        f_mid = fpr_ref[0, pl.ds(nh + h, 1), :]
        f_lo = fpr_ref[0, pl.ds(2 * nh + h, 1), :]
        top = jnp.where(sub8 == 0, f_hi,
                        jnp.where(sub8 == 1, f_mid,
                                  jnp.where(sub8 == 2, f_lo, jnp.where(sub8 < 6, 1.0, 0.0))))
        q_aug.append(jnp.concatenate(
            [jnp.where(own, q_t, 0.0), top, jnp.zeros((LANES - HALO, t), F32)], axis=0).astype(BF16))

    key_idx = lax.broadcasted_iota(jnp.int32, (t, t), 0)
    qry_idx = lax.broadcasted_iota(jnp.int32, (t, t), 1)
    causal = key_idx <= qry_idx

    def block(j, carry, masked):
        start = pl.multiple_of(j * t, t)
        k2 = k_ref[pl.ds(start, t), :]
        out = []
        for e in (0, 1):
            m_prev, acc_prev = carry[e]
            k_aug = jnp.concatenate([k2, kaug_ref[e, pl.ds(start, t), :]], axis=1)
            s_t = jnp.dot(k_aug, q_aug[e], preferred_element_type=F32)
            if masked:
                s_t = jnp.where(causal, s_t, NEG)
            m_new = jnp.maximum(m_prev, jnp.max(s_t, axis=0, keepdims=True))
            a = jnp.exp2(m_prev - m_new)
            p_t = jnp.exp2(s_t - m_new).astype(BF16)
            pv = jnp.dot(vt_ref[e, :, pl.ds(start, t)], p_t, preferred_element_type=F32)
            out.append((m_new, a * acc_prev + pv))
        return tuple(out)

    init = tuple((jnp.full((1, t), NEG, F32), jnp.zeros((LANES, t), F32)) for _ in (0, 1))
    carry = lax.fori_loop(0, qi, lambda j, c: block(j, c, False), init)
    (_, a0), (_, a1) = block(qi, carry, True)
    o_t = jnp.concatenate([a0[0:d, :] / a0[d:d + 1, :], a1[d:, :] / a1[0:1, :]], axis=0)
    o_ref[...] = o_t.T.astype(o_ref.dtype)


def _fox_attention(qkv, fp_col, fp_row):
    t = ATTN_TILE
    nq = SEQ // t
    pairs = FOX_HEADS // 2
    return pl.pallas_call(
        functools.partial(_fox_kernel, t=t),
        grid=(BATCH, pairs, nq),
        in_specs=[pl.BlockSpec((t, LANES), lambda b, h, i: (b * nq + i, h)),
                  pl.BlockSpec((SEQ, LANES), lambda b, h, i: (b, pairs + h)),
                  pl.BlockSpec((SEQ, LANES), lambda b, h, i: (b, 2 * pairs + h)),
                  pl.BlockSpec((SEQ, LANES), lambda b, h, i: (b, 0)),
                  pl.BlockSpec((1, 4 * FOX_HEADS, t), lambda b, h, i: (b, 0, i))],
        out_specs=pl.BlockSpec((t, LANES), lambda b, h, i: (b * nq + i, h)),
        out_shape=jax.ShapeDtypeStruct((TOKENS, FOX_DIM), BF16),
        scratch_shapes=[pltpu.VMEM((2, SEQ, LANES), BF16),
                        pltpu.VMEM((2, LANES, SEQ), BF16)],
        compiler_params=_params(("arbitrary", "arbitrary", "arbitrary")),
        name="fox_attention",
    )(qkv, qkv, qkv, fp_col, fp_row)


def _even_out_kernel(gb_ref, gc_ref, hh_ref, yb_ref, h_ref, wc_ref, wa_ref, wb_ref, g_ref, b_ref,
                     o_ref, ext_ref, *, tm, tiles_per_seq):
    i = pl.program_id(0)

    @pl.when(lax.rem(i, tiles_per_seq) == 0)
    def _():
        ext_ref[0:HALO, :] = jnp.zeros((HALO, CONV_DIM), F32)

    u = gc_ref[...].astype(F32) * hh_ref[...].astype(F32)
    ext_ref[HALO:, :] = u
    conv = _causal_taps(ext_ref, u, wc_ref[...], tm)
    ext_ref[0:HALO, :] = u[tm - HALO:, :]
    ya = (gb_ref[...].astype(F32) * conv).astype(BF16)
    mix = jnp.dot(ya, wa_ref[...], preferred_element_type=F32)
    mix = mix + jnp.dot(yb_ref[...], wb_ref[...], preferred_element_type=F32)
    o_ref[...] = _layer_norm(ALPHA * h_ref[...] + mix, g_ref[...], b_ref[...])


def _even_out(conv_in, yb, h, w_conv, w_a, w_b, g, b):
    tm = ROW_TILE
    return pl.pallas_call(
        functools.partial(_even_out_kernel, tm=tm, tiles_per_seq=SEQ // tm),
        grid=(TOKENS // tm,),
        in_specs=[pl.BlockSpec((tm, CONV_DIM), lambda i: (i, 0)),
                  pl.BlockSpec((tm, CONV_DIM), lambda i: (i, 1)),
                  pl.BlockSpec((tm, CONV_DIM), lambda i: (i, 2)),
                  pl.BlockSpec((tm, FOX_DIM), lambda i: (i, 0)),
                  pl.BlockSpec((tm, D_MODEL), lambda i: (i, 0)),
                  _resident(w_conv.shape), _resident(w_a.shape), _resident(w_b.shape),
                  _resident(g.shape), _resident(b.shape)],
        out_specs=pl.BlockSpec((tm, D_MODEL), lambda i: (i, 0)),
        out_shape=jax.ShapeDtypeStruct((TOKENS, D_MODEL), F32),
        scratch_shapes=[pltpu.VMEM((tm + HALO, CONV_DIM), F32)],
        compiler_params=_params(("arbitrary",)),
        name="even_out_proj_ln",
    )(conv_in, conv_in, conv_in, yb, h, w_conv, w_a, w_b, g, b)


def _odd_out_kernel(u_ref, h_ref, w_ref, g_ref, b_ref, o_ref):
    mix = jnp.dot(u_ref[...], w_ref[...], preferred_element_type=F32)
    o_ref[...] = _layer_norm(ALPHA * h_ref[...] + mix, g_ref[...], b_ref[...])


def _odd_out(u, h, w, g, b):
    tm = ROW_TILE
    return pl.pallas_call(
        _odd_out_kernel,
        grid=(TOKENS // tm,),
        in_specs=[pl.BlockSpec((tm, SSM_INNER), lambda i: (i, 0)),
                  pl.BlockSpec((tm, D_MODEL), lambda i: (i, 0)),
                  _resident(w.shape), _resident(g.shape), _resident(b.shape)],
        out_specs=pl.BlockSpec((tm, D_MODEL), lambda i: (i, 0)),
        out_shape=jax.ShapeDtypeStruct((TOKENS, D_MODEL), F32),
        compiler_params=_params(("arbitrary",)),
        name="odd_out_proj_ln",
    )(u, h, w, g, b)


def _ffn_kernel(x_ref, p_ref, wup_ref, cw_ref, cb_ref, wdn_ref, g_ref, b_ref, wproj_ref, wgate_ref,
                bgate_ref, o_ref, halo_ref, extg_ref, extv_ref, acc_ref, *, tm, tiles_per_seq):
    i = pl.program_id(0)

    @pl.when(lax.rem(i, tiles_per_seq) == 0)
    def _():
        halo_ref[...] = jnp.zeros_like(halo_ref)

    x = x_ref[...]
    xb = x.astype(BF16)
    tf = FFN_CHUNK
    for c in range(D_FF // tf):
        branches = []
        for part, ext_ref in ((0, extg_ref), (1, extv_ref)):
            c0 = part * D_FF + c * tf
            u = jnp.dot(xb, wup_ref[:, c0:c0 + tf], preferred_element_type=F32)
            ext_ref[0:HALO, :] = halo_ref[:, c0:c0 + tf]
            ext_ref[HALO:, :] = u
            halo_ref[:, c0:c0 + tf] = u[tm - HALO:, :]
            branches.append(_causal_taps(ext_ref, u, cw_ref[:, c0:c0 + tf], tm) + cb_ref[:, c0:c0 + tf])
        act = (_silu(branches[0]) * branches[1]).astype(BF16)
        d = jnp.dot(act, wdn_ref[c * tf:(c + 1) * tf, :], preferred_element_type=F32)
        if c == 0:
            acc_ref[...] = d
        else:
            acc_ref[...] += d
    h2 = _layer_norm(ALPHA * x + acc_ref[...], g_ref[...], b_ref[...])
    gate_logit = jnp.dot(h2.astype(BF16), wgate_ref[...], preferred_element_type=F32) + bgate_ref[...]
    gate = 1.0 / (1.0 + jnp.exp(-gate_logit))
    emb = jnp.dot(p_ref[...].astype(BF16), wproj_ref[...], preferred_element_type=F32)
    o_ref[...] = h2 + gate * emb


def _ffn_ple(x, p, w_up, conv_w, conv_b, w_down, g, b, w_proj, w_gate, b_gate):
    tm = ROW_TILE
    tf = FFN_CHUNK
    return pl.pallas_call(
        functools.partial(_ffn_kernel, tm=tm, tiles_per_seq=SEQ // tm),
        grid=(TOKENS // tm,),
        in_specs=[pl.BlockSpec((tm, D_MODEL), lambda i: (i, 0)),
                  pl.BlockSpec((tm, PLE_DIM), lambda i: (i, 0)),
                  _resident(w_up.shape), _resident(conv_w.shape), _resident(conv_b.shape),
                  _resident(w_down.shape), _resident(g.shape), _resident(b.shape),
                  _resident(w_proj.shape), _resident(w_gate.shape), _resident(b_gate.shape)],
        out_specs=pl.BlockSpec((tm, D_MODEL), lambda i: (i, 0)),
        out_shape=jax.ShapeDtypeStruct((TOKENS, D_MODEL), F32),
        scratch_shapes=[pltpu.VMEM((HALO, 2 * D_FF), F32),
                        pltpu.VMEM((tm + HALO, tf), F32),
                        pltpu.VMEM((tm + HALO, tf), F32),
                        pltpu.VMEM((tm, D_MODEL), F32)],
        compiler_params=_params(("arbitrary",)),
        name="conv_ffn_ln_ple",
    )(x, p, w_up, conv_w, conv_b, w_down, g, b, w_proj, w_gate, b_gate)


def _ssd_kernel(z_ref, xbc_ref, dt_ref, cw_ref, cb_ref, dtb_ref, alog_ref, dsk_ref, ng_ref, o_ref,
                ext_ref, xc_ref, s_ref, u_ref):
    q = SSM_CHUNK
    n = SSM_STATE
    c = pl.program_id(1)

    @pl.when(c == 0)
    def _():
        ext_ref[0:HALO, :] = jnp.zeros((HALO, SSM_CONV_CH), F32)
        s_ref[...] = jnp.zeros_like(s_ref)

    xr = xbc_ref[...].astype(F32)
    ext_ref[HALO:, :] = xr
    conv = _causal_taps(ext_ref, xr, cw_ref[...], q) + cb_ref[...]
    ext_ref[0:HALO, :] = xr[q - HALO:, :]
    xc_ref[...] = _silu(conv)

    dt = _softplus(dt_ref[...] + dtb_ref[...])
    a = dt * (-jnp.exp(alog_ref[...]))
    row = lax.broadcasted_iota(jnp.int32, (q, q), 0)
    col = lax.broadcasted_iota(jnp.int32, (q, q), 1)
    causal = row >= col
    acs = jnp.dot(causal.astype(F32), a, preferred_element_type=F32, precision=lax.Precision.HIGHEST)
    dt_t = dt.T
    acs_t = acs.T
    tot = acs_t[:, q - 1:q]
    w_t = dt_t * jnp.exp(tot - acs_t)
    eacs = jnp.exp(acs)
    dec = jnp.exp(tot)
    lo = lax.broadcasted_iota(jnp.int32, (q, LANES), 1) < SSM_HEAD_DIM

    heads_per_group = SSM_HEADS // SSM_GROUPS
    group_w = heads_per_group * SSM_HEAD_DIM
    for g in range(SSM_GROUPS):
        b0 = SSM_INNER + g * n
        c0 = SSM_INNER + SSM_GROUPS * n + g * n
        bg = xc_ref[:, b0:b0 + n]
        cg = xc_ref[:, c0:c0 + n]
        cb = lax.dot_general(cg.astype(BF16), bg.astype(BF16), (((1,), (1,)), ((), ())),
                             preferred_element_type=F32)
        bg_t = bg.T
        for pr in range(heads_per_group // 2):
            j = g * (heads_per_group // 2) + pr
            sl = slice(j * LANES, (j + 1) * LANES)
            x = xc_ref[:, sl]
            xb = x.astype(BF16)
            s_prev = s_ref[:, sl]
            rhs = jnp.concatenate([xb, s_prev.astype(BF16)], axis=0)
            ys, news, decs = [], [], []
            for e in (0, 1):
                h = 2 * j + e
                seg = acs[:, h:h + 1] - acs_t[h:h + 1, :]
                lmat = jnp.exp(jnp.where(causal, seg, NEG))
                m_h = (cb * lmat * dt_t[h:h + 1, :]).astype(BF16)
                w2 = (cg * eacs[:, h:h + 1]).astype(BF16)
                lhs = jnp.concatenate([m_h, w2], axis=1)
                ys.append(jnp.dot(lhs, rhs, preferred_element_type=F32))
                bw_t = (bg_t * w_t[h:h + 1, :]).astype(BF16)
                news.append(jnp.dot(bw_t, xb, preferred_element_type=F32))
                decs.append(jnp.broadcast_to(dec[h:h + 1, :], (n, LANES)))
            y = jnp.where(lo, ys[0], ys[1])
            s_ref[:, sl] = s_prev * jnp.where(lo, decs[0], decs[1]) + jnp.where(lo, news[0], news[1])
            y = y + dsk_ref[:, sl] * x
            u_ref[:, sl] = y * _silu(z_ref[:, sl].astype(F32))
        gs = slice(g * group_w, (g + 1) * group_w)
        ug = u_ref[:, gs]
        ms = jnp.mean(ug * ug, axis=-1, keepdims=True)
        o_ref[:, gs] = (ug * lax.rsqrt(ms + RMS_EPS) * ng_ref[:, gs]).astype(o_ref.dtype)


def _ssd(z, xbc, dt_raw, conv_w, conv_b, dt_bias, a_log, d_skip, norm_g):
    q = SSM_CHUNK
    nc = SEQ // q
    row = lambda b, c: (b * nc + c, 0)
    return pl.pallas_call(
        _ssd_kernel,
        grid=(BATCH, nc),
        in_specs=[pl.BlockSpec((q, SSM_INNER), row),
                  pl.BlockSpec((q, SSM_CONV_CH), row),
                  pl.BlockSpec((q, LANES), row),
                  _resident(conv_w.shape), _resident(conv_b.shape), _resident(dt_bias.shape),
                  _resident(a_log.shape), _resident(d_skip.shape), _resident(norm_g.shape)],
        out_specs=pl.BlockSpec((q, SSM_INNER), row),
        out_shape=jax.ShapeDtypeStruct((TOKENS, SSM_INNER), BF16),
        scratch_shapes=[pltpu.VMEM((q + HALO, SSM_CONV_CH), F32),
                        pltpu.VMEM((q, SSM_CONV_CH), F32),
                        pltpu.VMEM((SSM_STATE, SSM_INNER), F32),
                        pltpu.VMEM((q, SSM_INNER), F32)],
        compiler_params=_params(("arbitrary", "arbitrary")),
        name="mamba2_ssd",
    )(z, xbc, dt_raw, conv_w, conv_b, dt_bias, a_log, d_skip, norm_g)


def _pad_cols(w, n):
    return jnp.pad(w, ((0, 0), (0, n - w.shape[1])))


def _row(v):
    return v.reshape(1, -1)


def _even_mixer(h, w_in, b_f, w_conv, w_out, ln_g, ln_b):
    c_end = 3 * CONV_DIM
    q_end = c_end + FOX_DIM
    a_end = c_end + 3 * FOX_DIM
    w_c = w_in[:, :c_end].astype(BF16)
    w_qkv = w_in[:, c_end:a_end].astype(BF16)
    w_f = _pad_cols(w_in[:, a_end:], LANES)
    q_scale = (FOX_HEAD_DIM ** -0.5) * LOG2E
    conv_in, qkv, f_logit = _proj(h, [w_c, w_qkv, w_f], [BF16, BF16, F32], "even_in_proj",
                                  lead_scales=[(0, 1.0), (FOX_DIM, q_scale), (0, 1.0)])
    fp_col, fp_row = _fcum(f_logit, _pad_cols(_row(b_f), LANES))
    yb = _fox_attention(qkv, fp_col, fp_row)
    return _even_out(conv_in, yb, h, w_conv, w_out[:CONV_DIM].astype(BF16),
                     w_out[CONV_DIM:].astype(BF16), _row(ln_g), _row(ln_b))


def _odd_mixer(h, w_in, conv_w, conv_b, dt_bias, a_log, d_skip, norm_g, w_out, ln_g, ln_b):
    x_end = SSM_INNER + SSM_CONV_CH
    w_z = w_in[:, :SSM_INNER].astype(BF16)
    w_x = w_in[:, SSM_INNER:x_end].astype(BF16)
    w_dt = _pad_cols(w_in[:, x_end:], LANES)
    z, xbc, dt_raw = _proj(h, [w_z, w_x, w_dt], [BF16, BF16, F32], "odd_in_proj")
    u = _ssd(z, xbc, dt_raw, conv_w, _row(conv_b), _pad_cols(_row(dt_bias), LANES),
             _pad_cols(_row(a_log), LANES), _row(jnp.repeat(d_skip, SSM_HEAD_DIM)), _row(norm_g))
    return _odd_out(u, h, w_out.astype(BF16), _row(ln_g), _row(ln_b))


def kernel(x, p, even_w_in, even_b_f, even_conv_w, even_w_out, odd_w_in, odd_conv_w, odd_conv_b,
           odd_dt_bias, odd_a_log, odd_d_skip, odd_norm_g, odd_w_out, ln_mix_g, ln_mix_b, ffn_w_up,
           ffn_conv_w, ffn_conv_b, ffn_w_down, ln_ffn_g, ln_ffn_b, ple_w_proj, ple_w_gate,
           ple_b_gate):
    h = x.reshape(TOKENS, D_MODEL)
    for i in range(DEPTH):
        j = i // 2
        if i % 2 == 0:
            h = _even_mixer(h, even_w_in[j], even_b_f[j], even_conv_w[j], even_w_out[j],
                            ln_mix_g[i], ln_mix_b[i])
        else:
            h = _odd_mixer(h, odd_w_in[j], odd_conv_w[j], odd_conv_b[j], odd_dt_bias[j],
                           odd_a_log[j], odd_d_skip[j], odd_norm_g[j], odd_w_out[j],
                           ln_mix_g[i], ln_mix_b[i])
        h = _ffn_ple(h, p[i].reshape(TOKENS, PLE_DIM), ffn_w_up[i].astype(BF16), ffn_conv_w[i],
                     _row(ffn_conv_b[i]), ffn_w_down[i].astype(BF16), _row(ln_ffn_g[i]),
                     _row(ln_ffn_b[i]), ple_w_proj[i].astype(BF16), ple_w_gate[i].astype(BF16),
                     _row(ple_b_gate[i]))
    return h.reshape(BATCH, SEQ, D_MODEL)
```

```python
import functools

import jax
import jax.numpy as jnp
from jax import lax
from jax.experimental import pallas as pl
from jax.experimental.pallas import tpu as pltpu

F32 = jnp.float32
BF16 = jnp.bfloat16

D_MODEL = 1024
BATCH = 2
SEQ = 8192
DEPTH = 2
TOKENS = BATCH * SEQ

CONV_DIM = 512
CONV_WIDTH = 3
FOX_HEADS = 8
FOX_HEAD_DIM = 64
FOX_DIM = FOX_HEADS * FOX_HEAD_DIM
SSM_INNER = 2 * D_MODEL
SSM_HEAD_DIM = 64
SSM_HEADS = SSM_INNER // SSM_HEAD_DIM
SSM_GROUPS = 4
SSM_STATE = 128
SSM_CONV_WIDTH = 4
SSM_CHUNK = 128
SSM_CONV_CH = SSM_INNER + 2 * SSM_GROUPS * SSM_STATE
D_FF = 2816
FFN_CONV_WIDTH = 3
PLE_DIM = 256
LN_EPS = 1e-5
RMS_EPS = 1e-5
ALPHA = (2.0 * DEPTH) ** 0.25

LANES = 128
HALO = 8
NEG = -1e30
VMEM_LIMIT = 56 * 1024 * 1024

ROW_TILE = 512
FFN_ROW_TILE = 512
FFN_CHUNK = 256
ATTN_TILE = 512
ATTN_QUERY_SLICE = 256
LOG2E = 1.4426950408889634


def _resident(shape):
    nd = len(shape)
    return pl.BlockSpec(shape, lambda *_: (0,) * nd, pipeline_mode=pl.Buffered(1))


def _params(sem, flags=None):
    return pltpu.CompilerParams(dimension_semantics=sem, vmem_limit_bytes=VMEM_LIMIT, flags=flags)


def _silu(x):
    return x * (1.0 / (1.0 + jnp.exp(-x)))


def _softplus(x):
    return jnp.maximum(x, 0.0) + jnp.log1p(jnp.exp(-jnp.abs(x)))


def _log_sigmoid(x):
    return jnp.minimum(x, 0.0) - jnp.log1p(jnp.exp(-jnp.abs(x)))


def _layer_norm(r, g, b):
    mu = jnp.mean(r, axis=-1, keepdims=True)
    d = r - mu
    var = jnp.mean(d * d, axis=-1, keepdims=True)
    return d * lax.rsqrt(var + LN_EPS) * g + b


def _causal_taps(ext_ref, cur, w, rows):
    k_taps = w.shape[0]
    out = w[k_taps - 1:k_taps, :] * cur
    for k in range(k_taps - 1):
        off = HALO - (k_taps - 1) + k
        out = out + w[k:k + 1, :] * ext_ref[off:off + rows, :]
    return out


def _causal_taps_rolled(cur, prev, w):
    k_taps = w.shape[0]
    sub = lax.broadcasted_iota(jnp.int32, prev.shape, 0)
    out = w[k_taps - 1:k_taps, :] * cur
    for k in range(k_taps - 1):
        shift = k_taps - 1 - k
        rolled = pltpu.roll(cur, shift, axis=0)
        head = jnp.where(sub < shift, pltpu.roll(prev, shift, axis=0), rolled[0:HALO, :])
        out = out + w[k:k + 1, :] * jnp.concatenate([head, rolled[HALO:, :]], axis=0)
    return out


def _proj_kernel(x_ref, *refs, n_out, chunk, lead_scales, precise):
    w_refs, o_refs = refs[:n_out], refs[n_out:]
    x = x_ref[...]
    xb = x.astype(BF16)
    for w_ref, o_ref, (lead_cols, lead_scale), hi_lo in zip(w_refs, o_refs, lead_scales, precise):
        if hi_lo:
            n = o_ref.shape[1]
            x_lo = (x - xb.astype(F32)).astype(BF16)
            both = jnp.dot(xb, w_ref[...], preferred_element_type=F32)
            cross = jnp.dot(x_lo, w_ref[:, 0:n], preferred_element_type=F32)
            o_ref[...] = (both[:, 0:n] + both[:, n:] + cross).astype(o_ref.dtype)
            continue
        n = w_ref.shape[1]
        for c0 in range(0, n, chunk):
            c1 = min(c0 + chunk, n)
            acc = jnp.dot(xb, w_ref[:, c0:c1], preferred_element_type=F32)
            if c1 <= lead_cols:
                acc = acc * lead_scale
            o_ref[:, c0:c1] = acc.astype(o_ref.dtype)


def _split_bf16(w):
    w_hi = w.astype(BF16)
    w_lo = (w - w_hi.astype(F32)).astype(BF16)
    return jnp.concatenate([w_hi, w_lo], axis=1)


def _proj(x, ws, out_dtypes, name, lead_scales=None):
    m, k = x.shape
    tm = ROW_TILE
    chunk = 512
    if lead_scales is None:
        lead_scales = [(0, 1.0)] * len(ws)
    assert all(cols % chunk == 0 for cols, _ in lead_scales)
    precise = tuple(w.dtype == F32 for w in ws)
    widths = [w.shape[1] for w in ws]
    ws = [_split_bf16(w) if p else w for w, p in zip(ws, precise)]
    return pl.pallas_call(
        functools.partial(_proj_kernel, n_out=len(ws), chunk=chunk, lead_scales=tuple(lead_scales),
                          precise=precise),
        grid=(m // tm,),
        in_specs=[pl.BlockSpec((tm, k), lambda i: (i, 0))] + [_resident(w.shape) for w in ws],
        out_specs=[pl.BlockSpec((tm, n), lambda i: (i, 0)) for n in widths],
        out_shape=[jax.ShapeDtypeStruct((m, n), dt) for n, dt in zip(widths, out_dtypes)],
        compiler_params=_params(("arbitrary",)),
        name=name,
    )(x, *ws)


def _fcum_kernel(f_ref, b_ref, col_ref, row_ref, carry_ref, *, rows):
    j = pl.program_id(1)

    @pl.when(j == 0)
    def _():
        carry_ref[...] = jnp.zeros_like(carry_ref)

    r = lax.broadcasted_iota(jnp.int32, (LANES, LANES), 0)
    c = lax.broadcasted_iota(jnp.int32, (LANES, LANES), 1)
    tri = (r >= c).astype(F32)
    nh = FOX_HEADS
    carry = carry_ref[0:1, :]
    for blk in range(rows // LANES):
        sl = slice(blk * LANES, (blk + 1) * LANES)
        lf = _log_sigmoid(f_ref[sl, :] + b_ref[...])
        cs = jnp.dot(tri, lf, preferred_element_type=F32, precision=lax.Precision.HIGHEST)
        cum = cs + carry
        carry = cum[LANES - 1:LANES, :]
        f2 = cum * LOG2E
        hi = f2.astype(BF16).astype(F32)
        mid = (f2 - hi).astype(BF16).astype(F32)
        lo = (f2 - hi - mid).astype(BF16).astype(F32)
        packed = jnp.where(c < nh, hi,
                           jnp.where(c < 2 * nh, pltpu.roll(mid, nh, axis=1),
                                     jnp.where(c < 3 * nh, pltpu.roll(lo, 2 * nh, axis=1), 0.0)))
        col_ref[sl, :] = packed.astype(BF16)
        row_ref[0, :, sl] = jnp.concatenate([hi.T[0:nh, :], mid.T[0:nh, :], lo.T[0:nh, :],
                                             jnp.zeros((nh, LANES), F32)], axis=0)
    carry_ref[...] = jnp.broadcast_to(carry, carry_ref.shape)


def _fcum(f_logit, b_f):
    rows = ROW_TILE
    nblk = SEQ // rows
    return pl.pallas_call(
        functools.partial(_fcum_kernel, rows=rows),
        grid=(BATCH, nblk),
        in_specs=[pl.BlockSpec((rows, LANES), lambda b, j: (b * nblk + j, 0)),
                  pl.BlockSpec((1, LANES), lambda b, j: (0, 0))],
        out_specs=[pl.BlockSpec((rows, LANES), lambda b, j: (b * nblk + j, 0)),
                   pl.BlockSpec((1, 4 * FOX_HEADS, rows), lambda b, j: (b, 0, j))],
        out_shape=[jax.ShapeDtypeStruct((TOKENS, LANES), BF16),
                   jax.ShapeDtypeStruct((BATCH, 4 * FOX_HEADS, SEQ), F32)],
        scratch_shapes=[pltpu.VMEM((HALO, LANES), F32)],
        compiler_params=_params(("arbitrary", "arbitrary")),
        name="fox_forget_cumsum",
    )(f_logit, b_f)


def _fox_kernel(q_ref, k_ref, v_ref, fpc_ref, fpr_ref, o_ref, kaug_ref, vt_ref, qaug_ref,
                s0_ref, s1_ref, bm0_ref, bm1_ref, m_ref, acc_ref, *, t):
    hp = pl.program_id(1)
    qi = pl.program_id(2)
    d = FOX_HEAD_DIM
    nh = FOX_HEADS
    prep_rows = 1024

    @pl.when(qi == 0)
    def _():
        r = lax.broadcasted_iota(jnp.int32, (LANES, LANES), 0)
        c = lax.broadcasted_iota(jnp.int32, (LANES, LANES), 1)
        lane = lax.broadcasted_iota(jnp.int32, (1, LANES), 1)
        ones_lanes = jnp.where(lane < 3, 1.0, 0.0)
        sub = lax.broadcasted_iota(jnp.int32, (LANES, prep_rows), 0)
        for e in (0, 1):
            h = 2 * hp + e
            pick = jnp.where((c >= 3) & (c < 6) & (r == (c - 3) * nh + h), -1.0, 0.0).astype(BF16)
            for blk in range(SEQ // prep_rows):
                rows = slice(blk * prep_rows, (blk + 1) * prep_rows)
                aug = jnp.dot(fpc_ref[rows, :], pick, preferred_element_type=F32) + ones_lanes
                kaug_ref[e, rows, :] = aug.astype(BF16)
                v_t = v_ref[rows, :].astype(F32).T
                own = (sub < d) if e == 0 else (sub >= d)
                vt_ref[e, :, rows] = jnp.where(own, v_t, 1.0).astype(BF16)

    q_t = q_ref[...].astype(F32).T
    sub_q = lax.broadcasted_iota(jnp.int32, (LANES, t), 0)
    sub8 = lax.broadcasted_iota(jnp.int32, (HALO, t), 0)
    q_aug = []
    for e in (0, 1):
        h = 2 * hp + e
        own = (sub_q < d) if e == 0 else (sub_q >= d)
        f_hi = fpr_ref[0, pl.ds(h, 1), :]
        f_mid = fpr_ref[0, pl.ds(nh + h, 1), :]
        f_lo = fpr_ref[0, pl.ds(2 * nh + h, 1), :]
        top = jnp.where(sub8 == 0, f_hi,
                        jnp.where(sub8 == 1, f_mid,
                                  jnp.where(sub8 == 2, f_lo, jnp.where(sub8 < 6, 1.0, 0.0))))
        q_aug.append(jnp.concatenate(
            [jnp.where(own, q_t, 0.0), top, jnp.zeros((LANES - HALO, t), F32)], axis=0).astype(BF16))

    for e in (0, 1):
        qaug_ref[e] = q_aug[e]
        m_ref[e] = jnp.full((1, t), NEG, F32)
        acc_ref[e] = jnp.zeros((LANES, t), F32)

    key_idx = lax.broadcasted_iota(jnp.int32, (t, t), 0)
    qry_idx = lax.broadcasted_iota(jnp.int32, (t, t), 1)
    causal = key_idx <= qry_idx

    def score_stage(j, s_ref, bm_ref):
        start = pl.multiple_of(j * t, t)
        k2 = k_ref[pl.ds(start, t), :]
        for e in (0, 1):
            k_aug = jnp.concatenate([k2, kaug_ref[e, pl.ds(start, t), :]], axis=1)
            s_t = jnp.dot(k_aug, qaug_ref[e], preferred_element_type=F32)
            s_ref[e] = s_t
            bm_ref[e] = jnp.max(s_t, axis=0, keepdims=True)

    def softmax_stage(j, s_ref, bm_ref, masked):
        start = pl.multiple_of(j * t, t)
        for e in (0, 1):
            s_t = s_ref[e]
            if masked:
                s_t = jnp.where(causal, s_t, NEG)
                bm = jnp.max(s_t, axis=0, keepdims=True)
            else:
                bm = bm_ref[e]
            m_prev = m_ref[e]
            m_new = jnp.maximum(m_prev, bm)
            m_ref[e] = m_new
            p_t = jnp.exp2(s_t - m_new).astype(BF16)
            pv = jnp.dot(vt_ref[e, :, pl.ds(start, t)], p_t, preferred_element_type=F32)
            acc_ref[e] = jnp.exp2(m_prev - m_new) * acc_ref[e] + pv

    score_stage(0, s0_ref, bm0_ref)

    def two_blocks(i, _):
        j = 2 * i
        score_stage(j + 1, s1_ref, bm1_ref)
        softmax_stage(j, s0_ref, bm0_ref, False)
        score_stage(j + 2, s0_ref, bm0_ref)
        softmax_stage(j + 1, s1_ref, bm1_ref, False)
        return 0

    lax.fori_loop(0, lax.shift_right_logical(qi, 1), two_blocks, 0)
    odd = lax.bitwise_and(qi, 1)

    @pl.when(odd == 0)
    def _():
        softmax_stage(qi, s0_ref, bm0_ref, True)

    @pl.when(odd == 1)
    def _():
        score_stage(qi, s1_ref, bm1_ref)
        softmax_stage(qi - 1, s0_ref, bm0_ref, False)
        softmax_stage(qi, s1_ref, bm1_ref, True)

    a0 = acc_ref[0]
    a1 = acc_ref[1]
    o_t = jnp.concatenate([a0[0:d, :] / a0[d:d + 1, :], a1[d:, :] / a1[0:1, :]], axis=0)
    o_ref[...] = o_t.T.astype(o_ref.dtype)


def _fox_attention(qkv, fp_col, fp_row):
    t = ATTN_TILE
    nq = SEQ // t
    pairs = FOX_HEADS // 2
    return pl.pallas_call(
        functools.partial(_fox_kernel, t=t),
        grid=(BATCH, pairs, nq),
        in_specs=[pl.BlockSpec((t, LANES), lambda b, h, i: (b * nq + i, h)),
                  pl.BlockSpec((SEQ, LANES), lambda b, h, i: (b, pairs + h)),
                  pl.BlockSpec((SEQ, LANES), lambda b, h, i: (b, 2 * pairs + h)),
                  pl.BlockSpec((SEQ, LANES), lambda b, h, i: (b, 0)),
                  pl.BlockSpec((1, 4 * FOX_HEADS, t), lambda b, h, i: (b, 0, i))],
        out_specs=pl.BlockSpec((t, LANES), lambda b, h, i: (b * nq + i, h)),
        out_shape=jax.ShapeDtypeStruct((TOKENS, FOX_DIM), BF16),
        scratch_shapes=[pltpu.VMEM((2, SEQ, LANES), BF16),
                        pltpu.VMEM((2, LANES, SEQ), BF16),
                        pltpu.VMEM((2, 2 * LANES, t), BF16),
                        pltpu.VMEM((2, t, t), F32),
                        pltpu.VMEM((2, t, t), F32),
                        pltpu.VMEM((2, 1, t), F32),
                        pltpu.VMEM((2, 1, t), F32),
                        pltpu.VMEM((2, 1, t), F32),
                        pltpu.VMEM((2, LANES, t), F32)],
        compiler_params=_params(("arbitrary", "arbitrary", "arbitrary")),
        name="fox_attention",
    )(qkv, qkv, qkv, fp_col, fp_row)


def _even_out_kernel(gb_ref, gc_ref, hh_ref, yb_ref, h_ref, wc_ref, wa_ref, wb_ref, g_ref, b_ref,
                     o_ref, ext_ref, *, tm, tiles_per_seq):
    i = pl.program_id(0)

    @pl.when(lax.rem(i, tiles_per_seq) == 0)
    def _():
        ext_ref[0:HALO, :] = jnp.zeros((HALO, CONV_DIM), F32)

    u = gc_ref[...].astype(F32) * hh_ref[...].astype(F32)
    ext_ref[HALO:, :] = u
    conv = _causal_taps(ext_ref, u, wc_ref[...], tm)
    ext_ref[0:HALO, :] = u[tm - HALO:, :]
    ya = (gb_ref[...].astype(F32) * conv).astype(BF16)
    mix = jnp.dot(ya, wa_ref[...], preferred_element_type=F32)
    mix = mix + jnp.dot(yb_ref[...], wb_ref[...], preferred_element_type=F32)
    o_ref[...] = _layer_norm(ALPHA * h_ref[...] + mix, g_ref[...], b_ref[...])


def _even_out(conv_in, yb, h, w_conv, w_a, w_b, g, b):
    tm = ROW_TILE
    return pl.pallas_call(
        functools.partial(_even_out_kernel, tm=tm, tiles_per_seq=SEQ // tm),
        grid=(TOKENS // tm,),
        in_specs=[pl.BlockSpec((tm, CONV_DIM), lambda i: (i, 0)),
                  pl.BlockSpec((tm, CONV_DIM), lambda i: (i, 1)),
                  pl.BlockSpec((tm, CONV_DIM), lambda i: (i, 2)),
                  pl.BlockSpec((tm, FOX_DIM), lambda i: (i, 0)),
                  pl.BlockSpec((tm, D_MODEL), lambda i: (i, 0)),
                  _resident(w_conv.shape), _resident(w_a.shape), _resident(w_b.shape),
                  _resident(g.shape), _resident(b.shape)],
        out_specs=pl.BlockSpec((tm, D_MODEL), lambda i: (i, 0)),
        out_shape=jax.ShapeDtypeStruct((TOKENS, D_MODEL), F32),
        scratch_shapes=[pltpu.VMEM((tm + HALO, CONV_DIM), F32)],
        compiler_params=_params(("arbitrary",)),
        name="even_out_proj_ln",
    )(conv_in, conv_in, conv_in, yb, h, w_conv, w_a, w_b, g, b)


def _odd_out_kernel(u_ref, h_ref, w_ref, g_ref, b_ref, o_ref):
    mix = jnp.dot(u_ref[...], w_ref[...], preferred_element_type=F32)
    o_ref[...] = _layer_norm(ALPHA * h_ref[...] + mix, g_ref[...], b_ref[...])


def _odd_out(u, h, w, g, b):
    tm = ROW_TILE
    return pl.pallas_call(
        _odd_out_kernel,
        grid=(TOKENS // tm,),
        in_specs=[pl.BlockSpec((tm, SSM_INNER), lambda i: (i, 0)),
                  pl.BlockSpec((tm, D_MODEL), lambda i: (i, 0)),
                  _resident(w.shape), _resident(g.shape), _resident(b.shape)],
        out_specs=pl.BlockSpec((tm, D_MODEL), lambda i: (i, 0)),
        out_shape=jax.ShapeDtypeStruct((TOKENS, D_MODEL), F32),
        compiler_params=_params(("arbitrary",)),
        name="odd_out_proj_ln",
    )(u, h, w, g, b)


def _ffn_kernel(x_ref, p_ref, wup_ref, cw_ref, cb_ref, wdn_ref, g_ref, b_ref, wproj_ref, wgate_ref,
                bgate_ref, o_ref, halo_ref, acc_ref, *, tm, tiles_per_seq):
    i = pl.program_id(0)

    @pl.when(lax.rem(i, tiles_per_seq) == 0)
    def _():
        halo_ref[...] = jnp.zeros_like(halo_ref)

    x = x_ref[...]
    xb = x.astype(BF16)
    tf = FFN_CHUNK
    n_chunks = D_FF // tf

    def up(c):
        return [jnp.dot(xb, wup_ref[:, part * D_FF + c * tf:part * D_FF + (c + 1) * tf],
                        preferred_element_type=F32) for part in (0, 1)]

    us = up(0)
    for c in range(n_chunks):
        us_next = up(c + 1) if c + 1 < n_chunks else None
        branches = []
        for part in (0, 1):
            c0 = part * D_FF + c * tf
            u = us[part]
            prev = halo_ref[:, c0:c0 + tf]
            halo_ref[:, c0:c0 + tf] = u[tm - HALO:, :]
            branches.append(_causal_taps_rolled(u, prev, cw_ref[:, c0:c0 + tf]) + cb_ref[:, c0:c0 + tf])
        us = us_next
        act = (_silu(branches[0]) * branches[1]).astype(BF16)
        d = jnp.dot(act, wdn_ref[c * tf:(c + 1) * tf, :], preferred_element_type=F32)
        if c == 0:
            acc_ref[...] = d
        else:
            acc_ref[...] += d
    h2 = _layer_norm(ALPHA * x + acc_ref[...], g_ref[...], b_ref[...])
    gate_logit = jnp.dot(h2.astype(BF16), wgate_ref[...], preferred_element_type=F32) + bgate_ref[...]
    gate = 1.0 / (1.0 + jnp.exp(-gate_logit))
    emb = jnp.dot(p_ref[...].astype(BF16), wproj_ref[...], preferred_element_type=F32)
    o_ref[...] = h2 + gate * emb


def _ffn_ple(x, p, w_up, conv_w, conv_b, w_down, g, b, w_proj, w_gate, b_gate):
    tm = FFN_ROW_TILE
    return pl.pallas_call(
        functools.partial(_ffn_kernel, tm=tm, tiles_per_seq=SEQ // tm),
        grid=(TOKENS // tm,),
        in_specs=[pl.BlockSpec((tm, D_MODEL), lambda i: (i, 0)),
                  pl.BlockSpec((tm, PLE_DIM), lambda i: (i, 0)),
                  _resident(w_up.shape), _resident(conv_w.shape), _resident(conv_b.shape),
                  _resident(w_down.shape), _resident(g.shape), _resident(b.shape),
                  _resident(w_proj.shape), _resident(w_gate.shape), _resident(b_gate.shape)],
        out_specs=pl.BlockSpec((tm, D_MODEL), lambda i: (i, 0)),
        out_shape=jax.ShapeDtypeStruct((TOKENS, D_MODEL), F32),
        scratch_shapes=[pltpu.VMEM((HALO, 2 * D_FF), F32),
                        pltpu.VMEM((tm, D_MODEL), F32)],
        compiler_params=_params(("arbitrary",)),
        name="conv_ffn_ln_ple",
    )(x, p, w_up, conv_w, conv_b, w_down, g, b, w_proj, w_gate, b_gate)


def _ssd_kernel(z_ref, xbc_ref, dt_ref, cw_ref, cb_ref, dtb_ref, alog_ref, dsk_ref, ng_ref, o_ref,
                ext_ref, xc_ref, s_ref, u_ref):
    q = SSM_CHUNK
    n = SSM_STATE
    c = pl.program_id(1)

    @pl.when(c == 0)
    def _():
        ext_ref[0:HALO, :] = jnp.zeros((HALO, SSM_CONV_CH), F32)
        s_ref[...] = jnp.zeros_like(s_ref)

    xr = xbc_ref[...].astype(F32)
    ext_ref[HALO:, :] = xr
    conv = _causal_taps(ext_ref, xr, cw_ref[...], q) + cb_ref[...]
    ext_ref[0:HALO, :] = xr[q - HALO:, :]
    xc_ref[...] = _silu(conv)

    dt = _softplus(dt_ref[...] + dtb_ref[...])
    a = dt * (-jnp.exp(alog_ref[...]))
    row = lax.broadcasted_iota(jnp.int32, (q, q), 0)
    col = lax.broadcasted_iota(jnp.int32, (q, q), 1)
    causal = row >= col
    acs = jnp.dot(causal.astype(F32), a, preferred_element_type=F32, precision=lax.Precision.HIGHEST)
    dt_t = dt.T
    acs_t = acs.T
    tot = acs_t[:, q - 1:q]
    w_t = dt_t * jnp.exp(tot - acs_t)
    eacs = jnp.exp(acs)
    dec = jnp.exp(tot)
    lo = lax.broadcasted_iota(jnp.int32, (q, LANES), 1) < SSM_HEAD_DIM

    heads_per_group = SSM_HEADS // SSM_GROUPS
    group_w = heads_per_group * SSM_HEAD_DIM
    for g in range(SSM_GROUPS):
        b0 = SSM_INNER + g * n
        c0 = SSM_INNER + SSM_GROUPS * n + g * n
        bg = xc_ref[:, b0:b0 + n]
        cg = xc_ref[:, c0:c0 + n]
        cb = lax.dot_general(cg.astype(BF16), bg.astype(BF16), (((1,), (1,)), ((), ())),
                             preferred_element_type=F32)
        bg_t = bg.T
        for pr in range(heads_per_group // 2):
            j = g * (heads_per_group // 2) + pr
            sl = slice(j * LANES, (j + 1) * LANES)
            x = xc_ref[:, sl]
            xb = x.astype(BF16)
            s_prev = s_ref[:, sl]
            rhs = jnp.concatenate([xb, s_prev.astype(BF16)], axis=0)
            ys, news, decs = [], [], []
            for e in (0, 1):
                h = 2 * j + e
                seg = acs[:, h:h + 1] - acs_t[h:h + 1, :]
                lmat = jnp.exp(jnp.where(causal, seg, NEG))
                m_h = (cb * lmat * dt_t[h:h + 1, :]).astype(BF16)
                w2 = (cg * eacs[:, h:h + 1]).astype(BF16)
                lhs = jnp.concatenate([m_h, w2], axis=1)
                ys.append(jnp.dot(lhs, rhs, preferred_element_type=F32))
                bw_t = (bg_t * w_t[h:h + 1, :]).astype(BF16)
                news.append(jnp.dot(bw_t, xb, preferred_element_type=F32))
                decs.append(jnp.broadcast_to(dec[h:h + 1, :], (n, LANES)))
            y = jnp.where(lo, ys[0], ys[1])
            s_ref[:, sl] = s_prev * jnp.where(lo, decs[0], decs[1]) + jnp.where(lo, news[0], news[1])
            y = y + dsk_ref[:, sl] * x
            u_ref[:, sl] = y * _silu(z_ref[:, sl].astype(F32))
        gs = slice(g * group_w, (g + 1) * group_w)
        ug = u_ref[:, gs]
        ms = jnp.mean(ug * ug, axis=-1, keepdims=True)
        o_ref[:, gs] = (ug * lax.rsqrt(ms + RMS_EPS) * ng_ref[:, gs]).astype(o_ref.dtype)


def _ssd(z, xbc, dt_raw, conv_w, conv_b, dt_bias, a_log, d_skip, norm_g):
    q = SSM_CHUNK
    nc = SEQ // q
    row = lambda b, c: (b * nc + c, 0)
    return pl.pallas_call(
        _ssd_kernel,
        grid=(BATCH, nc),
        in_specs=[pl.BlockSpec((q, SSM_INNER), row),
                  pl.BlockSpec((q, SSM_CONV_CH), row),
                  pl.BlockSpec((q, LANES), row),
                  _resident(conv_w.shape), _resident(conv_b.shape), _resident(dt_bias.shape),
                  _resident(a_log.shape), _resident(d_skip.shape), _resident(norm_g.shape)],
        out_specs=pl.BlockSpec((q, SSM_INNER), row),
        out_shape=jax.ShapeDtypeStruct((TOKENS, SSM_INNER), BF16),
        scratch_shapes=[pltpu.VMEM((q + HALO, SSM_CONV_CH), F32),
                        pltpu.VMEM((q, SSM_CONV_CH), F32),
                        pltpu.VMEM((SSM_STATE, SSM_INNER), F32),
                        pltpu.VMEM((q, SSM_INNER), F32)],
        compiler_params=_params(("arbitrary", "arbitrary")),
        name="mamba2_ssd",
    )(z, xbc, dt_raw, conv_w, conv_b, dt_bias, a_log, d_skip, norm_g)


def _pad_cols(w, n):
    return jnp.pad(w, ((0, 0), (0, n - w.shape[1])))


def _row(v):
    return v.reshape(1, -1)


def _even_mixer(h, w_in, b_f, w_conv, w_out, ln_g, ln_b):
    c_end = 3 * CONV_DIM
    q_end = c_end + FOX_DIM
    a_end = c_end + 3 * FOX_DIM
    w_c = w_in[:, :c_end].astype(BF16)
    w_qkv = w_in[:, c_end:a_end].astype(BF16)
    w_f = _pad_cols(w_in[:, a_end:], LANES)
    q_scale = (FOX_HEAD_DIM ** -0.5) * LOG2E
    conv_in, qkv, f_logit = _proj(h, [w_c, w_qkv, w_f], [BF16, BF16, F32], "even_in_proj",
                                  lead_scales=[(0, 1.0), (FOX_DIM, q_scale), (0, 1.0)])
    fp_col, fp_row = _fcum(f_logit, _pad_cols(_row(b_f), LANES))
    yb = _fox_attention(qkv, fp_col, fp_row)
    return _even_out(conv_in, yb, h, w_conv, w_out[:CONV_DIM].astype(BF16),
                     w_out[CONV_DIM:].astype(BF16), _row(ln_g), _row(ln_b))


def _odd_mixer(h, w_in, conv_w, conv_b, dt_bias, a_log, d_skip, norm_g, w_out, ln_g, ln_b):
    x_end = SSM_INNER + SSM_CONV_CH
    w_z = w_in[:, :SSM_INNER].astype(BF16)
    w_x = w_in[:, SSM_INNER:x_end].astype(BF16)
    w_dt = _pad_cols(w_in[:, x_end:], LANES)
    z, xbc, dt_raw = _proj(h, [w_z, w_x, w_dt], [BF16, BF16, F32], "odd_in_proj")
    u = _ssd(z, xbc, dt_raw, conv_w, _row(conv_b), _pad_cols(_row(dt_bias), LANES),
             _pad_cols(_row(a_log), LANES), _row(jnp.repeat(d_skip, SSM_HEAD_DIM)), _row(norm_g))
    return _odd_out(u, h, w_out.astype(BF16), _row(ln_g), _row(ln_b))


def kernel(x, p, even_w_in, even_b_f, even_conv_w, even_w_out, odd_w_in, odd_conv_w, odd_conv_b,
           odd_dt_bias, odd_a_log, odd_d_skip, odd_norm_g, odd_w_out, ln_mix_g, ln_mix_b, ffn_w_up,
           ffn_conv_w, ffn_conv_b, ffn_w_down, ln_ffn_g, ln_ffn_b, ple_w_proj, ple_w_gate,
           ple_b_gate):
    h = x.reshape(TOKENS, D_MODEL)
    for i in range(DEPTH):
        j = i // 2
        if i % 2 == 0:
            h = _even_mixer(h, even_w_in[j], even_b_f[j], even_conv_w[j], even_w_out[j],
                            ln_mix_g[i], ln_mix_b[i])
        else:
            h = _odd_mixer(h, odd_w_in[j], odd_conv_w[j], odd_conv_b[j], odd_dt_bias[j],
                           odd_a_log[j], odd_d_skip[j], odd_norm_g[j], odd_w_out[j],
                           ln_mix_g[i], ln_mix_b[i])
        h = _ffn_ple(h, p[i].reshape(TOKENS, PLE_DIM), ffn_w_up[i].astype(BF16), ffn_conv_w[i],
                     _row(ffn_conv_b[i]), ffn_w_down[i].astype(BF16), _row(ln_ffn_g[i]),
                     _row(ln_ffn_b[i]), ple_w_proj[i].astype(BF16), ple_w_gate[i].astype(BF16),
                     _row(ple_b_gate[i]))
    return h.reshape(BATCH, SEQ, D_MODEL)
```

```python
import functools

import jax
import jax.numpy as jnp
from jax import lax
from jax.experimental import pallas as pl
from jax.experimental.pallas import tpu as pltpu

F32 = jnp.float32
BF16 = jnp.bfloat16

D_MODEL = 1024
BATCH = 2
SEQ = 8192
DEPTH = 2
TOKENS = BATCH * SEQ

CONV_DIM = 512
CONV_WIDTH = 3
FOX_HEADS = 8
FOX_HEAD_DIM = 64
FOX_DIM = FOX_HEADS * FOX_HEAD_DIM
SSM_INNER = 2 * D_MODEL
SSM_HEAD_DIM = 64
SSM_HEADS = SSM_INNER // SSM_HEAD_DIM
SSM_GROUPS = 4
SSM_STATE = 128
SSM_CONV_WIDTH = 4
SSM_CHUNK = 128
SSM_CONV_CH = SSM_INNER + 2 * SSM_GROUPS * SSM_STATE
D_FF = 2816
FFN_CONV_WIDTH = 3
PLE_DIM = 256
LN_EPS = 1e-5
RMS_EPS = 1e-5
ALPHA = (2.0 * DEPTH) ** 0.25

LANES = 128
HALO = 8
NEG = -1e30
VMEM_LIMIT = 56 * 1024 * 1024

ROW_TILE = 512
SSD_CONV_COLS = 512
FFN_ROW_TILE = 512
FFN_CHUNK = 256
FFN_DOWN_GROUP_ENDS = (6, 11)
ATTN_TILE = 512
ATTN_QUERY_SLICE = 256
LOG2E = 1.4426950408889634


def _resident(shape):
    nd = len(shape)
    return pl.BlockSpec(shape, lambda *_: (0,) * nd, pipeline_mode=pl.Buffered(1))


def _params(sem, flags=None):
    return pltpu.CompilerParams(dimension_semantics=sem, vmem_limit_bytes=VMEM_LIMIT, flags=flags)


def _sigmoid(x):
    return 0.5 + 0.5 * jnp.tanh(0.5 * x)


def _silu(x):
    h = 0.5 * x
    return h + h * jnp.tanh(h)


def _softplus(x):
    return jnp.maximum(x, 0.0) + jnp.log1p(jnp.exp(-jnp.abs(x)))


def _log_sigmoid(x):
    return jnp.minimum(x, 0.0) - jnp.log1p(jnp.exp(-jnp.abs(x)))


def _layer_norm(r, g, b):
    mu = jnp.mean(r, axis=-1, keepdims=True)
    d = r - mu
    var = jnp.mean(d * d, axis=-1, keepdims=True)
    return d * lax.rsqrt(var + LN_EPS) * g + b


def _causal_taps_rolled(cur, prev, w):
    k_taps = w.shape[0]
    sub = lax.broadcasted_iota(jnp.int32, prev.shape, 0)
    out = w[k_taps - 1:k_taps, :] * cur
    for k in range(k_taps - 1):
        shift = k_taps - 1 - k
        rolled = pltpu.roll(cur, shift, axis=0)
        head = jnp.where(sub < shift, pltpu.roll(prev, shift, axis=0), rolled[0:HALO, :])
        out = out + w[k:k + 1, :] * jnp.concatenate([head, rolled[HALO:, :]], axis=0)
    return out


def _proj_kernel(x_ref, *refs, n_out, chunk, lead_scales, precise):
    w_refs, o_refs = refs[:n_out], refs[n_out:]
    x = x_ref[...]
    xb = x.astype(BF16)
    for w_ref, o_ref, (lead_cols, lead_scale), hi_lo in zip(w_refs, o_refs, lead_scales, precise):
        if hi_lo:
            n = o_ref.shape[1]
            x_lo = (x - xb.astype(F32)).astype(BF16)
            both = jnp.dot(xb, w_ref[...], preferred_element_type=F32)
            cross = jnp.dot(x_lo, w_ref[:, 0:n], preferred_element_type=F32)
            o_ref[...] = (both[:, 0:n] + both[:, n:] + cross).astype(o_ref.dtype)
            continue
        n = w_ref.shape[1]
        for c0 in range(0, n, chunk):
            c1 = min(c0 + chunk, n)
            acc = jnp.dot(xb, w_ref[:, c0:c1], preferred_element_type=F32)
            if c1 <= lead_cols:
                acc = acc * lead_scale
            o_ref[:, c0:c1] = acc.astype(o_ref.dtype)


def _split_bf16(w):
    w_hi = w.astype(BF16)
    w_lo = (w - w_hi.astype(F32)).astype(BF16)
    return jnp.concatenate([w_hi, w_lo], axis=1)


def _proj(x, ws, out_dtypes, name, lead_scales=None):
    m, k = x.shape
    tm = ROW_TILE
    chunk = 512
    if lead_scales is None:
        lead_scales = [(0, 1.0)] * len(ws)
    assert all(cols % chunk == 0 for cols, _ in lead_scales)
    precise = tuple(w.dtype == F32 for w in ws)
    widths = [w.shape[1] for w in ws]
    ws = [_split_bf16(w) if p else w for w, p in zip(ws, precise)]
    return pl.pallas_call(
        functools.partial(_proj_kernel, n_out=len(ws), chunk=chunk, lead_scales=tuple(lead_scales),
                          precise=precise),
        grid=(m // tm,),
        in_specs=[pl.BlockSpec((tm, k), lambda i: (i, 0))] + [_resident(w.shape) for w in ws],
        out_specs=[pl.BlockSpec((tm, n), lambda i: (i, 0)) for n in widths],
        out_shape=[jax.ShapeDtypeStruct((m, n), dt) for n, dt in zip(widths, out_dtypes)],
        compiler_params=_params(("arbitrary",)),
        name=name,
    )(x, *ws)


def _fcum_kernel(f_ref, b_ref, col_ref, row_ref, carry_ref, *, rows):
    j = pl.program_id(1)

    @pl.when(j == 0)
    def _():
        carry_ref[...] = jnp.zeros_like(carry_ref)

    r = lax.broadcasted_iota(jnp.int32, (LANES, LANES), 0)
    c = lax.broadcasted_iota(jnp.int32, (LANES, LANES), 1)
    tri = (r >= c).astype(F32)
    nh = FOX_HEADS
    carry = carry_ref[0:1, :]
    for blk in range(rows // LANES):
        sl = slice(blk * LANES, (blk + 1) * LANES)
        lf = _log_sigmoid(f_ref[sl, :] + b_ref[...])
        cs = jnp.dot(tri, lf, preferred_element_type=F32, precision=lax.Precision.HIGHEST)
        cum = cs + carry
        carry = cum[LANES - 1:LANES, :]
        f2 = cum * LOG2E
        hi = f2.astype(BF16).astype(F32)
        mid = (f2 - hi).astype(BF16).astype(F32)
        lo = (f2 - hi - mid).astype(BF16).astype(F32)
        packed = jnp.where(c < nh, hi,
                           jnp.where(c < 2 * nh, pltpu.roll(mid, nh, axis=1),
                                     jnp.where(c < 3 * nh, pltpu.roll(lo, 2 * nh, axis=1), 0.0)))
        col_ref[sl, :] = packed.astype(BF16)
        row_ref[0, :, sl] = jnp.concatenate([hi.T[0:nh, :], mid.T[0:nh, :], lo.T[0:nh, :],
                                             jnp.zeros((nh, LANES), F32)], axis=0)
    carry_ref[...] = jnp.broadcast_to(carry, carry_ref.shape)


def _fcum(f_logit, b_f):
    rows = ROW_TILE
    nblk = SEQ // rows
    return pl.pallas_call(
        functools.partial(_fcum_kernel, rows=rows),
        grid=(BATCH, nblk),
        in_specs=[pl.BlockSpec((rows, LANES), lambda b, j: (b * nblk + j, 0)),
                  pl.BlockSpec((1, LANES), lambda b, j: (0, 0))],
        out_specs=[pl.BlockSpec((rows, LANES), lambda b, j: (b * nblk + j, 0)),
                   pl.BlockSpec((1, 4 * FOX_HEADS, rows), lambda b, j: (b, 0, j))],
        out_shape=[jax.ShapeDtypeStruct((TOKENS, LANES), BF16),
                   jax.ShapeDtypeStruct((BATCH, 4 * FOX_HEADS, SEQ), F32)],
        scratch_shapes=[pltpu.VMEM((HALO, LANES), F32)],
        compiler_params=_params(("arbitrary", "arbitrary")),
        name="fox_forget_cumsum",
    )(f_logit, b_f)


def _fox_kernel(q_ref, k_ref, v_ref, fpc_ref, fpr_ref, o_ref, kaug_ref, vt_ref, qaug_ref,
                s0_ref, s1_ref, bm0_ref, bm1_ref, m_ref, acc_ref, *, t):
    hp = pl.program_id(1)
    qi = pl.program_id(2)
    d = FOX_HEAD_DIM
    nh = FOX_HEADS
    prep_rows = 1024

    @pl.when(qi == 0)
    def _():
        r = lax.broadcasted_iota(jnp.int32, (LANES, LANES), 0)
        c = lax.broadcasted_iota(jnp.int32, (LANES, LANES), 1)
        lane = lax.broadcasted_iota(jnp.int32, (1, LANES), 1)
        ones_lanes = jnp.where(lane < 3, 1.0, 0.0)
        sub = lax.broadcasted_iota(jnp.int32, (LANES, prep_rows), 0)
        for e in (0, 1):
            h = 2 * hp + e
            pick = jnp.where((c >= 3) & (c < 6) & (r == (c - 3) * nh + h), -1.0, 0.0).astype(BF16)
            for blk in range(SEQ // prep_rows):
                rows = slice(blk * prep_rows, (blk + 1) * prep_rows)
                aug = jnp.dot(fpc_ref[rows, :], pick, preferred_element_type=F32) + ones_lanes
                kaug_ref[e, rows, :] = aug.astype(BF16)
                v_t = v_ref[rows, :].astype(F32).T
                own = (sub < d) if e == 0 else (sub >= d)
                vt_ref[e, :, rows] = jnp.where(own, v_t, 1.0).astype(BF16)

    q_t = q_ref[...].astype(F32).T
    sub_q = lax.broadcasted_iota(jnp.int32, (LANES, t), 0)
    sub8 = lax.broadcasted_iota(jnp.int32, (HALO, t), 0)
    q_aug = []
    for e in (0, 1):
        h = 2 * hp + e
        own = (sub_q < d) if e == 0 else (sub_q >= d)
        f_hi = fpr_ref[0, pl.ds(h, 1), :]
        f_mid = fpr_ref[0, pl.ds(nh + h, 1), :]
        f_lo = fpr_ref[0, pl.ds(2 * nh + h, 1), :]
        top = jnp.where(sub8 == 0, f_hi,
                        jnp.where(sub8 == 1, f_mid,
                                  jnp.where(sub8 == 2, f_lo, jnp.where(sub8 < 6, 1.0, 0.0))))
        q_aug.append(jnp.concatenate(
            [jnp.where(own, q_t, 0.0), top, jnp.zeros((LANES - HALO, t), F32)], axis=0).astype(BF16))

    for e in (0, 1):
        qaug_ref[e] = q_aug[e]
        m_ref[e] = jnp.full((1, t), NEG, F32)
        acc_ref[e] = jnp.zeros((LANES, t), F32)

    key_idx = lax.broadcasted_iota(jnp.int32, (t, t), 0)
    qry_idx = lax.broadcasted_iota(jnp.int32, (t, t), 1)
    causal = key_idx <= qry_idx

    def score_stage(j, s_ref, bm_ref):
        start = pl.multiple_of(j * t, t)
        k2 = k_ref[pl.ds(start, t), :]
        for e in (0, 1):
            k_aug = jnp.concatenate([k2, kaug_ref[e, pl.ds(start, t), :]], axis=1)
            s_t = jnp.dot(k_aug, qaug_ref[e], preferred_element_type=F32)
            s_ref[e] = s_t
            bm_ref[e] = jnp.max(s_t, axis=0, keepdims=True)

    def softmax_stage(j, s_ref, bm_ref, masked):
        start = pl.multiple_of(j * t, t)
        for e in (0, 1):
            s_t = s_ref[e]
            if masked:
                s_t = jnp.where(causal, s_t, NEG)
                bm = jnp.max(s_t, axis=0, keepdims=True)
            else:
                bm = bm_ref[e]
            m_prev = m_ref[e]
            m_new = jnp.maximum(m_prev, bm)
            m_ref[e] = m_new
            p_t = jnp.exp2(s_t - m_new).astype(BF16)
            pv = jnp.dot(vt_ref[e, :, pl.ds(start, t)], p_t, preferred_element_type=F32)
            acc_ref[e] = jnp.exp2(m_prev - m_new) * acc_ref[e] + pv

    score_stage(0, s0_ref, bm0_ref)

    def two_blocks(i, _):
        j = 2 * i
        score_stage(j + 1, s1_ref, bm1_ref)
        softmax_stage(j, s0_ref, bm0_ref, False)
        score_stage(j + 2, s0_ref, bm0_ref)
        softmax_stage(j + 1, s1_ref, bm1_ref, False)
        return 0

    lax.fori_loop(0, lax.shift_right_logical(qi, 1), two_blocks, 0)
    odd = lax.bitwise_and(qi, 1)

    @pl.when(odd == 0)
    def _():
        softmax_stage(qi, s0_ref, bm0_ref, True)

    @pl.when(odd == 1)
    def _():
        score_stage(qi, s1_ref, bm1_ref)
        softmax_stage(qi - 1, s0_ref, bm0_ref, False)
        softmax_stage(qi, s1_ref, bm1_ref, True)

    a0 = acc_ref[0]
    a1 = acc_ref[1]
    o_t = jnp.concatenate([a0[0:d, :] / a0[d:d + 1, :], a1[d:, :] / a1[0:1, :]], axis=0)
    o_ref[...] = o_t.T.astype(o_ref.dtype)


def _fox_attention(qkv, fp_col, fp_row):
    t = ATTN_TILE
    nq = SEQ // t
    pairs = FOX_HEADS // 2
    return pl.pallas_call(
        functools.partial(_fox_kernel, t=t),
        grid=(BATCH, pairs, nq),
        in_specs=[pl.BlockSpec((t, LANES), lambda b, h, i: (b * nq + i, h)),
                  pl.BlockSpec((SEQ, LANES), lambda b, h, i: (b, pairs + h)),
                  pl.BlockSpec((SEQ, LANES), lambda b, h, i: (b, 2 * pairs + h)),
                  pl.BlockSpec((SEQ, LANES), lambda b, h, i: (b, 0)),
                  pl.BlockSpec((1, 4 * FOX_HEADS, t), lambda b, h, i: (b, 0, i))],
        out_specs=pl.BlockSpec((t, LANES), lambda b, h, i: (b * nq + i, h)),
        out_shape=jax.ShapeDtypeStruct((TOKENS, FOX_DIM), BF16),
        scratch_shapes=[pltpu.VMEM((2, SEQ, LANES), BF16),
                        pltpu.VMEM((2, LANES, SEQ), BF16),
                        pltpu.VMEM((2, 2 * LANES, t), BF16),
                        pltpu.VMEM((2, t, t), F32),
                        pltpu.VMEM((2, t, t), F32),
                        pltpu.VMEM((2, 1, t), F32),
                        pltpu.VMEM((2, 1, t), F32),
                        pltpu.VMEM((2, 1, t), F32),
                        pltpu.VMEM((2, LANES, t), F32)],
        compiler_params=_params(("arbitrary", "arbitrary", "arbitrary")),
        name="fox_attention",
    )(qkv, qkv, qkv, fp_col, fp_row)


def _even_mix(first_tile, gb_ref, gc_ref, hh_ref, yb_ref, wc_ref, wa_ref, wb_ref, chalo_ref):
    @pl.when(first_tile)
    def _():
        chalo_ref[...] = jnp.zeros_like(chalo_ref)

    u = gc_ref[...].astype(F32) * hh_ref[...].astype(F32)
    prev = chalo_ref[...]
    chalo_ref[...] = u[u.shape[0] - HALO:, :]
    ya = (gb_ref[...].astype(F32) * _causal_taps_rolled(u, prev, wc_ref[...])).astype(BF16)
    mix = jnp.dot(ya, wa_ref[...], preferred_element_type=F32)
    return mix + jnp.dot(yb_ref[...], wb_ref[...], preferred_element_type=F32)


def _odd_mix(first_tile, u_ref, w_ref):
    return jnp.dot(u_ref[...], w_ref[...], preferred_element_type=F32)


def _tail_kernel(*refs, mix_fn, n_mix, tm, tiles_per_seq):
    mix_refs = refs[:n_mix]
    (h_ref, p_ref, g1_ref, b1_ref, wup_ref, cw_ref, cb_ref, wdn_ref, g_ref, b_ref, wproj_ref, wgate_ref,
     bgate_ref, o_ref, halo_ref, act_ref, acc_ref) = refs[n_mix:n_mix + 17]
    mix_scratch = refs[n_mix + 17:]
    i = pl.program_id(0)
    first_tile = lax.rem(i, tiles_per_seq) == 0

    @pl.when(first_tile)
    def _():
        halo_ref[...] = jnp.zeros_like(halo_ref)

    mix = mix_fn(first_tile, *mix_refs, *mix_scratch)
    x = _layer_norm(ALPHA * h_ref[...] + mix, g1_ref[...], b1_ref[...])
    xb = x.astype(BF16)
    tf = FFN_CHUNK
    n_chunks = D_FF // tf

    def up(c):
        return [jnp.dot(xb, wup_ref[:, part * D_FF + c * tf:part * D_FF + (c + 1) * tf],
                        preferred_element_type=F32) for part in (0, 1)]

    us = up(0)
    group_start = 0
    for c in range(n_chunks):
        us_next = up(c + 1) if c + 1 < n_chunks else None
        branches = []
        for part in (0, 1):
            c0 = part * D_FF + c * tf
            u = us[part]
            prev = halo_ref[:, c0:c0 + tf]
            halo_ref[:, c0:c0 + tf] = u[tm - HALO:, :]
            branches.append(_causal_taps_rolled(u, prev, cw_ref[:, c0:c0 + tf]) + cb_ref[:, c0:c0 + tf])
        us = us_next
        act_ref[:, c * tf:(c + 1) * tf] = (_silu(branches[0]) * branches[1]).astype(BF16)
        if (c + 1) in FFN_DOWN_GROUP_ENDS:
            k0, k1 = group_start * tf, (c + 1) * tf
            d = jnp.dot(act_ref[:, k0:k1], wdn_ref[k0:k1, :], preferred_element_type=F32)
            if group_start == 0:
                acc_ref[...] = d
            else:
                acc_ref[...] += d
            group_start = c + 1
    h2 = _layer_norm(ALPHA * x + acc_ref[...], g_ref[...], b_ref[...])
    gate_logit = jnp.dot(h2.astype(BF16), wgate_ref[...], preferred_element_type=F32) + bgate_ref[...]
    emb = jnp.dot(p_ref[...].astype(BF16), wproj_ref[...], preferred_element_type=F32)
    o_ref[...] = h2 + _sigmoid(gate_logit) * emb


def _layer_tail(name, mix_fn, mix_tiles, mix_weights, mix_scratch, h, p, ln1, ffn):
    tm = FFN_ROW_TILE
    tile_specs = [pl.BlockSpec((tm, width), functools.partial(lambda i, col: (i, col), col=col))
                  for _, width, col in mix_tiles]
    mix_arrays = [a for a, _, _ in mix_tiles] + list(mix_weights)
    rest = [h, p, *ln1, *ffn]
    return pl.pallas_call(
        functools.partial(_tail_kernel, mix_fn=mix_fn, n_mix=len(mix_arrays), tm=tm,
                          tiles_per_seq=SEQ // tm),
        grid=(TOKENS // tm,),
        in_specs=(tile_specs + [_resident(w.shape) for w in mix_weights]
                  + [pl.BlockSpec((tm, D_MODEL), lambda i: (i, 0)),
                     pl.BlockSpec((tm, PLE_DIM), lambda i: (i, 0))]
                  + [_resident(a.shape) for a in rest[2:]]),
        out_specs=pl.BlockSpec((tm, D_MODEL), lambda i: (i, 0)),
        out_shape=jax.ShapeDtypeStruct((TOKENS, D_MODEL), F32),
        scratch_shapes=[pltpu.VMEM((HALO, 2 * D_FF), F32),
                        pltpu.VMEM((tm, D_FF), BF16),
                        pltpu.VMEM((tm, D_MODEL), F32)] + list(mix_scratch),
        compiler_params=_params(("arbitrary",)),
        name=name,
    )(*mix_arrays, *rest)


def _shift_select(q, taps):
    r = jnp.arange((taps - 1) * q)[:, None]
    c = jnp.arange(2 * q)[None, :]
    return (c == q + r % q - (r // q + 1)).astype(BF16)


def _ssd_kernel(z_ref, xbc_ref, dt_ref, sel_ref, cw_ref, cb_ref, dtb_ref, alog_ref, dsk_ref, ng_ref,
                o_ref, xprev_ref, xc_ref, s_ref, u_ref):
    q = SSM_CHUNK
    n = SSM_STATE
    c = pl.program_id(1)

    @pl.when(c == 0)
    def _():
        xprev_ref[...] = jnp.zeros_like(xprev_ref)
        s_ref[...] = jnp.zeros_like(s_ref)

    cw = cw_ref[...]
    taps = SSM_CONV_WIDTH
    sel = sel_ref[...]
    for c0 in range(0, SSM_CONV_CH, SSD_CONV_COLS):
        cols = slice(c0, c0 + SSD_CONV_COLS)
        x_cur = xbc_ref[:, cols]
        shifted = jnp.dot(sel, jnp.concatenate([xprev_ref[:, cols], x_cur], axis=0),
                          preferred_element_type=F32)
        conv = cw[taps - 1:taps, cols] * x_cur.astype(F32) + cb_ref[:, cols]
        for k in range(1, taps):
            conv = conv + cw[taps - 1 - k:taps - k, cols] * shifted[(k - 1) * q:k * q, :]
        xc_ref[:, cols] = _silu(conv)
    xprev_ref[...] = xbc_ref[...]

    dt = _softplus(dt_ref[...] + dtb_ref[...])
    a = dt * (-jnp.exp(alog_ref[...]))
    row = lax.broadcasted_iota(jnp.int32, (q, q), 0)
    col = lax.broadcasted_iota(jnp.int32, (q, q), 1)
    causal = row >= col
    acs = jnp.dot(causal.astype(F32), a, preferred_element_type=F32, precision=lax.Precision.HIGHEST)
    dt_t = dt.T
    acs_t = acs.T
    tot = acs_t[:, q - 1:q]
    w_t = dt_t * jnp.exp(tot - acs_t)
    eacs = jnp.exp(acs)
    dec = jnp.exp(tot)
    lo = lax.broadcasted_iota(jnp.int32, (q, LANES), 1) < SSM_HEAD_DIM

    heads_per_group = SSM_HEADS // SSM_GROUPS
    group_w = heads_per_group * SSM_HEAD_DIM
    for g in range(SSM_GROUPS):
        b0 = SSM_INNER + g * n
        c0 = SSM_INNER + SSM_GROUPS * n + g * n
        bg = xc_ref[:, b0:b0 + n]
        cg = xc_ref[:, c0:c0 + n]
        cb = lax.dot_general(cg.astype(BF16), bg.astype(BF16), (((1,), (1,)), ((), ())),
                             preferred_element_type=F32)
        bg_t = bg.T
        for pr in range(heads_per_group // 2):
            j = g * (heads_per_group // 2) + pr
            sl = slice(j * LANES, (j + 1) * LANES)
            x = xc_ref[:, sl]
            xb = x.astype(BF16)
            s_prev = s_ref[:, sl]
            rhs = jnp.concatenate([xb, s_prev.astype(BF16)], axis=0)
            ys, news, decs = [], [], []
            for e in (0, 1):
                h = 2 * j + e
                seg = acs[:, h:h + 1] - acs_t[h:h + 1, :]
                lmat = jnp.exp(jnp.where(causal, seg, NEG))
                m_h = (cb * lmat * dt_t[h:h + 1, :]).astype(BF16)
                w2 = (cg * eacs[:, h:h + 1]).astype(BF16)
                lhs = jnp.concatenate([m_h, w2], axis=1)
                ys.append(jnp.dot(lhs, rhs, preferred_element_type=F32))
                bw_t = (bg_t * w_t[h:h + 1, :]).astype(BF16)
                news.append(jnp.dot(bw_t, xb, preferred_element_type=F32))
                decs.append(jnp.broadcast_to(dec[h:h + 1, :], (n, LANES)))
            y = jnp.where(lo, ys[0], ys[1])
            s_ref[:, sl] = s_prev * jnp.where(lo, decs[0], decs[1]) + jnp.where(lo, news[0], news[1])
            y = y + dsk_ref[:, sl] * x
            u_ref[:, sl] = y * _silu(z_ref[:, sl].astype(F32))
        gs = slice(g * group_w, (g + 1) * group_w)
        ug = u_ref[:, gs]
        ms = jnp.mean(ug * ug, axis=-1, keepdims=True)
        o_ref[:, gs] = (ug * lax.rsqrt(ms + RMS_EPS) * ng_ref[:, gs]).astype(o_ref.dtype)


def _ssd(z, xbc, dt_raw, conv_w, conv_b, dt_bias, a_log, d_skip, norm_g):
    q = SSM_CHUNK
    nc = SEQ // q
    row = lambda b, c: (b * nc + c, 0)
    sel = _shift_select(q, SSM_CONV_WIDTH)
    return pl.pallas_call(
        _ssd_kernel,
        grid=(BATCH, nc),
        in_specs=[pl.BlockSpec((q, SSM_INNER), row),
                  pl.BlockSpec((q, SSM_CONV_CH), row),
                  pl.BlockSpec((q, LANES), row),
                  _resident(sel.shape),
                  _resident(conv_w.shape), _resident(conv_b.shape), _resident(dt_bias.shape),
                  _resident(a_log.shape), _resident(d_skip.shape), _resident(norm_g.shape)],
        out_specs=pl.BlockSpec((q, SSM_INNER), row),
        out_shape=jax.ShapeDtypeStruct((TOKENS, SSM_INNER), BF16),
        scratch_shapes=[pltpu.VMEM((q, SSM_CONV_CH), BF16),
                        pltpu.VMEM((q, SSM_CONV_CH), F32),
                        pltpu.VMEM((SSM_STATE, SSM_INNER), F32),
                        pltpu.VMEM((q, SSM_INNER), F32)],
        compiler_params=_params(("arbitrary", "arbitrary")),
        name="mamba2_ssd",
    )(z, xbc, dt_raw, sel, conv_w, conv_b, dt_bias, a_log, d_skip, norm_g)


def _pad_cols(w, n):
    return jnp.pad(w, ((0, 0), (0, n - w.shape[1])))


def _row(v):
    return v.reshape(1, -1)


def _even_mixer(h, w_in, b_f, w_conv, w_out):
    c_end = 3 * CONV_DIM
    q_end = c_end + FOX_DIM
    a_end = c_end + 3 * FOX_DIM
    w_c = w_in[:, :c_end].astype(BF16)
    w_qkv = w_in[:, c_end:a_end].astype(BF16)
    w_f = _pad_cols(w_in[:, a_end:], LANES)
    q_scale = (FOX_HEAD_DIM ** -0.5) * LOG2E
    conv_in, qkv, f_logit = _proj(h, [w_c, w_qkv, w_f], [BF16, BF16, F32], "even_in_proj",
                                  lead_scales=[(0, 1.0), (FOX_DIM, q_scale), (0, 1.0)])
    fp_col, fp_row = _fcum(f_logit, _pad_cols(_row(b_f), LANES))
    yb = _fox_attention(qkv, fp_col, fp_row)
    tiles = [(conv_in, CONV_DIM, 0), (conv_in, CONV_DIM, 1), (conv_in, CONV_DIM, 2), (yb, FOX_DIM, 0)]
    weights = [w_conv, w_out[:CONV_DIM].astype(BF16), w_out[CONV_DIM:].astype(BF16)]
    return _even_mix, tiles, weights, [pltpu.VMEM((HALO, CONV_DIM), F32)]


def _odd_mixer(h, w_in, conv_w, conv_b, dt_bias, a_log, d_skip, norm_g, w_out):
    x_end = SSM_INNER + SSM_CONV_CH
    w_z = w_in[:, :SSM_INNER].astype(BF16)
    w_x = w_in[:, SSM_INNER:x_end].astype(BF16)
    w_dt = _pad_cols(w_in[:, x_end:], LANES)
    z, xbc, dt_raw = _proj(h, [w_z, w_x, w_dt], [BF16, BF16, F32], "odd_in_proj")
    u = _ssd(z, xbc, dt_raw, conv_w, _row(conv_b), _pad_cols(_row(dt_bias), LANES),
             _pad_cols(_row(a_log), LANES), _row(jnp.repeat(d_skip, SSM_HEAD_DIM)), _row(norm_g))
    return _odd_mix, [(u, SSM_INNER, 0)], [w_out.astype(BF16)], []


def kernel(x, p, even_w_in, even_b_f, even_conv_w, even_w_out, odd_w_in, odd_conv_w, odd_conv_b,
           odd_dt_bias, odd_a_log, odd_d_skip, odd_norm_g, odd_w_out, ln_mix_g, ln_mix_b, ffn_w_up,
           ffn_conv_w, ffn_conv_b, ffn_w_down, ln_ffn_g, ln_ffn_b, ple_w_proj, ple_w_gate,
           ple_b_gate):
    h = x.reshape(TOKENS, D_MODEL)
    for i in range(DEPTH):
        j = i // 2
        if i % 2 == 0:
            name = "even_layer_tail"
            mix = _even_mixer(h, even_w_in[j], even_b_f[j], even_conv_w[j], even_w_out[j])
        else:
            name = "odd_layer_tail"
            mix = _odd_mixer(h, odd_w_in[j], odd_conv_w[j], odd_conv_b[j], odd_dt_bias[j],
                             odd_a_log[j], odd_d_skip[j], odd_norm_g[j], odd_w_out[j])
        ffn = (ffn_w_up[i].astype(BF16), ffn_conv_w[i], _row(ffn_conv_b[i]), ffn_w_down[i].astype(BF16),
               _row(ln_ffn_g[i]), _row(ln_ffn_b[i]), ple_w_proj[i].astype(BF16),
               ple_w_gate[i].astype(BF16), _row(ple_b_gate[i]))
        h = _layer_tail(name, *mix, h, p[i].reshape(TOKENS, PLE_DIM),
                        (_row(ln_mix_g[i]), _row(ln_mix_b[i])), ffn)
    return h.reshape(BATCH, SEQ, D_MODEL)
```

```python
import functools

import jax
import jax.numpy as jnp
from jax import lax
from jax.experimental import pallas as pl
from jax.experimental.pallas import tpu as pltpu

F32 = jnp.float32
BF16 = jnp.bfloat16

D_MODEL = 1024
BATCH = 2
SEQ = 8192
DEPTH = 2
TOKENS = BATCH * SEQ

CONV_DIM = 512
CONV_WIDTH = 3
FOX_HEADS = 8
FOX_HEAD_DIM = 64
FOX_DIM = FOX_HEADS * FOX_HEAD_DIM
SSM_INNER = 2 * D_MODEL
SSM_HEAD_DIM = 64
SSM_HEADS = SSM_INNER // SSM_HEAD_DIM
SSM_GROUPS = 4
SSM_STATE = 128
SSM_CONV_WIDTH = 4
SSM_CHUNK = 128
SSM_CONV_CH = SSM_INNER + 2 * SSM_GROUPS * SSM_STATE
D_FF = 2816
FFN_CONV_WIDTH = 3
PLE_DIM = 256
LN_EPS = 1e-5
RMS_EPS = 1e-5
ALPHA = (2.0 * DEPTH) ** 0.25

LANES = 128
HALO = 8
NEG = -1e30
VMEM_LIMIT = 56 * 1024 * 1024

ROW_TILE = 512
SSD_CONV_COLS = 512
SSD_CHUNKS_PER_STEP = 4
FFN_ROW_TILE = 512
FFN_CHUNK = 256
FFN_DOWN_GROUP_ENDS = (6, 11)
ATTN_TILE = 512
LOG2E = 1.4426950408889634


def _resident(shape):
    nd = len(shape)
    return pl.BlockSpec(shape, lambda *_: (0,) * nd, pipeline_mode=pl.Buffered(1))


def _resident_layer(shape, layer):
    nd = len(shape)
    return pl.BlockSpec((None,) + tuple(shape[1:]), lambda *_: (layer,) + (0,) * (nd - 1),
                        pipeline_mode=pl.Buffered(1))


def _params(sem, flags=None):
    return pltpu.CompilerParams(dimension_semantics=sem, vmem_limit_bytes=VMEM_LIMIT, flags=flags)


def _sigmoid(x):
    return 0.5 + 0.5 * jnp.tanh(0.5 * x)


def _silu(x):
    h = 0.5 * x
    return h + h * jnp.tanh(h)


def _softplus(x):
    return jnp.maximum(x, 0.0) + jnp.log1p(jnp.exp(-jnp.abs(x)))


def _log_sigmoid(x):
    return jnp.minimum(x, 0.0) - jnp.log1p(jnp.exp(-jnp.abs(x)))


def _layer_norm(r, g, b):
    mu = jnp.mean(r, axis=-1, keepdims=True)
    d = r - mu
    var = jnp.mean(d * d, axis=-1, keepdims=True)
    return d * lax.rsqrt(var + LN_EPS) * g + b


def _causal_taps_rolled(cur, prev, w):
    k_taps = w.shape[0]
    sub = lax.broadcasted_iota(jnp.int32, prev.shape, 0)
    out = w[k_taps - 1:k_taps, :] * cur
    for k in range(k_taps - 1):
        shift = k_taps - 1 - k
        rolled = pltpu.roll(cur, shift, axis=0)
        head = jnp.where(sub < shift, pltpu.roll(prev, shift, axis=0), rolled[0:HALO, :])
        out = out + w[k:k + 1, :] * jnp.concatenate([head, rolled[HALO:, :]], axis=0)
    return out


def _proj_kernel(x_ref, *refs, n_out, chunk, lead_scales, precise):
    w_refs, o_refs = refs[:n_out], refs[n_out:]
    x = x_ref[...]
    xb = x.astype(BF16)
    for w_ref, o_ref, (lead_cols, lead_scale), hi_lo in zip(w_refs, o_refs, lead_scales, precise):
        if hi_lo:
            n = o_ref.shape[1]
            x_lo = (x - xb.astype(F32)).astype(BF16)
            both = jnp.dot(xb, w_ref[...], preferred_element_type=F32)
            cross = jnp.dot(x_lo, w_ref[:, 0:n], preferred_element_type=F32)
            o_ref[...] = (both[:, 0:n] + both[:, n:] + cross).astype(o_ref.dtype)
            continue
        n = w_ref.shape[1]
        for c0 in range(0, n, chunk):
            c1 = min(c0 + chunk, n)
            acc = jnp.dot(xb, w_ref[:, c0:c1], preferred_element_type=F32)
            if c1 <= lead_cols:
                acc = acc * lead_scale
            o_ref[:, c0:c1] = acc.astype(o_ref.dtype)


def _split_bf16(w):
    w_hi = w.astype(BF16)
    w_lo = (w - w_hi.astype(F32)).astype(BF16)
    return jnp.concatenate([w_hi, w_lo], axis=1)


def _proj(x, ws, out_dtypes, name, lead_scales=None):
    m, k = x.shape
    tm = ROW_TILE
    chunk = 512
    if lead_scales is None:
        lead_scales = [(0, 1.0)] * len(ws)
    assert all(cols % chunk == 0 for cols, _ in lead_scales)
    precise = tuple(w.dtype == F32 for w in ws)
    widths = [w.shape[1] for w in ws]
    ws = [_split_bf16(w) if p else w for w, p in zip(ws, precise)]
    return pl.pallas_call(
        functools.partial(_proj_kernel, n_out=len(ws), chunk=chunk, lead_scales=tuple(lead_scales),
                          precise=precise),
        grid=(m // tm,),
        in_specs=[pl.BlockSpec((tm, k), lambda i: (i, 0))] + [_resident(w.shape) for w in ws],
        out_specs=[pl.BlockSpec((tm, n), lambda i: (i, 0)) for n in widths],
        out_shape=[jax.ShapeDtypeStruct((m, n), dt) for n, dt in zip(widths, out_dtypes)],
        compiler_params=_params(("arbitrary",)),
        name=name,
    )(x, *ws)


def _fcum_kernel(f_ref, b_ref, col_ref, row_ref, carry_ref, *, rows):
    j = pl.program_id(1)

    @pl.when(j == 0)
    def _():
        carry_ref[...] = jnp.zeros_like(carry_ref)

    r = lax.broadcasted_iota(jnp.int32, (LANES, LANES), 0)
    c = lax.broadcasted_iota(jnp.int32, (LANES, LANES), 1)
    tri = (r >= c).astype(F32)
    nh = FOX_HEADS
    carry = carry_ref[0:1, :]
    for blk in range(rows // LANES):
        sl = slice(blk * LANES, (blk + 1) * LANES)
        lf = _log_sigmoid(f_ref[sl, :] + b_ref[...])
        cs = jnp.dot(tri, lf, preferred_element_type=F32, precision=lax.Precision.HIGHEST)
        cum = cs + carry
        carry = cum[LANES - 1:LANES, :]
        f2 = cum * LOG2E
        hi = f2.astype(BF16).astype(F32)
        mid = (f2 - hi).astype(BF16).astype(F32)
        lo = (f2 - hi - mid).astype(BF16).astype(F32)
        packed = jnp.where(c < nh, hi,
                           jnp.where(c < 2 * nh, pltpu.roll(mid, nh, axis=1),
                                     jnp.where(c < 3 * nh, pltpu.roll(lo, 2 * nh, axis=1), 0.0)))
        col_ref[sl, :] = packed.astype(BF16)
        row_ref[0, :, sl] = jnp.concatenate([hi.T[0:nh, :], mid.T[0:nh, :], lo.T[0:nh, :],
                                             jnp.zeros((nh, LANES), F32)], axis=0)
    carry_ref[...] = jnp.broadcast_to(carry, carry_ref.shape)


def _fcum(f_logit, b_f):
    rows = ROW_TILE
    nblk = SEQ // rows
    return pl.pallas_call(
        functools.partial(_fcum_kernel, rows=rows),
        grid=(BATCH, nblk),
        in_specs=[pl.BlockSpec((rows, LANES), lambda b, j: (b * nblk + j, 0)),
                  pl.BlockSpec((1, LANES), lambda b, j: (0, 0))],
        out_specs=[pl.BlockSpec((rows, LANES), lambda b, j: (b * nblk + j, 0)),
                   pl.BlockSpec((1, 4 * FOX_HEADS, rows), lambda b, j: (b, 0, j))],
        out_shape=[jax.ShapeDtypeStruct((TOKENS, LANES), BF16),
                   jax.ShapeDtypeStruct((BATCH, 4 * FOX_HEADS, SEQ), F32)],
        scratch_shapes=[pltpu.VMEM((HALO, LANES), F32)],
        compiler_params=_params(("arbitrary", "arbitrary")),
        name="fox_forget_cumsum",
    )(f_logit, b_f)


def _fox_kernel(q_ref, k_ref, v_ref, fpc_ref, fpr_ref, o_ref, kaug_ref, vt_ref, qaug_ref,
                s0_ref, s1_ref, bm0_ref, bm1_ref, m_ref, acc_ref, *, t):
    hp = pl.program_id(1)
    qi = pl.program_id(2)
    d = FOX_HEAD_DIM
    nh = FOX_HEADS
    prep_rows = 1024

    @pl.when(qi == 0)
    def _():
        r = lax.broadcasted_iota(jnp.int32, (LANES, LANES), 0)
        c = lax.broadcasted_iota(jnp.int32, (LANES, LANES), 1)
        lane = lax.broadcasted_iota(jnp.int32, (1, LANES), 1)
        ones_lanes = jnp.where(lane < 3, 1.0, 0.0)
        sub = lax.broadcasted_iota(jnp.int32, (LANES, prep_rows), 0)
        for e in (0, 1):
            h = 2 * hp + e
            pick = jnp.where((c >= 3) & (c < 6) & (r == (c - 3) * nh + h), -1.0, 0.0).astype(BF16)
            for blk in range(SEQ // prep_rows):
                rows = slice(blk * prep_rows, (blk + 1) * prep_rows)
                aug = jnp.dot(fpc_ref[rows, :], pick, preferred_element_type=F32) + ones_lanes
                kaug_ref[e, rows, :] = aug.astype(BF16)
                v_t = v_ref[rows, :].astype(F32).T
                own = (sub < d) if e == 0 else (sub >= d)
                vt_ref[e, :, rows] = jnp.where(own, v_t, 1.0).astype(BF16)

    q_t = q_ref[...].astype(F32).T
    sub_q = lax.broadcasted_iota(jnp.int32, (LANES, t), 0)
    sub8 = lax.broadcasted_iota(jnp.int32, (HALO, t), 0)
    q_aug = []
    for e in (0, 1):
        h = 2 * hp + e
        own = (sub_q < d) if e == 0 else (sub_q >= d)
        f_hi = fpr_ref[0, pl.ds(h, 1), :]
        f_mid = fpr_ref[0, pl.ds(nh + h, 1), :]
        f_lo = fpr_ref[0, pl.ds(2 * nh + h, 1), :]
        top = jnp.where(sub8 == 0, f_hi,
                        jnp.where(sub8 == 1, f_mid,
                                  jnp.where(sub8 == 2, f_lo, jnp.where(sub8 < 6, 1.0, 0.0))))
        q_aug.append(jnp.concatenate(
            [jnp.where(own, q_t, 0.0), top, jnp.zeros((LANES - HALO, t), F32)], axis=0).astype(BF16))

    for e in (0, 1):
        qaug_ref[e] = q_aug[e]
        m_ref[e] = jnp.full((1, t), NEG, F32)
        acc_ref[e] = jnp.zeros((LANES, t), F32)

    key_idx = lax.broadcasted_iota(jnp.int32, (t, t), 0)
    qry_idx = lax.broadcasted_iota(jnp.int32, (t, t), 1)
    causal = key_idx <= qry_idx

    def score_stage(j, s_ref, bm_ref):
        start = pl.multiple_of(j * t, t)
        k2 = k_ref[pl.ds(start, t), :]
        for e in (0, 1):
            k_aug = jnp.concatenate([k2, kaug_ref[e, pl.ds(start, t), :]], axis=1)
            s_t = jnp.dot(k_aug, qaug_ref[e], preferred_element_type=F32)
            s_ref[e] = s_t
            bm_ref[e] = jnp.max(s_t, axis=0, keepdims=True)

    def softmax_stage(j, s_ref, bm_ref, masked):
        start = pl.multiple_of(j * t, t)
        for e in (0, 1):
            s_t = s_ref[e]
            if masked:
                s_t = jnp.where(causal, s_t, NEG)
                bm = jnp.max(s_t, axis=0, keepdims=True)
            else:
                bm = bm_ref[e]
            m_prev = m_ref[e]
            m_new = jnp.maximum(m_prev, bm)
            m_ref[e] = m_new
            p_t = jnp.exp2(s_t - m_new).astype(BF16)
            pv = jnp.dot(vt_ref[e, :, pl.ds(start, t)], p_t, preferred_element_type=F32)
            acc_ref[e] = jnp.exp2(m_prev - m_new) * acc_ref[e] + pv

    score_stage(0, s0_ref, bm0_ref)

    def two_blocks(i, _):
        j = 2 * i
        score_stage(j + 1, s1_ref, bm1_ref)
        softmax_stage(j, s0_ref, bm0_ref, False)
        score_stage(j + 2, s0_ref, bm0_ref)
        softmax_stage(j + 1, s1_ref, bm1_ref, False)
        return 0

    lax.fori_loop(0, lax.shift_right_logical(qi, 1), two_blocks, 0)
    odd = lax.bitwise_and(qi, 1)

    @pl.when(odd == 0)
    def _():
        softmax_stage(qi, s0_ref, bm0_ref, True)

    @pl.when(odd == 1)
    def _():
        score_stage(qi, s1_ref, bm1_ref)
        softmax_stage(qi - 1, s0_ref, bm0_ref, False)
        softmax_stage(qi, s1_ref, bm1_ref, True)

    a0 = acc_ref[0]
    a1 = acc_ref[1]
    o_t = jnp.concatenate([a0[0:d, :] / a0[d:d + 1, :], a1[d:, :] / a1[0:1, :]], axis=0)
    o_ref[...] = o_t.T.astype(o_ref.dtype)


def _fox_attention(qkv, fp_col, fp_row):
    t = ATTN_TILE
    nq = SEQ // t
    pairs = FOX_HEADS // 2
    return pl.pallas_call(
        functools.partial(_fox_kernel, t=t),
        grid=(BATCH, pairs, nq),
        in_specs=[pl.BlockSpec((t, LANES), lambda b, h, i: (b * nq + i, h)),
                  pl.BlockSpec((SEQ, LANES), lambda b, h, i: (b, pairs + h)),
                  pl.BlockSpec((SEQ, LANES), lambda b, h, i: (b, 2 * pairs + h)),
                  pl.BlockSpec((SEQ, LANES), lambda b, h, i: (b, 0)),
                  pl.BlockSpec((1, 4 * FOX_HEADS, t), lambda b, h, i: (b, 0, i))],
        out_specs=pl.BlockSpec((t, LANES), lambda b, h, i: (b * nq + i, h)),
        out_shape=jax.ShapeDtypeStruct((TOKENS, FOX_DIM), BF16),
        scratch_shapes=[pltpu.VMEM((2, SEQ, LANES), BF16),
                        pltpu.VMEM((2, LANES, SEQ), BF16),
                        pltpu.VMEM((2, 2 * LANES, t), BF16),
                        pltpu.VMEM((2, t, t), F32),
                        pltpu.VMEM((2, t, t), F32),
                        pltpu.VMEM((2, 1, t), F32),
                        pltpu.VMEM((2, 1, t), F32),
                        pltpu.VMEM((2, 1, t), F32),
                        pltpu.VMEM((2, LANES, t), F32)],
        compiler_params=_params(("arbitrary", "arbitrary", "arbitrary")),
        name="fox_attention",
    )(qkv, qkv, qkv, fp_col, fp_row)


def _even_mix(first_tile, gb_ref, gc_ref, hh_ref, yb_ref, wc_ref, wa_ref, wb_ref, chalo_ref):
    @pl.when(first_tile)
    def _():
        chalo_ref[...] = jnp.zeros_like(chalo_ref)

    u = gc_ref[...].astype(F32) * hh_ref[...].astype(F32)
    prev = chalo_ref[...]
    chalo_ref[...] = u[u.shape[0] - HALO:, :]
    ya = (gb_ref[...].astype(F32) * _causal_taps_rolled(u, prev, wc_ref[...])).astype(BF16)
    mix = jnp.dot(ya, wa_ref[...], preferred_element_type=F32)
    return mix + jnp.dot(yb_ref[...], wb_ref[...], preferred_element_type=F32)


def _odd_mix(first_tile, u_ref, w_ref):
    return jnp.dot(u_ref[...], w_ref[...], preferred_element_type=F32)


def _tail_kernel(*refs, mix_fn, n_mix, tm, tiles_per_seq):
    mix_refs = refs[:n_mix]
    (h_ref, p_ref, g1_ref, b1_ref, wup_ref, cw_ref, cb_ref, wdn_ref, g_ref, b_ref, wproj_ref, wgate_ref,
     bgate_ref, o_ref, halo_ref, act_ref, acc_ref) = refs[n_mix:n_mix + 17]
    mix_scratch = refs[n_mix + 17:]
    i = pl.program_id(0)
    first_tile = lax.rem(i, tiles_per_seq) == 0

    @pl.when(first_tile)
    def _():
        halo_ref[...] = jnp.zeros_like(halo_ref)

    mix = mix_fn(first_tile, *mix_refs, *mix_scratch)
    x = _layer_norm(ALPHA * h_ref[...] + mix, g1_ref[...], b1_ref[...])
    xb = x.astype(BF16)
    tf = FFN_CHUNK
    n_chunks = D_FF // tf

    def up(c):
        return [jnp.dot(xb, wup_ref[:, part * D_FF + c * tf:part * D_FF + (c + 1) * tf],
                        preferred_element_type=F32) for part in (0, 1)]

    us = up(0)
    group_start = 0
    for c in range(n_chunks):
        us_next = up(c + 1) if c + 1 < n_chunks else None
        branches = []
        for part in (0, 1):
            c0 = part * D_FF + c * tf
            u = us[part]
            prev = halo_ref[:, c0:c0 + tf]
            halo_ref[:, c0:c0 + tf] = u[tm - HALO:, :]
            branches.append(_causal_taps_rolled(u, prev, cw_ref[:, c0:c0 + tf]) + cb_ref[:, c0:c0 + tf])
        us = us_next
        act_ref[:, c * tf:(c + 1) * tf] = (_silu(branches[0]) * branches[1]).astype(BF16)
        if (c + 1) in FFN_DOWN_GROUP_ENDS:
            k0, k1 = group_start * tf, (c + 1) * tf
            d = jnp.dot(act_ref[:, k0:k1], wdn_ref[k0:k1, :], preferred_element_type=F32)
            if group_start == 0:
                acc_ref[...] = d
            else:
                acc_ref[...] += d
            group_start = c + 1
    h2 = _layer_norm(ALPHA * x + acc_ref[...], g_ref[...], b_ref[...])
    gate_logit = jnp.dot(h2.astype(BF16), wgate_ref[...], preferred_element_type=F32) + bgate_ref[...]
    emb = jnp.dot(p_ref[...].astype(BF16), wproj_ref[...], preferred_element_type=F32)
    o_ref[...] = h2 + _sigmoid(gate_logit) * emb


def _layer_tail(name, layer, mix_fn, mix_tiles, mix_weights, mix_scratch, h, p_all, stacked):
    tm = FFN_ROW_TILE
    tiles_per_layer = TOKENS // tm
    tile_specs = [pl.BlockSpec((tm, width), functools.partial(lambda i, col: (i, col), col=col))
                  for _, width, col in mix_tiles]
    mix_arrays = [a for a, _, _ in mix_tiles] + list(mix_weights)
    rest = [h, p_all, *stacked]
    return pl.pallas_call(
        functools.partial(_tail_kernel, mix_fn=mix_fn, n_mix=len(mix_arrays), tm=tm,
                          tiles_per_seq=SEQ // tm),
        grid=(TOKENS // tm,),
        in_specs=(tile_specs + [_resident(w.shape) for w in mix_weights]
                  + [pl.BlockSpec((tm, D_MODEL), lambda i: (i, 0)),
                     pl.BlockSpec((tm, PLE_DIM), lambda i: (layer * tiles_per_layer + i, 0))]
                  + [_resident_layer(a.shape, layer) for a in stacked]),
        out_specs=pl.BlockSpec((tm, D_MODEL), lambda i: (i, 0)),
        out_shape=jax.ShapeDtypeStruct((TOKENS, D_MODEL), F32),
        scratch_shapes=[pltpu.VMEM((HALO, 2 * D_FF), F32),
                        pltpu.VMEM((tm, D_FF), BF16),
                        pltpu.VMEM((tm, D_MODEL), F32)] + list(mix_scratch),
        compiler_params=_params(("arbitrary",)),
        name=name,
    )(*mix_arrays, *rest)


def _shift_select(q, taps):
    r = jnp.arange((taps - 1) * q)[:, None]
    c = jnp.arange(2 * q)[None, :]
    return (c == q + r % q - (r // q + 1)).astype(BF16)


def _ssd_kernel(z_ref, xbc_ref, dt_ref, sel_ref, cw_ref, cb_ref, dtb_ref, alog_ref, dsk_ref, ng_ref,
                o_ref, xprev_ref, xc_ref, s_ref, u_ref):
    q = SSM_CHUNK
    n = SSM_STATE

    @pl.when(pl.program_id(1) == 0)
    def _():
        xprev_ref[...] = jnp.zeros_like(xprev_ref)
        s_ref[...] = jnp.zeros_like(s_ref)

    cw = cw_ref[...]
    taps = SSM_CONV_WIDTH
    sel = sel_ref[...]
    row = lax.broadcasted_iota(jnp.int32, (q, q), 0)
    col = lax.broadcasted_iota(jnp.int32, (q, q), 1)
    causal = row >= col
    lo = lax.broadcasted_iota(jnp.int32, (q, LANES), 1) < SSM_HEAD_DIM
    heads_per_group = SSM_HEADS // SSM_GROUPS
    group_w = heads_per_group * SSM_HEAD_DIM

    for sub in range(SSD_CHUNKS_PER_STEP):
        rows = slice(sub * q, (sub + 1) * q)
        prev_rows = slice((sub - 1) * q, sub * q)

        for c0 in range(0, SSM_CONV_CH, SSD_CONV_COLS):
            cols = slice(c0, c0 + SSD_CONV_COLS)
            x_cur = xbc_ref[rows, cols]
            x_prev = xprev_ref[:, cols] if sub == 0 else xbc_ref[prev_rows, cols]
            shifted = jnp.dot(sel, jnp.concatenate([x_prev, x_cur], axis=0),
                              preferred_element_type=F32)
            conv = cw[taps - 1:taps, cols] * x_cur.astype(F32) + cb_ref[:, cols]
            for k in range(1, taps):
                conv = conv + cw[taps - 1 - k:taps - k, cols] * shifted[(k - 1) * q:k * q, :]
            xc_ref[rows, cols] = _silu(conv)

        dt = _softplus(dt_ref[rows, :] + dtb_ref[...])
        a = dt * (-LOG2E * jnp.exp(alog_ref[...]))
        acs = jnp.dot(causal.astype(F32), a, preferred_element_type=F32,
                      precision=lax.Precision.HIGHEST)
        dt_t = dt.T
        acs_t = acs.T
        tot = acs_t[:, q - 1:q]
        w_t = dt_t * jnp.exp2(tot - acs_t)
        src_t = acs_t - jnp.log2(dt_t)
        eacs = jnp.exp2(acs)
        dec = jnp.exp2(tot)

        for g in range(SSM_GROUPS):
            b0 = SSM_INNER + g * n
            c0 = SSM_INNER + SSM_GROUPS * n + g * n
            bg = xc_ref[rows, b0:b0 + n]
            cg = xc_ref[rows, c0:c0 + n]
            cb = lax.dot_general(cg.astype(BF16), bg.astype(BF16), (((1,), (1,)), ((), ())),
                                 preferred_element_type=F32)
            bg_t = bg.T
            for pr in range(heads_per_group // 2):
                j = g * (heads_per_group // 2) + pr
                sl = slice(j * LANES, (j + 1) * LANES)
                x = xc_ref[rows, sl]
                xb = x.astype(BF16)
                s_prev = s_ref[:, sl]
                rhs = jnp.concatenate([xb, s_prev.astype(BF16)], axis=0)
                ys, news, decs = [], [], []
                for e in (0, 1):
                    h = 2 * j + e
                    seg = acs[:, h:h + 1] - src_t[h:h + 1, :]
                    m_h = (cb * jnp.exp2(jnp.where(causal, seg, NEG))).astype(BF16)
                    w2 = (cg * eacs[:, h:h + 1]).astype(BF16)
                    lhs = jnp.concatenate([m_h, w2], axis=1)
                    ys.append(jnp.dot(lhs, rhs, preferred_element_type=F32))
                    bw_t = (bg_t * w_t[h:h + 1, :]).astype(BF16)
                    news.append(jnp.dot(bw_t, xb, preferred_element_type=F32))
                    decs.append(jnp.broadcast_to(dec[h:h + 1, :], (n, LANES)))
                y = jnp.where(lo, ys[0], ys[1])
                s_ref[:, sl] = (s_prev * jnp.where(lo, decs[0], decs[1])
                                + jnp.where(lo, news[0], news[1]))
                y = y + dsk_ref[:, sl] * x
                u_ref[rows, sl] = y * _silu(z_ref[rows, sl].astype(F32))
            gs = slice(g * group_w, (g + 1) * group_w)
            ug = u_ref[rows, gs]
            ms = jnp.mean(ug * ug, axis=-1, keepdims=True)
            o_ref[rows, gs] = (ug * lax.rsqrt(ms + RMS_EPS) * ng_ref[:, gs]).astype(o_ref.dtype)

    xprev_ref[...] = xbc_ref[(SSD_CHUNKS_PER_STEP - 1) * q:, :]


def _ssd(z, xbc, dt_raw, conv_w, conv_b, dt_bias, a_log, d_skip, norm_g):
    q = SSM_CHUNK
    rows = SSD_CHUNKS_PER_STEP * q
    steps = SEQ // rows
    row = lambda b, c: (b * steps + c, 0)
    sel = _shift_select(q, SSM_CONV_WIDTH)
    return pl.pallas_call(
        _ssd_kernel,
        grid=(BATCH, steps),
        in_specs=[pl.BlockSpec((rows, SSM_INNER), row),
                  pl.BlockSpec((rows, SSM_CONV_CH), row),
                  pl.BlockSpec((rows, LANES), row),
                  _resident(sel.shape),
                  _resident(conv_w.shape), _resident(conv_b.shape), _resident(dt_bias.shape),
                  _resident(a_log.shape), _resident(d_skip.shape), _resident(norm_g.shape)],
        out_specs=pl.BlockSpec((rows, SSM_INNER), row),
        out_shape=jax.ShapeDtypeStruct((TOKENS, SSM_INNER), BF16),
        scratch_shapes=[pltpu.VMEM((q, SSM_CONV_CH), BF16),
                        pltpu.VMEM((rows, SSM_CONV_CH), F32),
                        pltpu.VMEM((SSM_STATE, SSM_INNER), F32),
                        pltpu.VMEM((rows, SSM_INNER), F32)],
        compiler_params=_params(("arbitrary", "arbitrary")),
        name="mamba2_ssd",
    )(z, xbc, dt_raw, sel, conv_w, conv_b, dt_bias, a_log, d_skip, norm_g)


def _pad_cols(w, n):
    return jnp.pad(w, ((0, 0), (0, n - w.shape[1])))


def _row(v):
    return v.reshape(1, -1)


def _even_mixer(h, w_in, b_f, w_conv, w_out):
    c_end = 3 * CONV_DIM
    a_end = c_end + 3 * FOX_DIM
    w_c = w_in[:, :c_end].astype(BF16)
    w_qkv = w_in[:, c_end:a_end].astype(BF16)
    w_f = _pad_cols(w_in[:, a_end:], LANES)
    q_scale = (FOX_HEAD_DIM ** -0.5) * LOG2E
    conv_in, qkv, f_logit = _proj(h, [w_c, w_qkv, w_f], [BF16, BF16, F32], "even_in_proj",
                                  lead_scales=[(0, 1.0), (FOX_DIM, q_scale), (0, 1.0)])
    fp_col, fp_row = _fcum(f_logit, _pad_cols(_row(b_f), LANES))
    yb = _fox_attention(qkv, fp_col, fp_row)
    tiles = [(conv_in, CONV_DIM, 0), (conv_in, CONV_DIM, 1), (conv_in, CONV_DIM, 2), (yb, FOX_DIM, 0)]
    weights = [w_conv, w_out[:CONV_DIM].astype(BF16), w_out[CONV_DIM:].astype(BF16)]
    return _even_mix, tiles, weights, [pltpu.VMEM((HALO, CONV_DIM), F32)]


def _odd_mixer(h, w_in, conv_w, conv_b, dt_bias, a_log, d_skip, norm_g, w_out):
    x_end = SSM_INNER + SSM_CONV_CH
    w_z = w_in[:, :SSM_INNER].astype(BF16)
    w_x = w_in[:, SSM_INNER:x_end].astype(BF16)
    w_dt = _pad_cols(w_in[:, x_end:], LANES)
    z, xbc, dt_raw = _proj(h, [w_z, w_x, w_dt], [BF16, BF16, F32], "odd_in_proj")
    u = _ssd(z, xbc, dt_raw, conv_w, _row(conv_b), _pad_cols(_row(dt_bias), LANES),
             _pad_cols(_row(a_log), LANES), _row(jnp.repeat(d_skip, SSM_HEAD_DIM)), _row(norm_g))
    return _odd_mix, [(u, SSM_INNER, 0)], [w_out.astype(BF16)], []


def kernel(x, p, even_w_in, even_b_f, even_conv_w, even_w_out, odd_w_in, odd_conv_w, odd_conv_b,
           odd_dt_bias, odd_a_log, odd_d_skip, odd_norm_g, odd_w_out, ln_mix_g, ln_mix_b, ffn_w_up,
           ffn_conv_w, ffn_conv_b, ffn_w_down, ln_ffn_g, ln_ffn_b, ple_w_proj, ple_w_gate,
           ple_b_gate):
    h = x.reshape(TOKENS, D_MODEL)
    rows = lambda v: v.reshape(DEPTH, 1, -1)
    stacked = (rows(ln_mix_g), rows(ln_mix_b), ffn_w_up.astype(BF16), ffn_conv_w, rows(ffn_conv_b),
               ffn_w_down.astype(BF16), rows(ln_ffn_g), rows(ln_ffn_b), ple_w_proj.astype(BF16),
               ple_w_gate.astype(BF16), rows(ple_b_gate))
    p_all = p.reshape(DEPTH * TOKENS, PLE_DIM)
    for i in range(DEPTH):
        j = i // 2
        if i % 2 == 0:
            name = "even_layer_tail"
            mix = _even_mixer(h, even_w_in[j], even_b_f[j], even_conv_w[j], even_w_out[j])
        else:
            name = "odd_layer_tail"
            mix = _odd_mixer(h, odd_w_in[j], odd_conv_w[j], odd_conv_b[j], odd_dt_bias[j],
                             odd_a_log[j], odd_d_skip[j], odd_norm_g[j], odd_w_out[j])
        h = _layer_tail(name, i, *mix, h, p_all, stacked)
    return h.reshape(BATCH, SEQ, D_MODEL)
```

```python
import functools

import jax
import jax.numpy as jnp
from jax import lax
from jax.experimental import pallas as pl
from jax.experimental.pallas import tpu as pltpu

F32 = jnp.float32
BF16 = jnp.bfloat16

D_MODEL = 1024
BATCH = 2
SEQ = 8192
DEPTH = 2
TOKENS = BATCH * SEQ

CONV_DIM = 512
CONV_WIDTH = 3
FOX_HEADS = 8
FOX_HEAD_DIM = 64
FOX_DIM = FOX_HEADS * FOX_HEAD_DIM
SSM_INNER = 2 * D_MODEL
SSM_HEAD_DIM = 64
SSM_HEADS = SSM_INNER // SSM_HEAD_DIM
SSM_GROUPS = 4
SSM_STATE = 128
SSM_CONV_WIDTH = 4
SSM_CHUNK = 128
SSM_CONV_CH = SSM_INNER + 2 * SSM_GROUPS * SSM_STATE
D_FF = 2816
FFN_CONV_WIDTH = 3
PLE_DIM = 256
LN_EPS = 1e-5
RMS_EPS = 1e-5
ALPHA = (2.0 * DEPTH) ** 0.25

LANES = 128
HALO = 8
NEG = -1e30
VMEM_LIMIT = 56 * 1024 * 1024

ROW_TILE = 512
SSD_CONV_COLS = 512
SSD_CHUNKS_PER_STEP = 4
FFN_ROW_TILE = 512
FFN_CHUNK = 256
FFN_DOWN_GROUP_ENDS = (6, 11)
ATTN_TILE = 512
LOG2E = 1.4426950408889634


def _resident(shape):
    nd = len(shape)
    return pl.BlockSpec(shape, lambda *_: (0,) * nd, pipeline_mode=pl.Buffered(1))


def _resident_layer(shape, layer):
    nd = len(shape)
    return pl.BlockSpec((None,) + tuple(shape[1:]), lambda *_: (layer,) + (0,) * (nd - 1),
                        pipeline_mode=pl.Buffered(1))


def _params(sem, flags=None):
    return pltpu.CompilerParams(dimension_semantics=sem, vmem_limit_bytes=VMEM_LIMIT, flags=flags)


def _sigmoid(x):
    return 0.5 + 0.5 * jnp.tanh(0.5 * x)


def _silu(x):
    h = 0.5 * x
    return h + h * jnp.tanh(h)


def _softplus(x):
    return jnp.maximum(x, 0.0) + jnp.log1p(jnp.exp(-jnp.abs(x)))


def _log_sigmoid(x):
    return jnp.minimum(x, 0.0) - jnp.log1p(jnp.exp(-jnp.abs(x)))


def _layer_norm(r, g, b):
    mu = jnp.mean(r, axis=-1, keepdims=True)
    d = r - mu
    var = jnp.mean(d * d, axis=-1, keepdims=True)
    return d * lax.rsqrt(var + LN_EPS) * g + b


def _causal_taps_rolled(cur, prev, w):
    k_taps = w.shape[0]
    sub = lax.broadcasted_iota(jnp.int32, prev.shape, 0)
    out = w[k_taps - 1:k_taps, :] * cur
    for k in range(k_taps - 1):
        shift = k_taps - 1 - k
        rolled = pltpu.roll(cur, shift, axis=0)
        head = jnp.where(sub < shift, pltpu.roll(prev, shift, axis=0), rolled[0:HALO, :])
        out = out + w[k:k + 1, :] * jnp.concatenate([head, rolled[HALO:, :]], axis=0)
    return out


def _dot_nt(a, b_t):
    return lax.dot_general(a, b_t, (((1,), (1,)), ((), ())), preferred_element_type=F32)


def _proj_kernel(x_ref, *refs, n_out, chunk, lead_scales, precise):
    w_refs, o_refs = refs[:n_out], refs[n_out:]
    x = x_ref[...]
    xb = x.astype(BF16)
    for w_ref, o_ref, (lead_cols, lead_scale), hi_lo in zip(w_refs, o_refs, lead_scales, precise):
        n = o_ref.shape[1]
        if hi_lo:
            x_lo = (x - xb.astype(F32)).astype(BF16)
            both = _dot_nt(xb, w_ref[...])
            cross = _dot_nt(x_lo, w_ref[0:n, :])
            o_ref[...] = (both[:, 0:n] + both[:, n:] + cross).astype(o_ref.dtype)
            continue
        for c0 in range(0, n, chunk):
            c1 = min(c0 + chunk, n)
            acc = _dot_nt(xb, w_ref[c0:c1, :])
            if c1 <= lead_cols:
                acc = acc * lead_scale
            o_ref[:, c0:c1] = acc.astype(o_ref.dtype)


def _split_bf16(w_t):
    w_hi = w_t.astype(BF16)
    w_lo = (w_t - w_hi.astype(F32)).astype(BF16)
    return jnp.concatenate([w_hi, w_lo], axis=0)


def _proj(x, ws_t, out_dtypes, name, lead_scales=None):
    m, k = x.shape
    tm = ROW_TILE
    chunk = 512
    if lead_scales is None:
        lead_scales = [(0, 1.0)] * len(ws_t)
    assert all(cols % chunk == 0 for cols, _ in lead_scales)
    precise = tuple(w.dtype == F32 for w in ws_t)
    widths = [w.shape[0] for w in ws_t]
    ws = [_split_bf16(w) if p else w for w, p in zip(ws_t, precise)]
    return pl.pallas_call(
        functools.partial(_proj_kernel, n_out=len(ws), chunk=chunk, lead_scales=tuple(lead_scales),
                          precise=precise),
        grid=(m // tm,),
        in_specs=[pl.BlockSpec((tm, k), lambda i: (i, 0))] + [_resident(w.shape) for w in ws],
        out_specs=[pl.BlockSpec((tm, n), lambda i: (i, 0)) for n in widths],
        out_shape=[jax.ShapeDtypeStruct((m, n), dt) for n, dt in zip(widths, out_dtypes)],
        compiler_params=_params(("arbitrary",)),
        name=name,
    )(x, *ws)


def _fcum_kernel(f_ref, b_ref, col_ref, row_ref, carry_ref, *, rows):
    j = pl.program_id(1)

    @pl.when(j == 0)
    def _():
        carry_ref[...] = jnp.zeros_like(carry_ref)

    r = lax.broadcasted_iota(jnp.int32, (LANES, LANES), 0)
    c = lax.broadcasted_iota(jnp.int32, (LANES, LANES), 1)
    tri = (r >= c).astype(F32)
    nh = FOX_HEADS
    carry = carry_ref[0:1, :]
    for blk in range(rows // LANES):
        sl = slice(blk * LANES, (blk + 1) * LANES)
        lf = _log_sigmoid(f_ref[sl, :] + b_ref[...])
        cs = jnp.dot(tri, lf, preferred_element_type=F32, precision=lax.Precision.HIGHEST)
        cum = cs + carry
        carry = cum[LANES - 1:LANES, :]
        f2 = cum * LOG2E
        hi = f2.astype(BF16).astype(F32)
        mid = (f2 - hi).astype(BF16).astype(F32)
        lo = (f2 - hi - mid).astype(BF16).astype(F32)
        packed = jnp.where(c < nh, hi,
                           jnp.where(c < 2 * nh, pltpu.roll(mid, nh, axis=1),
                                     jnp.where(c < 3 * nh, pltpu.roll(lo, 2 * nh, axis=1), 0.0)))
        col_ref[sl, :] = packed.astype(BF16)
        row_ref[0, :, sl] = jnp.concatenate([hi.T[0:nh, :], mid.T[0:nh, :], lo.T[0:nh, :],
                                             jnp.zeros((nh, LANES), F32)], axis=0)
    carry_ref[...] = jnp.broadcast_to(carry, carry_ref.shape)


def _fcum(f_logit, b_f):
    rows = ROW_TILE
    nblk = SEQ // rows
    return pl.pallas_call(
        functools.partial(_fcum_kernel, rows=rows),
        grid=(BATCH, nblk),
        in_specs=[pl.BlockSpec((rows, LANES), lambda b, j: (b * nblk + j, 0)),
                  pl.BlockSpec((1, LANES), lambda b, j: (0, 0))],
        out_specs=[pl.BlockSpec((rows, LANES), lambda b, j: (b * nblk + j, 0)),
                   pl.BlockSpec((1, 4 * FOX_HEADS, rows), lambda b, j: (b, 0, j))],
        out_shape=[jax.ShapeDtypeStruct((TOKENS, LANES), BF16),
                   jax.ShapeDtypeStruct((BATCH, 4 * FOX_HEADS, SEQ), F32)],
        scratch_shapes=[pltpu.VMEM((HALO, LANES), F32)],
        compiler_params=_params(("arbitrary", "arbitrary")),
        name="fox_forget_cumsum",
    )(f_logit, b_f)


def _fox_kernel(q_ref, k_ref, v_ref, fpc_ref, fpr_ref, o_ref, kaug_ref, vt_ref, qaug_ref,
                s0_ref, s1_ref, bm0_ref, bm1_ref, m_ref, acc_ref, *, t):
    hp = pl.program_id(1)
    qi = pl.program_id(2)
    d = FOX_HEAD_DIM
    nh = FOX_HEADS
    prep_rows = 1024

    @pl.when(qi == 0)
    def _():
        r = lax.broadcasted_iota(jnp.int32, (LANES, LANES), 0)
        c = lax.broadcasted_iota(jnp.int32, (LANES, LANES), 1)
        lane = lax.broadcasted_iota(jnp.int32, (1, LANES), 1)
        ones_lanes = jnp.where(lane < 3, 1.0, 0.0)
        sub = lax.broadcasted_iota(jnp.int32, (LANES, prep_rows), 0)
        for e in (0, 1):
            h = 2 * hp + e
            pick = jnp.where((c >= 3) & (c < 6) & (r == (c - 3) * nh + h), -1.0, 0.0).astype(BF16)
            for blk in range(SEQ // prep_rows):
                rows = slice(blk * prep_rows, (blk + 1) * prep_rows)
                aug = jnp.dot(fpc_ref[rows, :], pick, preferred_element_type=F32) + ones_lanes
                kaug_ref[e, rows, :] = aug.astype(BF16)
                v_t = v_ref[rows, :].astype(F32).T
                own = (sub < d) if e == 0 else (sub >= d)
                vt_ref[e, :, rows] = jnp.where(own, v_t, 1.0).astype(BF16)

    q_t = q_ref[...].astype(F32).T
    sub_q = lax.broadcasted_iota(jnp.int32, (LANES, t), 0)
    sub8 = lax.broadcasted_iota(jnp.int32, (HALO, t), 0)
    q_aug = []
    for e in (0, 1):
        h = 2 * hp + e
        own = (sub_q < d) if e == 0 else (sub_q >= d)
        f_hi = fpr_ref[0, pl.ds(h, 1), :]
        f_mid = fpr_ref[0, pl.ds(nh + h, 1), :]
        f_lo = fpr_ref[0, pl.ds(2 * nh + h, 1), :]
        top = jnp.where(sub8 == 0, f_hi,
                        jnp.where(sub8 == 1, f_mid,
                                  jnp.where(sub8 == 2, f_lo, jnp.where(sub8 < 6, 1.0, 0.0))))
        q_aug.append(jnp.concatenate(
            [jnp.where(own, q_t, 0.0), top, jnp.zeros((LANES - HALO, t), F32)], axis=0).astype(BF16))

    for e in (0, 1):
        qaug_ref[e] = q_aug[e]
        m_ref[e] = jnp.full((1, t), NEG, F32)
        acc_ref[e] = jnp.zeros((LANES, t), F32)

    key_idx = lax.broadcasted_iota(jnp.int32, (t, t), 0)
    qry_idx = lax.broadcasted_iota(jnp.int32, (t, t), 1)
    causal = key_idx <= qry_idx

    def score_stage(j, s_ref, bm_ref):
        start = pl.multiple_of(j * t, t)
        k2 = k_ref[pl.ds(start, t), :]
        for e in (0, 1):
            k_aug = jnp.concatenate([k2, kaug_ref[e, pl.ds(start, t), :]], axis=1)
            s_t = jnp.dot(k_aug, qaug_ref[e], preferred_element_type=F32)
            s_ref[e] = s_t
            bm_ref[e] = jnp.max(s_t, axis=0, keepdims=True)

    def softmax_stage(j, s_ref, bm_ref, masked):
        start = pl.multiple_of(j * t, t)
        for e in (0, 1):
            s_t = s_ref[e]
            if masked:
                s_t = jnp.where(causal, s_t, NEG)
                bm = jnp.max(s_t, axis=0, keepdims=True)
            else:
                bm = bm_ref[e]
            m_prev = m_ref[e]
            m_new = jnp.maximum(m_prev, bm)
            m_ref[e] = m_new
            p_t = jnp.exp2(s_t - m_new).astype(BF16)
            pv = jnp.dot(vt_ref[e, :, pl.ds(start, t)], p_t, preferred_element_type=F32)
            acc_ref[e] = jnp.exp2(m_prev - m_new) * acc_ref[e] + pv

    score_stage(qi, s1_ref, bm1_ref)
    score_stage(0, s0_ref, bm0_ref)
    softmax_stage(qi, s1_ref, bm1_ref, True)

    def two_blocks(i, _):
        j = 2 * i
        score_stage(j + 1, s1_ref, bm1_ref)
        softmax_stage(j, s0_ref, bm0_ref, False)
        score_stage(j + 2, s0_ref, bm0_ref)
        softmax_stage(j + 1, s1_ref, bm1_ref, False)
        return 0

    lax.fori_loop(0, lax.shift_right_logical(qi, 1), two_blocks, 0)

    @pl.when(lax.bitwise_and(qi, 1) == 1)
    def _():
        softmax_stage(qi - 1, s0_ref, bm0_ref, False)

    a0 = acc_ref[0]
    a1 = acc_ref[1]
    o_t = jnp.concatenate([a0[0:d, :] / a0[d:d + 1, :], a1[d:, :] / a1[0:1, :]], axis=0)
    o_ref[...] = o_t.T.astype(o_ref.dtype)


def _fox_attention(qkv, fp_col, fp_row):
    t = ATTN_TILE
    nq = SEQ // t
    pairs = FOX_HEADS // 2
    return pl.pallas_call(
        functools.partial(_fox_kernel, t=t),
        grid=(BATCH, pairs, nq),
        in_specs=[pl.BlockSpec((t, LANES), lambda b, h, i: (b * nq + i, h)),
                  pl.BlockSpec((SEQ, LANES), lambda b, h, i: (b, pairs + h)),
                  pl.BlockSpec((SEQ, LANES), lambda b, h, i: (b, 2 * pairs + h)),
                  pl.BlockSpec((SEQ, LANES), lambda b, h, i: (b, 0)),
                  pl.BlockSpec((1, 4 * FOX_HEADS, t), lambda b, h, i: (b, 0, i))],
        out_specs=pl.BlockSpec((t, LANES), lambda b, h, i: (b * nq + i, h)),
        out_shape=jax.ShapeDtypeStruct((TOKENS, FOX_DIM), BF16),
        scratch_shapes=[pltpu.VMEM((2, SEQ, LANES), BF16),
                        pltpu.VMEM((2, LANES, SEQ), BF16),
                        pltpu.VMEM((2, 2 * LANES, t), BF16),
                        pltpu.VMEM((2, t, t), F32),
                        pltpu.VMEM((2, t, t), F32),
                        pltpu.VMEM((2, 1, t), F32),
                        pltpu.VMEM((2, 1, t), F32),
                        pltpu.VMEM((2, 1, t), F32),
                        pltpu.VMEM((2, LANES, t), F32)],
        compiler_params=_params(("arbitrary", "arbitrary", "arbitrary")),
        name="fox_attention",
    )(qkv, qkv, qkv, fp_col, fp_row)


def _even_mix(first_tile, gb_ref, gc_ref, hh_ref, yb_ref, wc_ref, wa_ref, wb_ref, chalo_ref):
    @pl.when(first_tile)
    def _():
        chalo_ref[...] = jnp.zeros_like(chalo_ref)

    u = gc_ref[...].astype(F32) * hh_ref[...].astype(F32)
    prev = chalo_ref[...]
    chalo_ref[...] = u[u.shape[0] - HALO:, :]
    ya = (gb_ref[...].astype(F32) * _causal_taps_rolled(u, prev, wc_ref[...])).astype(BF16)
    mix = jnp.dot(ya, wa_ref[...], preferred_element_type=F32)
    return mix + jnp.dot(yb_ref[...], wb_ref[...], preferred_element_type=F32)


def _odd_mix(first_tile, u_ref, w_ref):
    return jnp.dot(u_ref[...], w_ref[...], preferred_element_type=F32)


def _tail_kernel(*refs, mix_fn, n_mix, tm, tiles_per_seq):
    mix_refs = refs[:n_mix]
    (h_ref, p_ref, g1_ref, b1_ref, wup_ref, cw_ref, cb_ref, wdn_ref, g_ref, b_ref, wproj_ref, wgate_ref,
     bgate_ref, o_ref, halo_ref, act_ref, acc_ref) = refs[n_mix:n_mix + 17]
    mix_scratch = refs[n_mix + 17:]
    i = pl.program_id(0)
    first_tile = lax.rem(i, tiles_per_seq) == 0

    @pl.when(first_tile)
    def _():
        halo_ref[...] = jnp.zeros_like(halo_ref)

    mix = mix_fn(first_tile, *mix_refs, *mix_scratch)
    x = _layer_norm(ALPHA * h_ref[...] + mix, g1_ref[...], b1_ref[...])
    xb = x.astype(BF16)
    tf = FFN_CHUNK
    n_chunks = D_FF // tf

    def up(c):
        return [jnp.dot(xb, wup_ref[:, part * D_FF + c * tf:part * D_FF + (c + 1) * tf],
                        preferred_element_type=F32) for part in (0, 1)]

    us = up(0)
    group_start = 0
    for c in range(n_chunks):
        us_next = up(c + 1) if c + 1 < n_chunks else None
        branches = []
        for part in (0, 1):
            c0 = part * D_FF + c * tf
            u = us[part]
            prev = halo_ref[:, c0:c0 + tf]
            halo_ref[:, c0:c0 + tf] = u[tm - HALO:, :]
            branches.append(_causal_taps_rolled(u, prev, cw_ref[:, c0:c0 + tf]) + cb_ref[:, c0:c0 + tf])
        us = us_next
        act_ref[:, c * tf:(c + 1) * tf] = (_silu(branches[0]) * branches[1]).astype(BF16)
        if (c + 1) in FFN_DOWN_GROUP_ENDS:
            k0, k1 = group_start * tf, (c + 1) * tf
            d = jnp.dot(act_ref[:, k0:k1], wdn_ref[k0:k1, :], preferred_element_type=F32)
            if group_start == 0:
                acc_ref[...] = d
            else:
                acc_ref[...] += d
            group_start = c + 1
    h2 = _layer_norm(ALPHA * x + acc_ref[...], g_ref[...], b_ref[...])
    gate_logit = jnp.dot(h2.astype(BF16), wgate_ref[...], preferred_element_type=F32) + bgate_ref[...]
    emb = jnp.dot(p_ref[...].astype(BF16), wproj_ref[...], preferred_element_type=F32)
    o_ref[...] = h2 + _sigmoid(gate_logit) * emb


def _layer_tail(name, layer, mix_fn, mix_tiles, mix_weights, mix_scratch, h, p_all, stacked):
    tm = FFN_ROW_TILE
    tiles_per_layer = TOKENS // tm
    tile_specs = [pl.BlockSpec((tm, width), functools.partial(lambda i, col: (i, col), col=col))
                  for _, width, col in mix_tiles]
    mix_arrays = [a for a, _, _ in mix_tiles] + list(mix_weights)
    rest = [h, p_all, *stacked]
    return pl.pallas_call(
        functools.partial(_tail_kernel, mix_fn=mix_fn, n_mix=len(mix_arrays), tm=tm,
                          tiles_per_seq=SEQ // tm),
        grid=(TOKENS // tm,),
        in_specs=(tile_specs + [_resident(w.shape) for w in mix_weights]
                  + [pl.BlockSpec((tm, D_MODEL), lambda i: (i, 0)),
                     pl.BlockSpec((tm, PLE_DIM), lambda i: (layer * tiles_per_layer + i, 0))]
                  + [_resident_layer(a.shape, layer) for a in stacked]),
        out_specs=pl.BlockSpec((tm, D_MODEL), lambda i: (i, 0)),
        out_shape=jax.ShapeDtypeStruct((TOKENS, D_MODEL), F32),
        scratch_shapes=[pltpu.VMEM((HALO, 2 * D_FF), F32),
                        pltpu.VMEM((tm, D_FF), BF16),
                        pltpu.VMEM((tm, D_MODEL), F32)] + list(mix_scratch),
        compiler_params=_params(("arbitrary",)),
        name=name,
    )(*mix_arrays, *rest)


def _shift_select(q, taps):
    r = jnp.arange((taps - 1) * q)[:, None]
    c = jnp.arange(2 * q)[None, :]
    return (c == q + r % q - (r // q + 1)).astype(BF16)


def _ssd_kernel(z_ref, xbc_ref, dt_ref, sel_ref, cw_ref, cb_ref, dtb_ref, alog_ref, dsk_ref, ng_ref,
                o_ref, xprev_ref, xc_ref, s_ref, u_ref):
    q = SSM_CHUNK
    n = SSM_STATE

    @pl.when(pl.program_id(1) == 0)
    def _():
        xprev_ref[...] = jnp.zeros_like(xprev_ref)
        s_ref[...] = jnp.zeros_like(s_ref)

    cw = cw_ref[...]
    taps = SSM_CONV_WIDTH
    sel = sel_ref[...]
    row = lax.broadcasted_iota(jnp.int32, (q, q), 0)
    col = lax.broadcasted_iota(jnp.int32, (q, q), 1)
    causal = row >= col
    lo = lax.broadcasted_iota(jnp.int32, (q, LANES), 1) < SSM_HEAD_DIM
    heads_per_group = SSM_HEADS // SSM_GROUPS
    group_w = heads_per_group * SSM_HEAD_DIM

    for sub in range(SSD_CHUNKS_PER_STEP):
        rows = slice(sub * q, (sub + 1) * q)
        prev_rows = slice((sub - 1) * q, sub * q)

        for c0 in range(0, SSM_CONV_CH, SSD_CONV_COLS):
            cols = slice(c0, c0 + SSD_CONV_COLS)
            x_cur = xbc_ref[rows, cols]
            x_prev = xprev_ref[:, cols] if sub == 0 else xbc_ref[prev_rows, cols]
            shifted = jnp.dot(sel, jnp.concatenate([x_prev, x_cur], axis=0),
                              preferred_element_type=F32)
            conv = cw[taps - 1:taps, cols] * x_cur.astype(F32) + cb_ref[:, cols]
            for k in range(1, taps):
                conv = conv + cw[taps - 1 - k:taps - k, cols] * shifted[(k - 1) * q:k * q, :]
            xc_ref[rows, cols] = _silu(conv)

        dt = _softplus(dt_ref[rows, :] + dtb_ref[...])
        a = dt * (-LOG2E * jnp.exp(alog_ref[...]))
        acs = jnp.dot(causal.astype(F32), a, preferred_element_type=F32,
                      precision=lax.Precision.HIGHEST)
        dt_t = dt.T
        acs_t = acs.T
        tot = acs_t[:, q - 1:q]
        w_t = dt_t * jnp.exp2(tot - acs_t)
        src_t = acs_t - jnp.log2(dt_t)
        eacs = jnp.exp2(acs)
        dec = jnp.exp2(tot)

        for g in range(SSM_GROUPS):
            b0 = SSM_INNER + g * n
            c0 = SSM_INNER + SSM_GROUPS * n + g * n
            bg = xc_ref[rows, b0:b0 + n]
            cg = xc_ref[rows, c0:c0 + n]
            cb = lax.dot_general(cg.astype(BF16), bg.astype(BF16), (((1,), (1,)), ((), ())),
                                 preferred_element_type=F32)
            bg_t = bg.T
            for pr in range(heads_per_group // 2):
                j = g * (heads_per_group // 2) + pr
                sl = slice(j * LANES, (j + 1) * LANES)
                x = xc_ref[rows, sl]
                xb = x.astype(BF16)
                s_prev = s_ref[:, sl]
                rhs = jnp.concatenate([xb, s_prev.astype(BF16)], axis=0)
                ys, news, decs = [], [], []
                for e in (0, 1):
                    h = 2 * j + e
                    seg = acs[:, h:h + 1] - src_t[h:h + 1, :]
                    m_h = (cb * jnp.exp2(jnp.where(causal, seg, NEG))).astype(BF16)
                    w2 = (cg * eacs[:, h:h + 1]).astype(BF16)
                    lhs = jnp.concatenate([m_h, w2], axis=1)
                    ys.append(jnp.dot(lhs, rhs, preferred_element_type=F32))
                    bw_t = (bg_t * w_t[h:h + 1, :]).astype(BF16)
                    news.append(jnp.dot(bw_t, xb, preferred_element_type=F32))
                    decs.append(jnp.broadcast_to(dec[h:h + 1, :], (n, LANES)))
                y = jnp.where(lo, ys[0], ys[1])
                s_ref[:, sl] = (s_prev * jnp.where(lo, decs[0], decs[1])
                                + jnp.where(lo, news[0], news[1]))
                y = y + dsk_ref[:, sl] * x
                u_ref[rows, sl] = y * _silu(z_ref[rows, sl].astype(F32))
            gs = slice(g * group_w, (g + 1) * group_w)
            ug = u_ref[rows, gs]
            ms = jnp.mean(ug * ug, axis=-1, keepdims=True)
            o_ref[rows, gs] = (ug * lax.rsqrt(ms + RMS_EPS) * ng_ref[:, gs]).astype(o_ref.dtype)

    xprev_ref[...] = xbc_ref[(SSD_CHUNKS_PER_STEP - 1) * q:, :]


def _ssd(z, xbc, dt_raw, conv_w, conv_b, dt_bias, a_log, d_skip, norm_g):
    q = SSM_CHUNK
    rows = SSD_CHUNKS_PER_STEP * q
    steps = SEQ // rows
    row = lambda b, c: (b * steps + c, 0)
    sel = _shift_select(q, SSM_CONV_WIDTH)
    return pl.pallas_call(
        _ssd_kernel,
        grid=(BATCH, steps),
        in_specs=[pl.BlockSpec((rows, SSM_INNER), row),
                  pl.BlockSpec((rows, SSM_CONV_CH), row),
                  pl.BlockSpec((rows, LANES), row),
                  _resident(sel.shape),
                  _resident(conv_w.shape), _resident(conv_b.shape), _resident(dt_bias.shape),
                  _resident(a_log.shape), _resident(d_skip.shape), _resident(norm_g.shape)],
        out_specs=pl.BlockSpec((rows, SSM_INNER), row),
        out_shape=jax.ShapeDtypeStruct((TOKENS, SSM_INNER), BF16),
        scratch_shapes=[pltpu.VMEM((q, SSM_CONV_CH), BF16),
                        pltpu.VMEM((rows, SSM_CONV_CH), F32),
                        pltpu.VMEM((SSM_STATE, SSM_INNER), F32),
                        pltpu.VMEM((rows, SSM_INNER), F32)],
        compiler_params=_params(("arbitrary", "arbitrary")),
        name="mamba2_ssd",
    )(z, xbc, dt_raw, sel, conv_w, conv_b, dt_bias, a_log, d_skip, norm_g)


def _pad_cols(w, n):
    return jnp.pad(w, ((0, 0), (0, n - w.shape[1])))


def _pad_rows(w, n):
    return jnp.pad(w, ((0, n - w.shape[0]), (0, 0)))


def _row(v):
    return v.reshape(1, -1)


def _even_mixer(h, w_in, b_f, w_conv, w_out):
    c_end = 3 * CONV_DIM
    a_end = c_end + 3 * FOX_DIM
    w_t = w_in.T
    w_c = w_t[:c_end].astype(BF16)
    w_qkv = w_t[c_end:a_end].astype(BF16)
    w_f = _pad_rows(w_t[a_end:], LANES)
    q_scale = (FOX_HEAD_DIM ** -0.5) * LOG2E
    conv_in, qkv, f_logit = _proj(h, [w_c, w_qkv, w_f], [BF16, BF16, F32], "even_in_proj",
                                  lead_scales=[(0, 1.0), (FOX_DIM, q_scale), (0, 1.0)])
    fp_col, fp_row = _fcum(f_logit, _pad_cols(_row(b_f), LANES))
    yb = _fox_attention(qkv, fp_col, fp_row)
    tiles = [(conv_in, CONV_DIM, 0), (conv_in, CONV_DIM, 1), (conv_in, CONV_DIM, 2), (yb, FOX_DIM, 0)]
    weights = [w_conv, w_out[:CONV_DIM].astype(BF16), w_out[CONV_DIM:].astype(BF16)]
    return _even_mix, tiles, weights, [pltpu.VMEM((HALO, CONV_DIM), F32)]


def _odd_mixer(h, w_in, conv_w, conv_b, dt_bias, a_log, d_skip, norm_g, w_out):
    x_end = SSM_INNER + SSM_CONV_CH
    w_t = w_in.T
    w_z = w_t[:SSM_INNER].astype(BF16)
    w_x = w_t[SSM_INNER:x_end].astype(BF16)
    w_dt = _pad_rows(w_t[x_end:], LANES)
    z, xbc, dt_raw = _proj(h, [w_z, w_x, w_dt], [BF16, BF16, F32], "odd_in_proj")
    u = _ssd(z, xbc, dt_raw, conv_w, _row(conv_b), _pad_cols(_row(dt_bias), LANES),
             _pad_cols(_row(a_log), LANES), _row(jnp.repeat(d_skip, SSM_HEAD_DIM)), _row(norm_g))
    return _odd_mix, [(u, SSM_INNER, 0)], [w_out.astype(BF16)], []


def kernel(x, p, even_w_in, even_b_f, even_conv_w, even_w_out, odd_w_in, odd_conv_w, odd_conv_b,
           odd_dt_bias, odd_a_log, odd_d_skip, odd_norm_g, odd_w_out, ln_mix_g, ln_mix_b, ffn_w_up,
           ffn_conv_w, ffn_conv_b, ffn_w_down, ln_ffn_g, ln_ffn_b, ple_w_proj, ple_w_gate,
           ple_b_gate):
    h = x.reshape(TOKENS, D_MODEL)
    rows = lambda v: v.reshape(DEPTH, 1, -1)
    stacked = (rows(ln_mix_g), rows(ln_mix_b), ffn_w_up.astype(BF16), ffn_conv_w, rows(ffn_conv_b),
               ffn_w_down.astype(BF16), rows(ln_ffn_g), rows(ln_ffn_b), ple_w_proj.astype(BF16),
               ple_w_gate.astype(BF16), rows(ple_b_gate))
    p_all = p.reshape(DEPTH * TOKENS, PLE_DIM)
    for i in range(DEPTH):
        j = i // 2
        if i % 2 == 0:
            name = "even_layer_tail"
            mix = _even_mixer(h, even_w_in[j], even_b_f[j], even_conv_w[j], even_w_out[j])
        else:
            name = "odd_layer_tail"
            mix = _odd_mixer(h, odd_w_in[j], odd_conv_w[j], odd_conv_b[j], odd_dt_bias[j],
                             odd_a_log[j], odd_d_skip[j], odd_norm_g[j], odd_w_out[j])
        h = _layer_tail(name, i, *mix, h, p_all, stacked)
    return h.reshape(BATCH, SEQ, D_MODEL)
```

```python
import functools

import jax
import jax.numpy as jnp
from jax import lax
from jax.experimental import pallas as pl
from jax.experimental.pallas import tpu as pltpu

F32 = jnp.float32
BF16 = jnp.bfloat16

D_MODEL = 1024
BATCH = 2
SEQ = 8192
DEPTH = 2
TOKENS = BATCH * SEQ

CONV_DIM = 512
CONV_WIDTH = 3
FOX_HEADS = 8
FOX_HEAD_DIM = 64
FOX_DIM = FOX_HEADS * FOX_HEAD_DIM
SSM_INNER = 2 * D_MODEL
SSM_HEAD_DIM = 64
SSM_HEADS = SSM_INNER // SSM_HEAD_DIM
SSM_GROUPS = 4
SSM_STATE = 128
SSM_CONV_WIDTH = 4
SSM_CHUNK = 128
SSM_CONV_CH = SSM_INNER + 2 * SSM_GROUPS * SSM_STATE
D_FF = 2816
FFN_CONV_WIDTH = 3
PLE_DIM = 256
LN_EPS = 1e-5
RMS_EPS = 1e-5
ALPHA = (2.0 * DEPTH) ** 0.25

LANES = 128
HALO = 8
NEG = -1e30
VMEM_LIMIT = 56 * 1024 * 1024

ROW_TILE = 512
SSD_CONV_COLS = 512
SSD_CHUNKS_PER_STEP = 4
FFN_ROW_TILE = 512
FFN_CHUNK = 256
FFN_DOWN_GROUP_ENDS = (6, 11)
ATTN_TILE = 512
LOG2E = 1.4426950408889634


def _resident(shape):
    nd = len(shape)
    return pl.BlockSpec(shape, lambda *_: (0,) * nd, pipeline_mode=pl.Buffered(1))


def _resident_layer(shape, layer):
    nd = len(shape)
    return pl.BlockSpec((None,) + tuple(shape[1:]), lambda *_: (layer,) + (0,) * (nd - 1),
                        pipeline_mode=pl.Buffered(1))


def _params(sem, flags=None):
    return pltpu.CompilerParams(dimension_semantics=sem, vmem_limit_bytes=VMEM_LIMIT, flags=flags)


def _sigmoid(x):
    return 0.5 + 0.5 * jnp.tanh(0.5 * x)


def _silu(x):
    h = 0.5 * x
    return h + h * jnp.tanh(h)


def _softplus(x):
    return jnp.maximum(x, 0.0) + jnp.log1p(jnp.exp(-jnp.abs(x)))


def _log_sigmoid(x):
    return jnp.minimum(x, 0.0) - jnp.log1p(jnp.exp(-jnp.abs(x)))


def _layer_norm(r, g, b):
    mu = jnp.mean(r, axis=-1, keepdims=True)
    d = r - mu
    var = jnp.mean(d * d, axis=-1, keepdims=True)
    return d * lax.rsqrt(var + LN_EPS) * g + b


def _causal_taps_rolled(cur, prev, w):
    k_taps = w.shape[0]
    sub = lax.broadcasted_iota(jnp.int32, prev.shape, 0)
    out = w[k_taps - 1:k_taps, :] * cur
    for k in range(k_taps - 1):
        shift = k_taps - 1 - k
        rolled = pltpu.roll(cur, shift, axis=0)
        head = jnp.where(sub < shift, pltpu.roll(prev, shift, axis=0), rolled[0:HALO, :])
        out = out + w[k:k + 1, :] * jnp.concatenate([head, rolled[HALO:, :]], axis=0)
    return out


def _dot_nt(a, b_t):
    return lax.dot_general(a, b_t, (((1,), (1,)), ((), ())), preferred_element_type=F32)


def _proj_kernel(x_ref, *refs, n_out, chunk, lead_scales, precise):
    w_refs, o_refs = refs[:n_out], refs[n_out:]
    x = x_ref[...]
    xb = x.astype(BF16)
    for w_ref, o_ref, (lead_cols, lead_scale), hi_lo in zip(w_refs, o_refs, lead_scales, precise):
        n = o_ref.shape[1]
        if hi_lo:
            x_lo = (x - xb.astype(F32)).astype(BF16)
            both = _dot_nt(xb, w_ref[...])
            cross = _dot_nt(x_lo, w_ref[0:n, :])
            o_ref[...] = (both[:, 0:n] + both[:, n:] + cross).astype(o_ref.dtype)
            continue
        for c0 in range(0, n, chunk):
            c1 = min(c0 + chunk, n)
            acc = _dot_nt(xb, w_ref[c0:c1, :])
            if c1 <= lead_cols:
                acc = acc * lead_scale
            o_ref[:, c0:c1] = acc.astype(o_ref.dtype)


def _split_bf16(w_t):
    w_hi = w_t.astype(BF16)
    w_lo = (w_t - w_hi.astype(F32)).astype(BF16)
    return jnp.concatenate([w_hi, w_lo], axis=0)


def _proj(x, ws_t, out_dtypes, name, lead_scales=None):
    m, k = x.shape
    tm = ROW_TILE
    chunk = 512
    if lead_scales is None:
        lead_scales = [(0, 1.0)] * len(ws_t)
    assert all(cols % chunk == 0 for cols, _ in lead_scales)
    precise = tuple(w.dtype == F32 for w in ws_t)
    widths = [w.shape[0] for w in ws_t]
    ws = [_split_bf16(w) if p else w for w, p in zip(ws_t, precise)]
    return pl.pallas_call(
        functools.partial(_proj_kernel, n_out=len(ws), chunk=chunk, lead_scales=tuple(lead_scales),
                          precise=precise),
        grid=(m // tm,),
        in_specs=[pl.BlockSpec((tm, k), lambda i: (i, 0))] + [_resident(w.shape) for w in ws],
        out_specs=[pl.BlockSpec((tm, n), lambda i: (i, 0)) for n in widths],
        out_shape=[jax.ShapeDtypeStruct((m, n), dt) for n, dt in zip(widths, out_dtypes)],
        compiler_params=_params(("arbitrary",)),
        name=name,
    )(x, *ws)


def _fcum_kernel(f_ref, b_ref, col_ref, row_ref, carry_ref, *, rows):
    j = pl.program_id(1)

    @pl.when(j == 0)
    def _():
        carry_ref[...] = jnp.zeros_like(carry_ref)

    r = lax.broadcasted_iota(jnp.int32, (LANES, LANES), 0)
    c = lax.broadcasted_iota(jnp.int32, (LANES, LANES), 1)
    tri = (r >= c).astype(F32)
    nh = FOX_HEADS
    carry = carry_ref[0:1, :]
    for blk in range(rows // LANES):
        sl = slice(blk * LANES, (blk + 1) * LANES)
        lf = _log_sigmoid(f_ref[sl, :] + b_ref[...])
        cs = jnp.dot(tri, lf, preferred_element_type=F32, precision=lax.Precision.HIGHEST)
        cum = cs + carry
        carry = cum[LANES - 1:LANES, :]
        f2 = cum * LOG2E
        hi = f2.astype(BF16).astype(F32)
        mid = (f2 - hi).astype(BF16).astype(F32)
        lo = (f2 - hi - mid).astype(BF16).astype(F32)
        packed = jnp.where(c < nh, hi,
                           jnp.where(c < 2 * nh, pltpu.roll(mid, nh, axis=1),
                                     jnp.where(c < 3 * nh, pltpu.roll(lo, 2 * nh, axis=1), 0.0)))
        col_ref[sl, :] = packed.astype(BF16)
        row_ref[0, :, sl] = jnp.concatenate([hi.T[0:nh, :], mid.T[0:nh, :], lo.T[0:nh, :],
                                             jnp.zeros((nh, LANES), F32)], axis=0)
    carry_ref[...] = jnp.broadcast_to(carry, carry_ref.shape)


def _fcum(f_logit, b_f):
    rows = ROW_TILE
    nblk = SEQ // rows
    return pl.pallas_call(
        functools.partial(_fcum_kernel, rows=rows),
        grid=(BATCH, nblk),
        in_specs=[pl.BlockSpec((rows, LANES), lambda b, j: (b * nblk + j, 0)),
                  pl.BlockSpec((1, LANES), lambda b, j: (0, 0))],
        out_specs=[pl.BlockSpec((rows, LANES), lambda b, j: (b * nblk + j, 0)),
                   pl.BlockSpec((1, 4 * FOX_HEADS, rows), lambda b, j: (b, 0, j))],
        out_shape=[jax.ShapeDtypeStruct((TOKENS, LANES), BF16),
                   jax.ShapeDtypeStruct((BATCH, 4 * FOX_HEADS, SEQ), F32)],
        scratch_shapes=[pltpu.VMEM((HALO, LANES), F32)],
        compiler_params=_params(("arbitrary", "arbitrary")),
        name="fox_forget_cumsum",
    )(f_logit, b_f)


def _fox_kernel(q_ref, k_ref, v_ref, fpc_ref, fpr_ref, o_ref, kaug_ref, vt_ref, qaug_ref,
                s0_ref, s1_ref, bm0_ref, bm1_ref, m_ref, acc_ref, *, t):
    hp = pl.program_id(1)
    qi = pl.program_id(2)
    d = FOX_HEAD_DIM
    nh = FOX_HEADS
    prep_rows = 1024

    @pl.when(qi == 0)
    def _():
        r = lax.broadcasted_iota(jnp.int32, (LANES, LANES), 0)
        c = lax.broadcasted_iota(jnp.int32, (LANES, LANES), 1)
        lane = lax.broadcasted_iota(jnp.int32, (1, LANES), 1)
        ones_lanes = jnp.where(lane < 3, 1.0, 0.0)
        sub = lax.broadcasted_iota(jnp.int32, (LANES, prep_rows), 0)
        for e in (0, 1):
            h = 2 * hp + e
            pick = jnp.where((c >= 3) & (c < 6) & (r == (c - 3) * nh + h), -1.0, 0.0).astype(BF16)
            for blk in range(SEQ // prep_rows):
                rows = slice(blk * prep_rows, (blk + 1) * prep_rows)
                aug = jnp.dot(fpc_ref[rows, :], pick, preferred_element_type=F32) + ones_lanes
                kaug_ref[e, rows, :] = aug.astype(BF16)
                v_t = v_ref[rows, :].astype(F32).T
                own = (sub < d) if e == 0 else (sub >= d)
                vt_ref[e, :, rows] = jnp.where(own, v_t, 1.0).astype(BF16)

    q_t = q_ref[...].astype(F32).T
    sub_q = lax.broadcasted_iota(jnp.int32, (LANES, t), 0)
    sub8 = lax.broadcasted_iota(jnp.int32, (HALO, t), 0)
    q_aug = []
    for e in (0, 1):
        h = 2 * hp + e
        own = (sub_q < d) if e == 0 else (sub_q >= d)
        f_hi = fpr_ref[0, pl.ds(h, 1), :]
        f_mid = fpr_ref[0, pl.ds(nh + h, 1), :]
        f_lo = fpr_ref[0, pl.ds(2 * nh + h, 1), :]
        top = jnp.where(sub8 == 0, f_hi,
                        jnp.where(sub8 == 1, f_mid,
                                  jnp.where(sub8 == 2, f_lo, jnp.where(sub8 < 6, 1.0, 0.0))))
        q_aug.append(jnp.concatenate(
            [jnp.where(own, q_t, 0.0), top, jnp.zeros((LANES - HALO, t), F32)], axis=0).astype(BF16))

    for e in (0, 1):
        qaug_ref[e] = q_aug[e]
        m_ref[e] = jnp.full((1, t), NEG, F32)
        acc_ref[e] = jnp.zeros((LANES, t), F32)

    key_idx = lax.broadcasted_iota(jnp.int32, (t, t), 0)
    qry_idx = lax.broadcasted_iota(jnp.int32, (t, t), 1)
    causal = key_idx <= qry_idx

    def score_stage(j, s_ref, bm_ref):
        start = pl.multiple_of(j * t, t)
        k2 = k_ref[pl.ds(start, t), :]
        for e in (0, 1):
            k_aug = jnp.concatenate([k2, kaug_ref[e, pl.ds(start, t), :]], axis=1)
            s_t = jnp.dot(k_aug, qaug_ref[e], preferred_element_type=F32)
            s_ref[e] = s_t
            bm_ref[e] = jnp.max(s_t, axis=0, keepdims=True)

    def softmax_stage(j, s_ref, bm_ref, masked):
        start = pl.multiple_of(j * t, t)
        for e in (0, 1):
            s_t = s_ref[e]
            if masked:
                s_t = jnp.where(causal, s_t, NEG)
                bm = jnp.max(s_t, axis=0, keepdims=True)
            else:
                bm = bm_ref[e]
            m_prev = m_ref[e]
            m_new = jnp.maximum(m_prev, bm)
            m_ref[e] = m_new
            p_t = jnp.exp2(s_t - m_new).astype(BF16)
            pv = jnp.dot(vt_ref[e, :, pl.ds(start, t)], p_t, preferred_element_type=F32)
            acc_ref[e] = jnp.exp2(m_prev - m_new) * acc_ref[e] + pv

    score_stage(qi, s1_ref, bm1_ref)
    score_stage(0, s0_ref, bm0_ref)
    softmax_stage(qi, s1_ref, bm1_ref, True)

    def two_blocks(i, _):
        j = 2 * i
        score_stage(j + 1, s1_ref, bm1_ref)
        softmax_stage(j, s0_ref, bm0_ref, False)
        score_stage(j + 2, s0_ref, bm0_ref)
        softmax_stage(j + 1, s1_ref, bm1_ref, False)
        return 0

    def four_blocks(i, _):
        two_blocks(2 * i, 0)
        two_blocks(2 * i + 1, 0)
        return 0

    quads = lax.shift_right_logical(qi, 2)
    lax.fori_loop(0, quads, four_blocks, 0)
    lax.fori_loop(2 * quads, lax.shift_right_logical(qi, 1), two_blocks, 0)

    @pl.when(lax.bitwise_and(qi, 1) == 1)
    def _():
        softmax_stage(qi - 1, s0_ref, bm0_ref, False)

    a0 = acc_ref[0]
    a1 = acc_ref[1]
    o_t = jnp.concatenate([a0[0:d, :] / a0[d:d + 1, :], a1[d:, :] / a1[0:1, :]], axis=0)
    o_ref[...] = o_t.T.astype(o_ref.dtype)


def _fox_attention(qkv, fp_col, fp_row):
    t = ATTN_TILE
    nq = SEQ // t
    pairs = FOX_HEADS // 2
    return pl.pallas_call(
        functools.partial(_fox_kernel, t=t),
        grid=(BATCH, pairs, nq),
        in_specs=[pl.BlockSpec((t, LANES), lambda b, h, i: (b * nq + i, h)),
                  pl.BlockSpec((SEQ, LANES), lambda b, h, i: (b, pairs + h)),
                  pl.BlockSpec((SEQ, LANES), lambda b, h, i: (b, 2 * pairs + h)),
                  pl.BlockSpec((SEQ, LANES), lambda b, h, i: (b, 0)),
                  pl.BlockSpec((1, 4 * FOX_HEADS, t), lambda b, h, i: (b, 0, i))],
        out_specs=pl.BlockSpec((t, LANES), lambda b, h, i: (b * nq + i, h)),
        out_shape=jax.ShapeDtypeStruct((TOKENS, FOX_DIM), BF16),
        scratch_shapes=[pltpu.VMEM((2, SEQ, LANES), BF16),
                        pltpu.VMEM((2, LANES, SEQ), BF16),
                        pltpu.VMEM((2, 2 * LANES, t), BF16),
                        pltpu.VMEM((2, t, t), F32),
                        pltpu.VMEM((2, t, t), F32),
                        pltpu.VMEM((2, 1, t), F32),
                        pltpu.VMEM((2, 1, t), F32),
                        pltpu.VMEM((2, 1, t), F32),
                        pltpu.VMEM((2, LANES, t), F32)],
        compiler_params=_params(("arbitrary", "arbitrary", "arbitrary")),
        name="fox_attention",
    )(qkv, qkv, qkv, fp_col, fp_row)


def _even_mix(first_tile, gb_ref, gc_ref, hh_ref, yb_ref, wc_ref, wa_ref, wb_ref, chalo_ref):
    @pl.when(first_tile)
    def _():
        chalo_ref[...] = jnp.zeros_like(chalo_ref)

    u = gc_ref[...].astype(F32) * hh_ref[...].astype(F32)
    prev = chalo_ref[...]
    chalo_ref[...] = u[u.shape[0] - HALO:, :]
    ya = (gb_ref[...].astype(F32) * _causal_taps_rolled(u, prev, wc_ref[...])).astype(BF16)
    mix = jnp.dot(ya, wa_ref[...], preferred_element_type=F32)
    return mix + jnp.dot(yb_ref[...], wb_ref[...], preferred_element_type=F32)


def _odd_mix(first_tile, u_ref, w_ref):
    return jnp.dot(u_ref[...], w_ref[...], preferred_element_type=F32)


def _tail_kernel(*refs, mix_fn, n_mix, tm, tiles_per_seq):
    mix_refs = refs[:n_mix]
    (h_ref, p_ref, g1_ref, b1_ref, wup_ref, cw_ref, cb_ref, wdn_ref, g_ref, b_ref, wproj_ref, wgate_ref,
     bgate_ref, o_ref, halo_ref, act_ref, acc_ref) = refs[n_mix:n_mix + 17]
    mix_scratch = refs[n_mix + 17:]
    i = pl.program_id(0)
    first_tile = lax.rem(i, tiles_per_seq) == 0

    @pl.when(first_tile)
    def _():
        halo_ref[...] = jnp.zeros_like(halo_ref)

    mix = mix_fn(first_tile, *mix_refs, *mix_scratch)
    x = _layer_norm(ALPHA * h_ref[...] + mix, g1_ref[...], b1_ref[...])
    xb = x.astype(BF16)
    tf = FFN_CHUNK
    n_chunks = D_FF // tf

    def up(c):
        return [jnp.dot(xb, wup_ref[:, part * D_FF + c * tf:part * D_FF + (c + 1) * tf],
                        preferred_element_type=F32) for part in (0, 1)]

    us = up(0)
    group_start = 0
    for c in range(n_chunks):
        us_next = up(c + 1) if c + 1 < n_chunks else None
        branches = []
        for part in (0, 1):
            c0 = part * D_FF + c * tf
            u = us[part]
            prev = halo_ref[:, c0:c0 + tf]
            halo_ref[:, c0:c0 + tf] = u[tm - HALO:, :]
            branches.append(_causal_taps_rolled(u, prev, cw_ref[:, c0:c0 + tf]) + cb_ref[:, c0:c0 + tf])
        us = us_next
        act_ref[:, c * tf:(c + 1) * tf] = (_silu(branches[0]) * branches[1]).astype(BF16)
        if (c + 1) in FFN_DOWN_GROUP_ENDS:
            k0, k1 = group_start * tf, (c + 1) * tf
            d = jnp.dot(act_ref[:, k0:k1], wdn_ref[k0:k1, :], preferred_element_type=F32)
            if group_start == 0:
                acc_ref[...] = d
            else:
                acc_ref[...] += d
            group_start = c + 1
    h2 = _layer_norm(ALPHA * x + acc_ref[...], g_ref[...], b_ref[...])
    gate_logit = jnp.dot(h2.astype(BF16), wgate_ref[...], preferred_element_type=F32) + bgate_ref[...]
    emb = jnp.dot(p_ref[...].astype(BF16), wproj_ref[...], preferred_element_type=F32)
    o_ref[...] = h2 + _sigmoid(gate_logit) * emb


def _layer_tail(name, layer, mix_fn, mix_tiles, mix_weights, mix_scratch, h, p_all, stacked):
    tm = FFN_ROW_TILE
    tiles_per_layer = TOKENS // tm
    tile_specs = [pl.BlockSpec((tm, width), functools.partial(lambda i, col: (i, col), col=col))
                  for _, width, col in mix_tiles]
    mix_arrays = [a for a, _, _ in mix_tiles] + list(mix_weights)
    rest = [h, p_all, *stacked]
    return pl.pallas_call(
        functools.partial(_tail_kernel, mix_fn=mix_fn, n_mix=len(mix_arrays), tm=tm,
                          tiles_per_seq=SEQ // tm),
        grid=(TOKENS // tm,),
        in_specs=(tile_specs + [_resident(w.shape) for w in mix_weights]
                  + [pl.BlockSpec((tm, D_MODEL), lambda i: (i, 0)),
                     pl.BlockSpec((tm, PLE_DIM), lambda i: (layer * tiles_per_layer + i, 0))]
                  + [_resident_layer(a.shape, layer) for a in stacked]),
        out_specs=pl.BlockSpec((tm, D_MODEL), lambda i: (i, 0)),
        out_shape=jax.ShapeDtypeStruct((TOKENS, D_MODEL), F32),
        scratch_shapes=[pltpu.VMEM((HALO, 2 * D_FF), F32),
                        pltpu.VMEM((tm, D_FF), BF16),
                        pltpu.VMEM((tm, D_MODEL), F32)] + list(mix_scratch),
        compiler_params=_params(("arbitrary",)),
        name=name,
    )(*mix_arrays, *rest)


def _shift_select(q, taps):
    r = jnp.arange((taps - 1) * q)[:, None]
    c = jnp.arange(2 * q)[None, :]
    return (c == q + r % q - (r // q + 1)).astype(BF16)


def _ssd_kernel(z_ref, xbc_ref, dt_ref, sel_ref, cw_ref, cb_ref, dtb_ref, alog_ref, dsk_ref, ng_ref,
                o_ref, xprev_ref, xc_ref, s_ref, u_ref):
    q = SSM_CHUNK
    n = SSM_STATE

    @pl.when(pl.program_id(1) == 0)
    def _():
        xprev_ref[...] = jnp.zeros_like(xprev_ref)
        s_ref[...] = jnp.zeros_like(s_ref)

    cw = cw_ref[...]
    taps = SSM_CONV_WIDTH
    sel = sel_ref[...]
    row = lax.broadcasted_iota(jnp.int32, (q, q), 0)
    col = lax.broadcasted_iota(jnp.int32, (q, q), 1)
    causal = row >= col
    lo = lax.broadcasted_iota(jnp.int32, (q, LANES), 1) < SSM_HEAD_DIM
    heads_per_group = SSM_HEADS // SSM_GROUPS
    group_w = heads_per_group * SSM_HEAD_DIM

    for sub in range(SSD_CHUNKS_PER_STEP):
        rows = slice(sub * q, (sub + 1) * q)
        prev_rows = slice((sub - 1) * q, sub * q)

        for c0 in range(0, SSM_CONV_CH, SSD_CONV_COLS):
            cols = slice(c0, c0 + SSD_CONV_COLS)
            x_cur = xbc_ref[rows, cols]
            x_prev = xprev_ref[:, cols] if sub == 0 else xbc_ref[prev_rows, cols]
            shifted = jnp.dot(sel, jnp.concatenate([x_prev, x_cur], axis=0),
                              preferred_element_type=F32)
            conv = cw[taps - 1:taps, cols] * x_cur.astype(F32) + cb_ref[:, cols]
            for k in range(1, taps):
                conv = conv + cw[taps - 1 - k:taps - k, cols] * shifted[(k - 1) * q:k * q, :]
            xc_ref[rows, cols] = _silu(conv)

        dt = _softplus(dt_ref[rows, :] + dtb_ref[...])
        a = dt * (-LOG2E * jnp.exp(alog_ref[...]))
        acs = jnp.dot(causal.astype(F32), a, preferred_element_type=F32,
                      precision=lax.Precision.HIGHEST)
        dt_t = dt.T
        acs_t = acs.T
        tot = acs_t[:, q - 1:q]
        w_t = dt_t * jnp.exp2(tot - acs_t)
        src_t = acs_t - jnp.log2(dt_t)
        eacs = jnp.exp2(acs)
        dec = jnp.exp2(tot)

        for g in range(SSM_GROUPS):
            b0 = SSM_INNER + g * n
            c0 = SSM_INNER + SSM_GROUPS * n + g * n
            bg = xc_ref[rows, b0:b0 + n]
            cg = xc_ref[rows, c0:c0 + n]
            cb = lax.dot_general(cg.astype(BF16), bg.astype(BF16), (((1,), (1,)), ((), ())),
                                 preferred_element_type=F32)
            bg_t = bg.T
            for pr in range(heads_per_group // 2):
                j = g * (heads_per_group // 2) + pr
                sl = slice(j * LANES, (j + 1) * LANES)
                x = xc_ref[rows, sl]
                xb = x.astype(BF16)
                s_prev = s_ref[:, sl]
                rhs = jnp.concatenate([xb, s_prev.astype(BF16)], axis=0)
                ys, news, decs = [], [], []
                for e in (0, 1):
                    h = 2 * j + e
                    seg = acs[:, h:h + 1] - src_t[h:h + 1, :]
                    m_h = (cb * jnp.exp2(jnp.where(causal, seg, NEG))).astype(BF16)
                    w2 = (cg * eacs[:, h:h + 1]).astype(BF16)
                    lhs = jnp.concatenate([m_h, w2], axis=1)
                    ys.append(jnp.dot(lhs, rhs, preferred_element_type=F32))
                    bw_t = (bg_t * w_t[h:h + 1, :]).astype(BF16)
                    news.append(jnp.dot(bw_t, xb, preferred_element_type=F32))
                    decs.append(jnp.broadcast_to(dec[h:h + 1, :], (n, LANES)))
                y = jnp.where(lo, ys[0], ys[1])
                s_ref[:, sl] = (s_prev * jnp.where(lo, decs[0], decs[1])
                                + jnp.where(lo, news[0], news[1]))
                y = y + dsk_ref[:, sl] * x
                u_ref[rows, sl] = y * _silu(z_ref[rows, sl].astype(F32))
            gs = slice(g * group_w, (g + 1) * group_w)
            ug = u_ref[rows, gs]
            ms = jnp.mean(ug * ug, axis=-1, keepdims=True)
            o_ref[rows, gs] = (ug * lax.rsqrt(ms + RMS_EPS) * ng_ref[:, gs]).astype(o_ref.dtype)

    xprev_ref[...] = xbc_ref[(SSD_CHUNKS_PER_STEP - 1) * q:, :]


def _ssd(z, xbc, dt_raw, conv_w, conv_b, dt_bias, a_log, d_skip, norm_g):
    q = SSM_CHUNK
    rows = SSD_CHUNKS_PER_STEP * q
    steps = SEQ // rows
    row = lambda b, c: (b * steps + c, 0)
    sel = _shift_select(q, SSM_CONV_WIDTH)
    return pl.pallas_call(
        _ssd_kernel,
        grid=(BATCH, steps),
        in_specs=[pl.BlockSpec((rows, SSM_INNER), row),
                  pl.BlockSpec((rows, SSM_CONV_CH), row),
                  pl.BlockSpec((rows, LANES), row),
                  _resident(sel.shape),
                  _resident(conv_w.shape), _resident(conv_b.shape), _resident(dt_bias.shape),
                  _resident(a_log.shape), _resident(d_skip.shape), _resident(norm_g.shape)],
        out_specs=pl.BlockSpec((rows, SSM_INNER), row),
        out_shape=jax.ShapeDtypeStruct((TOKENS, SSM_INNER), BF16),
        scratch_shapes=[pltpu.VMEM((q, SSM_CONV_CH), BF16),
                        pltpu.VMEM((rows, SSM_CONV_CH), F32),
                        pltpu.VMEM((SSM_STATE, SSM_INNER), F32),
                        pltpu.VMEM((rows, SSM_INNER), F32)],
        compiler_params=_params(("arbitrary", "arbitrary")),
        name="mamba2_ssd",
    )(z, xbc, dt_raw, sel, conv_w, conv_b, dt_bias, a_log, d_skip, norm_g)


def _pad_cols(w, n):
    return jnp.pad(w, ((0, 0), (0, n - w.shape[1])))


def _pad_rows(w, n):
    return jnp.pad(w, ((0, n - w.shape[0]), (0, 0)))


def _row(v):
    return v.reshape(1, -1)


def _even_mixer(h, w_in, b_f, w_conv, w_out):
    c_end = 3 * CONV_DIM
    a_end = c_end + 3 * FOX_DIM
    w_t = w_in.T
    w_c = w_t[:c_end].astype(BF16)
    w_qkv = w_t[c_end:a_end].astype(BF16)
    w_f = _pad_rows(w_t[a_end:], LANES)
    q_scale = (FOX_HEAD_DIM ** -0.5) * LOG2E
    conv_in, qkv, f_logit = _proj(h, [w_c, w_qkv, w_f], [BF16, BF16, F32], "even_in_proj",
                                  lead_scales=[(0, 1.0), (FOX_DIM, q_scale), (0, 1.0)])
    fp_col, fp_row = _fcum(f_logit, _pad_cols(_row(b_f), LANES))
    yb = _fox_attention(qkv, fp_col, fp_row)
    tiles = [(conv_in, CONV_DIM, 0), (conv_in, CONV_DIM, 1), (conv_in, CONV_DIM, 2), (yb, FOX_DIM, 0)]
    weights = [w_conv, w_out[:CONV_DIM].astype(BF16), w_out[CONV_DIM:].astype(BF16)]
    return _even_mix, tiles, weights, [pltpu.VMEM((HALO, CONV_DIM), F32)]


def _odd_mixer(h, w_in, conv_w, conv_b, dt_bias, a_log, d_skip, norm_g, w_out):
    x_end = SSM_INNER + SSM_CONV_CH
    w_t = w_in.T
    w_z = w_t[:SSM_INNER].astype(BF16)
    w_x = w_t[SSM_INNER:x_end].astype(BF16)
    w_dt = _pad_rows(w_t[x_end:], LANES)
    z, xbc, dt_raw = _proj(h, [w_z, w_x, w_dt], [BF16, BF16, F32], "odd_in_proj")
    u = _ssd(z, xbc, dt_raw, conv_w, _row(conv_b), _pad_cols(_row(dt_bias), LANES),
             _pad_cols(_row(a_log), LANES), _row(jnp.repeat(d_skip, SSM_HEAD_DIM)), _row(norm_g))
    return _odd_mix, [(u, SSM_INNER, 0)], [w_out.astype(BF16)], []


def kernel(x, p, even_w_in, even_b_f, even_conv_w, even_w_out, odd_w_in, odd_conv_w, odd_conv_b,
           odd_dt_bias, odd_a_log, odd_d_skip, odd_norm_g, odd_w_out, ln_mix_g, ln_mix_b, ffn_w_up,
           ffn_conv_w, ffn_conv_b, ffn_w_down, ln_ffn_g, ln_ffn_b, ple_w_proj, ple_w_gate,
           ple_b_gate):
    h = x.reshape(TOKENS, D_MODEL)
    rows = lambda v: v.reshape(DEPTH, 1, -1)
    stacked = (rows(ln_mix_g), rows(ln_mix_b), ffn_w_up.astype(BF16), ffn_conv_w, rows(ffn_conv_b),
               ffn_w_down.astype(BF16), rows(ln_ffn_g), rows(ln_ffn_b), ple_w_proj.astype(BF16),
               ple_w_gate.astype(BF16), rows(ple_b_gate))
    p_all = p.reshape(DEPTH * TOKENS, PLE_DIM)
    for i in range(DEPTH):
        j = i // 2
        if i % 2 == 0:
            name = "even_layer_tail"
            mix = _even_mixer(h, even_w_in[j], even_b_f[j], even_conv_w[j], even_w_out[j])
        else:
            name = "odd_layer_tail"
            mix = _odd_mixer(h, odd_w_in[j], odd_conv_w[j], odd_conv_b[j], odd_dt_bias[j],
                             odd_a_log[j], odd_d_skip[j], odd_norm_g[j], odd_w_out[j])
        h = _layer_tail(name, i, *mix, h, p_all, stacked)
    return h.reshape(BATCH, SEQ, D_MODEL)
```

```python
import functools

import jax
import jax.numpy as jnp
from jax import lax
from jax.experimental import pallas as pl
from jax.experimental.pallas import tpu as pltpu

F32 = jnp.float32
BF16 = jnp.bfloat16

D_MODEL = 1024
BATCH = 2
SEQ = 8192
DEPTH = 2
TOKENS = BATCH * SEQ

CONV_DIM = 512
CONV_WIDTH = 3
FOX_HEADS = 8
FOX_HEAD_DIM = 64
FOX_DIM = FOX_HEADS * FOX_HEAD_DIM
SSM_INNER = 2 * D_MODEL
SSM_HEAD_DIM = 64
SSM_HEADS = SSM_INNER // SSM_HEAD_DIM
SSM_GROUPS = 4
SSM_STATE = 128
SSM_CONV_WIDTH = 4
SSM_CHUNK = 128
SSM_CONV_CH = SSM_INNER + 2 * SSM_GROUPS * SSM_STATE
D_FF = 2816
FFN_CONV_WIDTH = 3
PLE_DIM = 256
LN_EPS = 1e-5
RMS_EPS = 1e-5
ALPHA = (2.0 * DEPTH) ** 0.25

LANES = 128
HALO = 8
NEG = -1e30
VMEM_LIMIT = 56 * 1024 * 1024

ROW_TILE = 512
SSD_CONV_COLS = 512
SSD_CHUNKS_PER_STEP = 4
FFN_ROW_TILE = 512
FFN_CHUNK = 256
FFN_DOWN_GROUP_ENDS = (6, 11)
ATTN_TILE = 512
LOG2E = 1.4426950408889634


def _resident(shape):
    nd = len(shape)
    return pl.BlockSpec(shape, lambda *_: (0,) * nd, pipeline_mode=pl.Buffered(1))


def _resident_layer(shape, layer):
    nd = len(shape)
    return pl.BlockSpec((None,) + tuple(shape[1:]), lambda *_: (layer,) + (0,) * (nd - 1),
                        pipeline_mode=pl.Buffered(1))


def _params(sem, flags=None):
    return pltpu.CompilerParams(dimension_semantics=sem, vmem_limit_bytes=VMEM_LIMIT, flags=flags)


def _sigmoid(x):
    return 0.5 + 0.5 * jnp.tanh(0.5 * x)


def _silu(x):
    h = 0.5 * x
    return h + h * jnp.tanh(h)


def _softplus(x):
    return jnp.maximum(x, 0.0) + jnp.log1p(jnp.exp(-jnp.abs(x)))


def _log_sigmoid(x):
    return jnp.minimum(x, 0.0) - jnp.log1p(jnp.exp(-jnp.abs(x)))


def _layer_norm(r, g, b):
    mu = jnp.mean(r, axis=-1, keepdims=True)
    d = r - mu
    var = jnp.mean(d * d, axis=-1, keepdims=True)
    return d * lax.rsqrt(var + LN_EPS) * g + b


def _causal_taps_rolled(cur, prev, w):
    k_taps = w.shape[0]
    sub = lax.broadcasted_iota(jnp.int32, prev.shape, 0)
    out = w[k_taps - 1:k_taps, :] * cur
    for k in range(k_taps - 1):
        shift = k_taps - 1 - k
        rolled = pltpu.roll(cur, shift, axis=0)
        head = jnp.where(sub < shift, pltpu.roll(prev, shift, axis=0), rolled[0:HALO, :])
        out = out + w[k:k + 1, :] * jnp.concatenate([head, rolled[HALO:, :]], axis=0)
    return out


def _dot_nt(a, b_t):
    return lax.dot_general(a, b_t, (((1,), (1,)), ((), ())), preferred_element_type=F32)


def _proj_kernel(x_ref, *refs, n_out, chunk, lead_scales, precise):
    w_refs, o_refs = refs[:n_out], refs[n_out:]
    x = x_ref[...]
    xb = x.astype(BF16)
    for w_ref, o_ref, (lead_cols, lead_scale), hi_lo in zip(w_refs, o_refs, lead_scales, precise):
        n = o_ref.shape[1]
        if hi_lo:
            x_lo = (x - xb.astype(F32)).astype(BF16)
            both = _dot_nt(xb, w_ref[...])
            cross = _dot_nt(x_lo, w_ref[0:n, :])
            o_ref[...] = (both[:, 0:n] + both[:, n:] + cross).astype(o_ref.dtype)
            continue
        for c0 in range(0, n, chunk):
            c1 = min(c0 + chunk, n)
            acc = _dot_nt(xb, w_ref[c0:c1, :])
            if c1 <= lead_cols:
                acc = acc * lead_scale
            o_ref[:, c0:c1] = acc.astype(o_ref.dtype)


def _split_bf16(w_t):
    w_hi = w_t.astype(BF16)
    w_lo = (w_t - w_hi.astype(F32)).astype(BF16)
    return jnp.concatenate([w_hi, w_lo], axis=0)


def _proj(x, ws_t, out_dtypes, name, lead_scales=None):
    m, k = x.shape
    tm = ROW_TILE
    chunk = 512
    if lead_scales is None:
        lead_scales = [(0, 1.0)] * len(ws_t)
    assert all(cols % chunk == 0 for cols, _ in lead_scales)
    precise = tuple(w.dtype == F32 for w in ws_t)
    widths = [w.shape[0] for w in ws_t]
    ws = [_split_bf16(w) if p else w for w, p in zip(ws_t, precise)]
    return pl.pallas_call(
        functools.partial(_proj_kernel, n_out=len(ws), chunk=chunk, lead_scales=tuple(lead_scales),
                          precise=precise),
        grid=(m // tm,),
        in_specs=[pl.BlockSpec((tm, k), lambda i: (i, 0))] + [_resident(w.shape) for w in ws],
        out_specs=[pl.BlockSpec((tm, n), lambda i: (i, 0)) for n in widths],
        out_shape=[jax.ShapeDtypeStruct((m, n), dt) for n, dt in zip(widths, out_dtypes)],
        compiler_params=_params(("arbitrary",)),
        name=name,
    )(x, *ws)


def _fcum_kernel(f_ref, b_ref, col_ref, row_ref, carry_ref, *, rows):
    j = pl.program_id(1)

    @pl.when(j == 0)
    def _():
        carry_ref[...] = jnp.zeros_like(carry_ref)

    r = lax.broadcasted_iota(jnp.int32, (LANES, LANES), 0)
    c = lax.broadcasted_iota(jnp.int32, (LANES, LANES), 1)
    tri = (r >= c).astype(F32)
    nh = FOX_HEADS
    carry = carry_ref[0:1, :]
    for blk in range(rows // LANES):
        sl = slice(blk * LANES, (blk + 1) * LANES)
        lf = _log_sigmoid(f_ref[sl, :] + b_ref[...])
        cs = jnp.dot(tri, lf, preferred_element_type=F32, precision=lax.Precision.HIGHEST)
        cum = cs + carry
        carry = cum[LANES - 1:LANES, :]
        f2 = cum * LOG2E
        hi = f2.astype(BF16).astype(F32)
        mid = (f2 - hi).astype(BF16).astype(F32)
        lo = (f2 - hi - mid).astype(BF16).astype(F32)
        packed = jnp.where(c < nh, hi,
                           jnp.where(c < 2 * nh, pltpu.roll(mid, nh, axis=1),
                                     jnp.where(c < 3 * nh, pltpu.roll(lo, 2 * nh, axis=1), 0.0)))
        col_ref[sl, :] = packed.astype(BF16)
        row_ref[0, :, sl] = jnp.concatenate([hi.T[0:nh, :], mid.T[0:nh, :], lo.T[0:nh, :],
                                             jnp.zeros((nh, LANES), F32)], axis=0)
    carry_ref[...] = jnp.broadcast_to(carry, carry_ref.shape)


def _fcum(f_logit, b_f):
    rows = ROW_TILE
    nblk = SEQ // rows
    return pl.pallas_call(
        functools.partial(_fcum_kernel, rows=rows),
        grid=(BATCH, nblk),
        in_specs=[pl.BlockSpec((rows, LANES), lambda b, j: (b * nblk + j, 0)),
                  pl.BlockSpec((1, LANES), lambda b, j: (0, 0))],
        out_specs=[pl.BlockSpec((rows, LANES), lambda b, j: (b * nblk + j, 0)),
                   pl.BlockSpec((1, 4 * FOX_HEADS, rows), lambda b, j: (b, 0, j))],
        out_shape=[jax.ShapeDtypeStruct((TOKENS, LANES), BF16),
                   jax.ShapeDtypeStruct((BATCH, 4 * FOX_HEADS, SEQ), F32)],
        scratch_shapes=[pltpu.VMEM((HALO, LANES), F32)],
        compiler_params=_params(("arbitrary", "arbitrary")),
        name="fox_forget_cumsum",
    )(f_logit, b_f)


def _fox_kernel(q_ref, k_ref, v_ref, fpc_ref, fpr_ref, o_ref, kaug_ref, vt_ref, qaug_ref,
                s0_ref, s1_ref, bm0_ref, bm1_ref, m_ref, acc_ref, *, t):
    hp = pl.program_id(1)
    qi = pl.program_id(2)
    d = FOX_HEAD_DIM
    nh = FOX_HEADS
    prep_rows = 1024

    @pl.when(qi == 0)
    def _():
        r = lax.broadcasted_iota(jnp.int32, (LANES, LANES), 0)
        c = lax.broadcasted_iota(jnp.int32, (LANES, LANES), 1)
        lane = lax.broadcasted_iota(jnp.int32, (1, LANES), 1)
        ones_lanes = jnp.where(lane < 3, 1.0, 0.0)
        sub = lax.broadcasted_iota(jnp.int32, (LANES, prep_rows), 0)
        for e in (0, 1):
            h = 2 * hp + e
            pick = jnp.where((c >= 3) & (c < 6) & (r == (c - 3) * nh + h), -1.0, 0.0).astype(BF16)
            for blk in range(SEQ // prep_rows):
                rows = slice(blk * prep_rows, (blk + 1) * prep_rows)
                aug = jnp.dot(fpc_ref[rows, :], pick, preferred_element_type=F32) + ones_lanes
                kaug_ref[e, rows, :] = aug.astype(BF16)
                v_t = v_ref[rows, :].astype(F32).T
                own = (sub < d) if e == 0 else (sub >= d)
                vt_ref[e, :, rows] = jnp.where(own, v_t, 1.0).astype(BF16)

    q_t = q_ref[...].astype(F32).T
    sub_q = lax.broadcasted_iota(jnp.int32, (LANES, t), 0)
    sub8 = lax.broadcasted_iota(jnp.int32, (HALO, t), 0)
    q_aug = []
    for e in (0, 1):
        h = 2 * hp + e
        own = (sub_q < d) if e == 0 else (sub_q >= d)
        f_hi = fpr_ref[0, pl.ds(h, 1), :]
        f_mid = fpr_ref[0, pl.ds(nh + h, 1), :]
        f_lo = fpr_ref[0, pl.ds(2 * nh + h, 1), :]
        top = jnp.where(sub8 == 0, f_hi,
                        jnp.where(sub8 == 1, f_mid,
                                  jnp.where(sub8 == 2, f_lo, jnp.where(sub8 < 6, 1.0, 0.0))))
        q_aug.append(jnp.concatenate(
            [jnp.where(own, q_t, 0.0), top, jnp.zeros((LANES - HALO, t), F32)], axis=0).astype(BF16))

    for e in (0, 1):
        qaug_ref[e] = q_aug[e]
        m_ref[e] = jnp.full((1, t), NEG, F32)
        acc_ref[e] = jnp.zeros((LANES, t), F32)

    key_idx = lax.broadcasted_iota(jnp.int32, (t, t), 0)
    qry_idx = lax.broadcasted_iota(jnp.int32, (t, t), 1)
    causal = key_idx <= qry_idx

    def score_stage(j, s_ref, bm_ref):
        start = pl.multiple_of(j * t, t)
        k2 = k_ref[pl.ds(start, t), :]
        for e in (0, 1):
            k_aug = jnp.concatenate([k2, kaug_ref[e, pl.ds(start, t), :]], axis=1)
            s_t = jnp.dot(k_aug, qaug_ref[e], preferred_element_type=F32)
            s_ref[e] = s_t
            bm_ref[e] = jnp.max(s_t, axis=0, keepdims=True)

    def softmax_stage(j, s_ref, bm_ref, masked):
        start = pl.multiple_of(j * t, t)
        for e in (0, 1):
            s_t = s_ref[e]
            if masked:
                s_t = jnp.where(causal, s_t, NEG)
                bm = jnp.max(s_t, axis=0, keepdims=True)
            else:
                bm = bm_ref[e]
            m_prev = m_ref[e]
            m_new = jnp.maximum(m_prev, bm)
            m_ref[e] = m_new
            p_t = jnp.exp2(s_t - m_new).astype(BF16)
            pv = jnp.dot(vt_ref[e, :, pl.ds(start, t)], p_t, preferred_element_type=F32)
            acc_ref[e] = jnp.exp2(m_prev - m_new) * acc_ref[e] + pv

    score_stage(qi, s1_ref, bm1_ref)
    score_stage(0, s0_ref, bm0_ref)
    softmax_stage(qi, s1_ref, bm1_ref, True)

    def two_blocks(i, _):
        j = 2 * i
        score_stage(j + 1, s1_ref, bm1_ref)
        softmax_stage(j, s0_ref, bm0_ref, False)
        score_stage(j + 2, s0_ref, bm0_ref)
        softmax_stage(j + 1, s1_ref, bm1_ref, False)
        return 0

    def four_blocks(i, _):
        two_blocks(2 * i, 0)
        two_blocks(2 * i + 1, 0)
        return 0

    quads = lax.shift_right_logical(qi, 2)
    lax.fori_loop(0, quads, four_blocks, 0)
    lax.fori_loop(2 * quads, lax.shift_right_logical(qi, 1), two_blocks, 0)

    @pl.when(lax.bitwise_and(qi, 1) == 1)
    def _():
        softmax_stage(qi - 1, s0_ref, bm0_ref, False)

    a0 = acc_ref[0]
    a1 = acc_ref[1]
    o_t = jnp.concatenate([a0[0:d, :] / a0[d:d + 1, :], a1[d:, :] / a1[0:1, :]], axis=0)
    o_ref[...] = o_t.T.astype(o_ref.dtype)


def _fox_attention(qkv, fp_col, fp_row):
    t = ATTN_TILE
    nq = SEQ // t
    pairs = FOX_HEADS // 2
    return pl.pallas_call(
        functools.partial(_fox_kernel, t=t),
        grid=(BATCH, pairs, nq),
        in_specs=[pl.BlockSpec((t, LANES), lambda b, h, i: (b * nq + i, h)),
                  pl.BlockSpec((SEQ, LANES), lambda b, h, i: (b, pairs + h)),
                  pl.BlockSpec((SEQ, LANES), lambda b, h, i: (b, 2 * pairs + h)),
                  pl.BlockSpec((SEQ, LANES), lambda b, h, i: (b, 0)),
                  pl.BlockSpec((1, 4 * FOX_HEADS, t), lambda b, h, i: (b, 0, i))],
        out_specs=pl.BlockSpec((t, LANES), lambda b, h, i: (b * nq + i, h)),
        out_shape=jax.ShapeDtypeStruct((TOKENS, FOX_DIM), BF16),
        scratch_shapes=[pltpu.VMEM((2, SEQ, LANES), BF16),
                        pltpu.VMEM((2, LANES, SEQ), BF16),
                        pltpu.VMEM((2, 2 * LANES, t), BF16),
                        pltpu.VMEM((2, t, t), F32),
                        pltpu.VMEM((2, t, t), F32),
                        pltpu.VMEM((2, 1, t), F32),
                        pltpu.VMEM((2, 1, t), F32),
                        pltpu.VMEM((2, 1, t), F32),
                        pltpu.VMEM((2, LANES, t), F32)],
        compiler_params=_params(("arbitrary", "arbitrary", "arbitrary")),
        name="fox_attention",
    )(qkv, qkv, qkv, fp_col, fp_row)


def _even_mix(first_tile, gb_ref, gc_ref, hh_ref, yb_ref, wc_ref, wa_ref, wb_ref, chalo_ref):
    @pl.when(first_tile)
    def _():
        chalo_ref[...] = jnp.zeros_like(chalo_ref)

    u = gc_ref[...].astype(F32) * hh_ref[...].astype(F32)
    prev = chalo_ref[...]
    chalo_ref[...] = u[u.shape[0] - HALO:, :]
    ya = (gb_ref[...].astype(F32) * _causal_taps_rolled(u, prev, wc_ref[...])).astype(BF16)
    mix = jnp.dot(ya, wa_ref[...], preferred_element_type=F32)
    return mix + jnp.dot(yb_ref[...], wb_ref[...], preferred_element_type=F32)


def _odd_mix(first_tile, u_ref, w_ref):
    return jnp.dot(u_ref[...], w_ref[...], preferred_element_type=F32)


def _tail_kernel(*refs, mix_fn, n_mix, tm, tiles_per_seq):
    mix_refs = refs[:n_mix]
    (h_ref, p_ref, g1_ref, b1_ref, wup_ref, cw_ref, cb_ref, wdn_ref, g_ref, b_ref, wproj_ref, wgate_ref,
     bgate_ref, o_ref, halo_ref, act_ref, acc_ref) = refs[n_mix:n_mix + 17]
    mix_scratch = refs[n_mix + 17:]
    i = pl.program_id(0)
    first_tile = lax.rem(i, tiles_per_seq) == 0

    @pl.when(first_tile)
    def _():
        halo_ref[...] = jnp.zeros_like(halo_ref)

    mix = mix_fn(first_tile, *mix_refs, *mix_scratch)
    x = _layer_norm(ALPHA * h_ref[...] + mix, g1_ref[...], b1_ref[...])
    xb = x.astype(BF16)
    tf = FFN_CHUNK
    n_chunks = D_FF // tf

    def up(c):
        return [jnp.dot(xb, wup_ref[:, part * D_FF + c * tf:part * D_FF + (c + 1) * tf],
                        preferred_element_type=F32) for part in (0, 1)]

    us = up(0)
    group_start = 0
    for c in range(n_chunks):
        us_next = up(c + 1) if c + 1 < n_chunks else None
        branches = []
        for part in (0, 1):
            c0 = part * D_FF + c * tf
            u = us[part]
            prev = halo_ref[:, c0:c0 + tf]
            halo_ref[:, c0:c0 + tf] = u[tm - HALO:, :]
            branches.append(_causal_taps_rolled(u, prev, cw_ref[:, c0:c0 + tf]) + cb_ref[:, c0:c0 + tf])
        us = us_next
        act_ref[:, c * tf:(c + 1) * tf] = (_silu(branches[0]) * branches[1]).astype(BF16)
        if (c + 1) in FFN_DOWN_GROUP_ENDS:
            k0, k1 = group_start * tf, (c + 1) * tf
            d = jnp.dot(act_ref[:, k0:k1], wdn_ref[k0:k1, :], preferred_element_type=F32)
            if group_start == 0:
                acc_ref[...] = d
            else:
                acc_ref[...] += d
            group_start = c + 1
    h2 = _layer_norm(ALPHA * x + acc_ref[...], g_ref[...], b_ref[...])
    gate_logit = jnp.dot(h2.astype(BF16), wgate_ref[...], preferred_element_type=F32) + bgate_ref[...]
    emb = jnp.dot(p_ref[...].astype(BF16), wproj_ref[...], preferred_element_type=F32)
    o_ref[...] = h2 + _sigmoid(gate_logit) * emb


def _layer_tail(name, layer, mix_fn, mix_tiles, mix_weights, mix_scratch, h, p_all, stacked):
    tm = FFN_ROW_TILE
    tiles_per_layer = TOKENS // tm
    tile_specs = [pl.BlockSpec((tm, width), functools.partial(lambda i, col: (i, col), col=col))
                  for _, width, col in mix_tiles]
    mix_arrays = [a for a, _, _ in mix_tiles] + list(mix_weights)
    rest = [h, p_all, *stacked]
    return pl.pallas_call(
        functools.partial(_tail_kernel, mix_fn=mix_fn, n_mix=len(mix_arrays), tm=tm,
                          tiles_per_seq=SEQ // tm),
        grid=(TOKENS // tm,),
        in_specs=(tile_specs + [_resident(w.shape) for w in mix_weights]
                  + [pl.BlockSpec((tm, D_MODEL), lambda i: (i, 0)),
                     pl.BlockSpec((tm, PLE_DIM), lambda i: (layer * tiles_per_layer + i, 0))]
                  + [_resident_layer(a.shape, layer) for a in stacked]),
        out_specs=pl.BlockSpec((tm, D_MODEL), lambda i: (i, 0)),
        out_shape=jax.ShapeDtypeStruct((TOKENS, D_MODEL), F32),
        scratch_shapes=[pltpu.VMEM((HALO, 2 * D_FF), F32),
                        pltpu.VMEM((tm, D_FF), BF16),
                        pltpu.VMEM((tm, D_MODEL), F32)] + list(mix_scratch),
        compiler_params=_params(("arbitrary",)),
        name=name,
    )(*mix_arrays, *rest)


def _ssd_tile(z_ref, xbc_ref, dt_ref, o_ref, out_row0, consts, scratch, fillers):
    cw_ref, cb_ref, dtb_ref, alog_ref, dsk_ref, ng_ref = consts
    xprev_ref, xc_ref, s_ref = scratch
    fillers = iter(fillers)
    q = SSM_CHUNK
    n = SSM_STATE
    cw = cw_ref[...]
    row =lax.broadcasted_iota(jnp.int32, (q, q), 0)
    col = lax.broadcasted_iota(jnp.int32, (q, q), 1)
    causal = row >= col
    lo = lax.broadcasted_iota(jnp.int32, (q, LANES), 1) < SSM_HEAD_DIM
    heads_per_group = SSM_HEADS // SSM_GROUPS
    group_w = heads_per_group * SSM_HEAD_DIM

    for sub in range(SSD_CHUNKS_PER_STEP):
        rows = slice(sub * q, (sub + 1) * q)

        for c0 in range(0, SSM_CONV_CH, SSD_CONV_COLS):
            cols = slice(c0, c0 + SSD_CONV_COLS)
            x_cur = xbc_ref[rows, cols].astype(F32)
            if sub == 0:
                prev = xprev_ref[:, cols]
            else:
                prev = xbc_ref[sub * q - 2 * HALO:sub * q, cols].astype(F32)[HALO:, :]
            conv = _causal_taps_rolled(x_cur, prev, cw[:, cols]) + cb_ref[:, cols]
            xc_ref[rows, cols] = _silu(conv).astype(BF16)

        dt = _softplus(dt_ref[rows, :] + dtb_ref[...])
        a = dt * (-LOG2E * jnp.exp(alog_ref[...]))
        acs = jnp.dot(causal.astype(F32), a, preferred_element_type=F32,
                      precision=lax.Precision.HIGHEST)
        dt_t = dt.T
        acs_t = acs.T
        tot = acs_t[:, q - 1:q]
        w_t = dt_t * jnp.exp2(tot - acs_t)
        src_t = acs_t - jnp.log2(dt_t)
        eacs = jnp.exp2(acs)
        dec = jnp.exp2(tot)

        for g in range(SSM_GROUPS):
            next(fillers, lambda: None)()
            b0 = SSM_INNER + g * n
            c0 = SSM_INNER + SSM_GROUPS * n + g * n
            bg_b = xc_ref[rows, b0:b0 + n]
            cg_b = xc_ref[rows, c0:c0 + n]
            cb = lax.dot_general(cg_b, bg_b, (((1,), (1,)), ((), ())),
                                 preferred_element_type=F32)
            cg = cg_b.astype(F32)
            bg_t = bg_b.astype(F32).T
            gated = []
            for pr in range(heads_per_group // 2):
                j = g * (heads_per_group // 2) + pr
                sl = slice(j * LANES, (j + 1) * LANES)
                xb = xc_ref[rows, sl]
                x = xb.astype(F32)
                s_prev = s_ref[:, sl]
                rhs = jnp.concatenate([xb, s_prev.astype(BF16)], axis=0)
                ys, news, decs = [], [], []
                for e in (0, 1):
                    h = 2 * j + e
                    seg = acs[:, h:h + 1] - src_t[h:h + 1, :]
                    m_h = (cb * jnp.exp2(jnp.where(causal, seg, NEG))).astype(BF16)
                    w2 = (cg * eacs[:, h:h + 1]).astype(BF16)
                    lhs = jnp.concatenate([m_h, w2], axis=1)
                    ys.append(jnp.dot(lhs, rhs, preferred_element_type=F32))
                    bw_t = (bg_t * w_t[h:h + 1, :]).astype(BF16)
                    news.append(jnp.dot(bw_t, xb, preferred_element_type=F32))
                    decs.append(jnp.broadcast_to(dec[h:h + 1, :], (n, LANES)))
                y = jnp.where(lo, ys[0], ys[1])
                s_ref[:, sl] = (s_prev * jnp.where(lo, decs[0], decs[1])
                                + jnp.where(lo, news[0], news[1]))
                y = y + dsk_ref[:, sl] * x
                gated.append(y * _silu(z_ref[rows, sl].astype(F32)))
            gs = slice(g * group_w, (g + 1) * group_w)
            ug = jnp.concatenate(gated, axis=1)
            ms = jnp.mean(ug * ug, axis=-1, keepdims=True)
            out_rows = slice(out_row0 + sub * q, out_row0 + (sub + 1) * q)
            o_ref[out_rows, gs] = (ug * lax.rsqrt(ms + RMS_EPS) * ng_ref[:, gs]).astype(o_ref.dtype)

    last = SSD_CHUNKS_PER_STEP * q
    xprev_ref[...] = xbc_ref[last - 2 * HALO:last, :].astype(F32)[HALO:, :]
    for filler in fillers:
        filler()


def _in_proj_pieces(h_ref, w_refs, dst_refs, xb_ref, xlo_ref):
    wz_ref, wx_ref, wdt_ref = w_refs
    z_ref, xbc_ref, dt_ref = dst_refs
    chunk = 512

    def split():
        x = h_ref[...]
        xb = x.astype(BF16)
        xb_ref[...] = xb
        xlo_ref[...] = (x - xb.astype(F32)).astype(BF16)

    def columns(w_ref, o_ref, c0):
        def piece():
            o_ref[:, c0:c0 + chunk] = _dot_nt(xb_ref[...], w_ref[c0:c0 + chunk, :]).astype(o_ref.dtype)
        return piece

    def dt_logits():
        both = _dot_nt(xb_ref[...], wdt_ref[...])
        cross = _dot_nt(xlo_ref[...], wdt_ref[0:LANES, :])
        dt_ref[...] = both[:, 0:LANES] + both[:, LANES:] + cross

    pieces = [split]
    pieces += [columns(wx_ref, xbc_ref, c0) for c0 in range(0, SSM_CONV_CH, chunk)]
    pieces += [dt_logits]
    pieces += [columns(wz_ref, z_ref, c0) for c0 in range(0, SSM_INNER, chunk)]
    return pieces


def _odd_mixer_kernel(h_first_ref, h_odd_ref, h_next_ref, wz_ref, wx_ref, wdt_ref, cw_ref, cb_ref,
                      dtb_ref, alog_ref, dsk_ref, ng_ref, o_ref,
                      xb_ref, xlo_ref, z0_ref, xbc0_ref, dt0_ref, z1_ref, xbc1_ref, dt1_ref,
                      xprev_ref, xc_ref, s_ref, *, tm, steps_per_seq):
    g = pl.program_id(0)
    w_refs = (wz_ref, wx_ref, wdt_ref)
    slot0 = (z0_ref, xbc0_ref, dt0_ref)
    slot1 = (z1_ref, xbc1_ref, dt1_ref)
    consts = (cw_ref, cb_ref, dtb_ref, alog_ref, dsk_ref, ng_ref)
    scratch = (xprev_ref, xc_ref, s_ref)

    @pl.when(lax.rem(g, steps_per_seq) == 0)
    def _():
        xprev_ref[...] = jnp.zeros_like(xprev_ref)
        s_ref[...] = jnp.zeros_like(s_ref)

    @pl.when(g == 0)
    def _():
        for piece in _in_proj_pieces(h_first_ref, w_refs, slot0, xb_ref, xlo_ref):
            piece()

    _ssd_tile(*slot0, o_ref, 0, consts, scratch,
              _in_proj_pieces(h_odd_ref, w_refs, slot1, xb_ref, xlo_ref))
    _ssd_tile(*slot1, o_ref, tm, consts, scratch,
              _in_proj_pieces(h_next_ref, w_refs, slot0, xb_ref, xlo_ref))


def _odd_in_proj_ssd(h, w_z, w_x, w_dt, conv_w, conv_b, dt_bias, a_log, d_skip, norm_g):
    tm = SSD_CHUNKS_PER_STEP * SSM_CHUNK
    n_tiles = TOKENS // tm
    steps = n_tiles // 2
    w_dt = _split_bf16(w_dt)
    tile = lambda index_map: pl.BlockSpec((tm, D_MODEL), index_map, pipeline_mode=pl.Buffered(1))
    consts = [w_z, w_x, w_dt, conv_w, conv_b, dt_bias, a_log, d_skip, norm_g]
    slot = [pltpu.VMEM((tm, SSM_INNER), BF16), pltpu.VMEM((tm, SSM_CONV_CH), BF16),
            pltpu.VMEM((tm, LANES), F32)]
    return pl.pallas_call(
        functools.partial(_odd_mixer_kernel, tm=tm, steps_per_seq=SEQ // (2 * tm)),
        grid=(steps,),
        in_specs=[pl.BlockSpec((tm, D_MODEL), lambda g: (0, 0), pipeline_mode=pl.Buffered(1)),
                  tile(lambda g: (2 * g + 1, 0)),
                  tile(lambda g: (jnp.minimum(2 * g + 2, n_tiles - 1), 0))]
                 + [_resident(a.shape) for a in consts],
        out_specs=pl.BlockSpec((2 * tm, SSM_INNER), lambda g: (g, 0)),
        out_shape=jax.ShapeDtypeStruct((TOKENS, SSM_INNER), BF16),
        scratch_shapes=[pltpu.VMEM((tm, D_MODEL), BF16),
                        pltpu.VMEM((tm, D_MODEL), BF16)]
                       + slot + slot
                       + [pltpu.VMEM((HALO, SSM_CONV_CH), F32),
                          pltpu.VMEM((tm, SSM_CONV_CH), BF16),
                          pltpu.VMEM((SSM_STATE, SSM_INNER), F32)],
        compiler_params=_params(("arbitrary",)),
        name="mamba2_in_proj_ssd",
    )(h, h, h, *consts)


def _pad_cols(w, n):
    return jnp.pad(w, ((0, 0), (0, n - w.shape[1])))


def _pad_rows(w, n):
    return jnp.pad(w, ((0, n - w.shape[0]), (0, 0)))


def _row(v):
    return v.reshape(1, -1)


def _even_mixer(h, w_in, b_f, w_conv, w_out):
    c_end = 3 * CONV_DIM
    a_end = c_end + 3 * FOX_DIM
    w_t = w_in.T
    w_c = w_t[:c_end].astype(BF16)
    w_qkv = w_t[c_end:a_end].astype(BF16)
    w_f = _pad_rows(w_t[a_end:], LANES)
    q_scale = (FOX_HEAD_DIM ** -0.5) * LOG2E
    conv_in, qkv, f_logit = _proj(h, [w_c, w_qkv, w_f], [BF16, BF16, F32], "even_in_proj",
                                  lead_scales=[(0, 1.0), (FOX_DIM, q_scale), (0, 1.0)])
    fp_col, fp_row = _fcum(f_logit, _pad_cols(_row(b_f), LANES))
    yb = _fox_attention(qkv, fp_col, fp_row)
    tiles = [(conv_in, CONV_DIM, 0), (conv_in, CONV_DIM, 1), (conv_in, CONV_DIM, 2), (yb, FOX_DIM, 0)]
    weights = [w_conv, w_out[:CONV_DIM].astype(BF16), w_out[CONV_DIM:].astype(BF16)]
    return _even_mix, tiles, weights, [pltpu.VMEM((HALO, CONV_DIM), F32)]


def _odd_mixer(h, w_in, conv_w, conv_b, dt_bias, a_log, d_skip, norm_g, w_out):
    x_end = SSM_INNER + SSM_CONV_CH
    w_t = w_in.T
    w_z = w_t[:SSM_INNER].astype(BF16)
    w_x = w_t[SSM_INNER:x_end].astype(BF16)
    w_dt = _pad_rows(w_t[x_end:], LANES)
    u = _odd_in_proj_ssd(h, w_z, w_x, w_dt, conv_w, _row(conv_b), _pad_cols(_row(dt_bias), LANES),
                         _pad_cols(_row(a_log), LANES), _row(jnp.repeat(d_skip, SSM_HEAD_DIM)),
                         _row(norm_g))
    return _odd_mix, [(u, SSM_INNER, 0)], [w_out.astype(BF16)], []


def kernel(x, p, even_w_in, even_b_f, even_conv_w, even_w_out, odd_w_in, odd_conv_w, odd_conv_b,
           odd_dt_bias, odd_a_log, odd_d_skip, odd_norm_g, odd_w_out, ln_mix_g, ln_mix_b, ffn_w_up,
           ffn_conv_w, ffn_conv_b, ffn_w_down, ln_ffn_g, ln_ffn_b, ple_w_proj, ple_w_gate,
           ple_b_gate):
    h = x.reshape(TOKENS, D_MODEL)
    rows = lambda v: v.reshape(DEPTH, 1, -1)
    stacked = (rows(ln_mix_g), rows(ln_mix_b), ffn_w_up.astype(BF16), ffn_conv_w, rows(ffn_conv_b),
               ffn_w_down.astype(BF16), rows(ln_ffn_g), rows(ln_ffn_b), ple_w_proj.astype(BF16),
               ple_w_gate.astype(BF16), rows(ple_b_gate))
    p_all = p.reshape(DEPTH * TOKENS, PLE_DIM)
    for i in range(DEPTH):
        j = i // 2
        if i % 2 == 0:
            name = "even_layer_tail"
            mix = _even_mixer(h, even_w_in[j], even_b_f[j], even_conv_w[j], even_w_out[j])
        else:
            name = "odd_layer_tail"
            mix = _odd_mixer(h, odd_w_in[j], odd_conv_w[j], odd_conv_b[j], odd_dt_bias[j],
                             odd_a_log[j], odd_d_skip[j], odd_norm_g[j], odd_w_out[j])
        h = _layer_tail(name, i, *mix, h, p_all, stacked)
    return h.reshape(BATCH, SEQ, D_MODEL)
```

```python
import functools

import jax
import jax.numpy as jnp
from jax import lax
from jax.experimental import pallas as pl
from jax.experimental.pallas import tpu as pltpu

F32 = jnp.float32
BF16 = jnp.bfloat16

D_MODEL = 1024
BATCH = 2
SEQ = 8192
DEPTH = 2
TOKENS = BATCH * SEQ

CONV_DIM = 512
CONV_WIDTH = 3
FOX_HEADS = 8
FOX_HEAD_DIM = 64
FOX_DIM = FOX_HEADS * FOX_HEAD_DIM
SSM_INNER = 2 * D_MODEL
SSM_HEAD_DIM = 64
SSM_HEADS = SSM_INNER // SSM_HEAD_DIM
SSM_GROUPS = 4
SSM_STATE = 128
SSM_CONV_WIDTH = 4
SSM_CHUNK = 128
SSM_CONV_CH = SSM_INNER + 2 * SSM_GROUPS * SSM_STATE
D_FF = 2816
FFN_CONV_WIDTH = 3
PLE_DIM = 256
LN_EPS = 1e-5
RMS_EPS = 1e-5
ALPHA = (2.0 * DEPTH) ** 0.25

LANES = 128
HALO = 8
NEG = -1e30
VMEM_LIMIT = 56 * 1024 * 1024

ROW_TILE = 512
SSD_CONV_COLS = 512
SSD_CHUNKS_PER_STEP = 4
FFN_ROW_TILE = 512
FFN_CHUNK = 256
FFN_DOWN_GROUP_ENDS = (6, 11)
ATTN_TILE = 512
LOG2E = 1.4426950408889634


def _resident(shape):
    nd = len(shape)
    return pl.BlockSpec(shape, lambda *_: (0,) * nd, pipeline_mode=pl.Buffered(1))


def _resident_layer(shape, layer):
    nd = len(shape)
    return pl.BlockSpec((None,) + tuple(shape[1:]), lambda *_: (layer,) + (0,) * (nd - 1),
                        pipeline_mode=pl.Buffered(1))


def _params(sem, flags=None):
    return pltpu.CompilerParams(dimension_semantics=sem, vmem_limit_bytes=VMEM_LIMIT, flags=flags)


def _sigmoid(x):
    return 0.5 + 0.5 * jnp.tanh(0.5 * x)


def _silu_of_half(h):
    return h + h * jnp.tanh(h)


def _silu(x):
    return _silu_of_half(0.5 * x)


def _softplus(x):
    return jnp.maximum(x, 0.0) + jnp.log1p(jnp.exp(-jnp.abs(x)))


def _log_sigmoid(x):
    return jnp.minimum(x, 0.0) - jnp.log1p(jnp.exp(-jnp.abs(x)))


def _layer_norm(r, g, b):
    mu = jnp.mean(r, axis=-1, keepdims=True)
    d = r - mu
    var = jnp.mean(d * d, axis=-1, keepdims=True)
    return d * lax.rsqrt(var + LN_EPS) * g + b


def _causal_taps_rolled(cur, prev, w):
    k_taps = w.shape[0]
    sub = lax.broadcasted_iota(jnp.int32, prev.shape, 0)
    out = w[k_taps - 1:k_taps, :] * cur
    for k in range(k_taps - 1):
        shift = k_taps - 1 - k
        rolled = pltpu.roll(cur, shift, axis=0)
        head = jnp.where(sub < shift, pltpu.roll(prev, shift, axis=0), rolled[0:HALO, :])
        out = out + w[k:k + 1, :] * jnp.concatenate([head, rolled[HALO:, :]], axis=0)
    return out


def _dot_nt(a, b_t):
    return lax.dot_general(a, b_t, (((1,), (1,)), ((), ())), preferred_element_type=F32)


def _proj_kernel(x_ref, *refs, n_out, chunk, lead_scales, precise):
    w_refs, o_refs = refs[:n_out], refs[n_out:]
    x = x_ref[...]
    xb = x.astype(BF16)
    for w_ref, o_ref, (lead_cols, lead_scale), hi_lo in zip(w_refs, o_refs, lead_scales, precise):
        n = o_ref.shape[1]
        if hi_lo:
            x_lo = (x - xb.astype(F32)).astype(BF16)
            both = _dot_nt(xb, w_ref[...])
            cross = _dot_nt(x_lo, w_ref[0:n, :])
            o_ref[...] = (both[:, 0:n] + both[:, n:] + cross).astype(o_ref.dtype)
            continue
        for c0 in range(0, n, chunk):
            c1 = min(c0 + chunk, n)
            acc = _dot_nt(xb, w_ref[c0:c1, :])
            if c1 <= lead_cols:
                acc = acc * lead_scale
            o_ref[:, c0:c1] = acc.astype(o_ref.dtype)


def _split_bf16(w_t):
    w_hi = w_t.astype(BF16)
    w_lo = (w_t - w_hi.astype(F32)).astype(BF16)
    return jnp.concatenate([w_hi, w_lo], axis=0)


def _proj(x, ws_t, out_dtypes, name, lead_scales=None):
    m, k = x.shape
    tm = ROW_TILE
    chunk = 512
    if lead_scales is None:
        lead_scales = [(0, 1.0)] * len(ws_t)
    assert all(cols % chunk == 0 for cols, _ in lead_scales)
    precise = tuple(w.dtype == F32 for w in ws_t)
    widths = [w.shape[0] for w in ws_t]
    ws = [_split_bf16(w) if p else w for w, p in zip(ws_t, precise)]
    return pl.pallas_call(
        functools.partial(_proj_kernel, n_out=len(ws), chunk=chunk, lead_scales=tuple(lead_scales),
                          precise=precise),
        grid=(m // tm,),
        in_specs=[pl.BlockSpec((tm, k), lambda i: (i, 0))] + [_resident(w.shape) for w in ws],
        out_specs=[pl.BlockSpec((tm, n), lambda i: (i, 0)) for n in widths],
        out_shape=[jax.ShapeDtypeStruct((m, n), dt) for n, dt in zip(widths, out_dtypes)],
        compiler_params=_params(("arbitrary",)),
        name=name,
    )(x, *ws)


def _fcum_kernel(f_ref, b_ref, col_ref, row_ref, carry_ref, *, rows):
    j = pl.program_id(1)

    @pl.when(j == 0)
    def _():
        carry_ref[...] = jnp.zeros_like(carry_ref)

    r = lax.broadcasted_iota(jnp.int32, (LANES, LANES), 0)
    c = lax.broadcasted_iota(jnp.int32, (LANES, LANES), 1)
    tri = (r >= c).astype(F32)
    nh = FOX_HEADS
    carry = carry_ref[0:1, :]
    for blk in range(rows // LANES):
        sl = slice(blk * LANES, (blk + 1) * LANES)
        lf = _log_sigmoid(f_ref[sl, :] + b_ref[...])
        cs = jnp.dot(tri, lf, preferred_element_type=F32, precision=lax.Precision.HIGHEST)
        cum = cs + carry
        carry = cum[LANES - 1:LANES, :]
        f2 = cum * LOG2E
        hi = f2.astype(BF16).astype(F32)
        mid = (f2 - hi).astype(BF16).astype(F32)
        lo = (f2 - hi - mid).astype(BF16).astype(F32)
        packed = jnp.where(c < nh, hi,
                           jnp.where(c < 2 * nh, pltpu.roll(mid, nh, axis=1),
                                     jnp.where(c < 3 * nh, pltpu.roll(lo, 2 * nh, axis=1), 0.0)))
        col_ref[sl, :] = packed.astype(BF16)
        row_ref[0, :, sl] = jnp.concatenate([hi.T[0:nh, :], mid.T[0:nh, :], lo.T[0:nh, :],
                                             jnp.zeros((nh, LANES), F32)], axis=0)
    carry_ref[...] = jnp.broadcast_to(carry, carry_ref.shape)


def _fcum(f_logit, b_f):
    rows = ROW_TILE
    nblk = SEQ // rows
    return pl.pallas_call(
        functools.partial(_fcum_kernel, rows=rows),
        grid=(BATCH, nblk),
        in_specs=[pl.BlockSpec((rows, LANES), lambda b, j: (b * nblk + j, 0)),
                  pl.BlockSpec((1, LANES), lambda b, j: (0, 0))],
        out_specs=[pl.BlockSpec((rows, LANES), lambda b, j: (b * nblk + j, 0)),
                   pl.BlockSpec((1, 4 * FOX_HEADS, rows), lambda b, j: (b, 0, j))],
        out_shape=[jax.ShapeDtypeStruct((TOKENS, LANES), BF16),
                   jax.ShapeDtypeStruct((BATCH, 4 * FOX_HEADS, SEQ), F32)],
        scratch_shapes=[pltpu.VMEM((HALO, LANES), F32)],
        compiler_params=_params(("arbitrary", "arbitrary")),
        name="fox_forget_cumsum",
    )(f_logit, b_f)


def _fox_kernel(q_ref, k_ref, v_ref, fpc_ref, fpr_ref, o_ref, kaug_ref, vt_ref, qaug_ref,
                s0_ref, s1_ref, bm0_ref, bm1_ref, m_ref, acc_ref, *, t):
    hp = pl.program_id(1)
    qi = pl.program_id(2)
    d = FOX_HEAD_DIM
    nh = FOX_HEADS
    prep_rows = 1024

    @pl.when(qi == 0)
    def _():
        r = lax.broadcasted_iota(jnp.int32, (LANES, LANES), 0)
        c = lax.broadcasted_iota(jnp.int32, (LANES, LANES), 1)
        lane = lax.broadcasted_iota(jnp.int32, (1, LANES), 1)
        ones_lanes = jnp.where(lane < 3, 1.0, 0.0)
        sub = lax.broadcasted_iota(jnp.int32, (LANES, prep_rows), 0)
        for e in (0, 1):
            h = 2 * hp + e
            pick = jnp.where((c >= 3) & (c < 6) & (r == (c - 3) * nh + h), -1.0, 0.0).astype(BF16)
            for blk in range(SEQ // prep_rows):
                rows = slice(blk * prep_rows, (blk + 1) * prep_rows)
                aug = jnp.dot(fpc_ref[rows, :], pick, preferred_element_type=F32) + ones_lanes
                kaug_ref[e, rows, :] = aug.astype(BF16)
                v_t = v_ref[rows, :].astype(F32).T
                own = (sub < d) if e == 0 else (sub >= d)
                vt_ref[e, :, rows] = jnp.where(own, v_t, 1.0).astype(BF16)

    q_t = q_ref[...].astype(F32).T
    sub_q = lax.broadcasted_iota(jnp.int32, (LANES, t), 0)
    sub8 = lax.broadcasted_iota(jnp.int32, (HALO, t), 0)
    q_aug = []
    for e in (0, 1):
        h = 2 * hp + e
        own = (sub_q < d) if e == 0 else (sub_q >= d)
        f_hi = fpr_ref[0, pl.ds(h, 1), :]
        f_mid = fpr_ref[0, pl.ds(nh + h, 1), :]
        f_lo = fpr_ref[0, pl.ds(2 * nh + h, 1), :]
        top = jnp.where(sub8 == 0, f_hi,
                        jnp.where(sub8 == 1, f_mid,
                                  jnp.where(sub8 == 2, f_lo, jnp.where(sub8 < 6, 1.0, 0.0))))
        q_aug.append(jnp.concatenate(
            [jnp.where(own, q_t, 0.0), top, jnp.zeros((LANES - HALO, t), F32)], axis=0).astype(BF16))

    for e in (0, 1):
        qaug_ref[e] = q_aug[e]
        m_ref[e] = jnp.full((1, t), NEG, F32)
        acc_ref[e] = jnp.zeros((LANES, t), F32)

    key_idx = lax.broadcasted_iota(jnp.int32, (t, t), 0)
    qry_idx = lax.broadcasted_iota(jnp.int32, (t, t), 1)
    causal = key_idx <= qry_idx

    def score_stage(j, s_ref, bm_ref):
        start = pl.multiple_of(j * t, t)
        k2 = k_ref[pl.ds(start, t), :]
        for e in (0, 1):
            k_aug = jnp.concatenate([k2, kaug_ref[e, pl.ds(start, t), :]], axis=1)
            s_t = jnp.dot(k_aug, qaug_ref[e], preferred_element_type=F32)
            s_ref[e] = s_t
            bm_ref[e] = jnp.max(s_t, axis=0, keepdims=True)

    def softmax_stage(j, s_ref, bm_ref, masked):
        start = pl.multiple_of(j * t, t)
        for e in (0, 1):
            s_t = s_ref[e]
            if masked:
                s_t = jnp.where(causal, s_t, NEG)
                bm = jnp.max(s_t, axis=0, keepdims=True)
            else:
                bm = bm_ref[e]
            m_prev = m_ref[e]
            m_new = jnp.maximum(m_prev, bm)
            m_ref[e] = m_new
            p_t = jnp.exp2(s_t - m_new).astype(BF16)
            pv = jnp.dot(vt_ref[e, :, pl.ds(start, t)], p_t, preferred_element_type=F32)
            acc_ref[e] = jnp.exp2(m_prev - m_new) * acc_ref[e] + pv

    score_stage(qi, s1_ref, bm1_ref)
    score_stage(0, s0_ref, bm0_ref)
    softmax_stage(qi, s1_ref, bm1_ref, True)

    def two_blocks(i, _):
        j = 2 * i
        score_stage(j + 1, s1_ref, bm1_ref)
        softmax_stage(j, s0_ref, bm0_ref, False)
        score_stage(j + 2, s0_ref, bm0_ref)
        softmax_stage(j + 1, s1_ref, bm1_ref, False)
        return 0

    def pairs_per_trip(n_pairs):
        def body(i, _):
            for pair in range(n_pairs):
                two_blocks(n_pairs * i + pair, 0)
            return 0
        return body

    octs = lax.shift_right_logical(qi, 3)
    quads = lax.shift_right_logical(qi, 2)
    lax.fori_loop(0, octs, pairs_per_trip(4), 0)
    lax.fori_loop(2 * octs, quads, pairs_per_trip(2), 0)
    lax.fori_loop(2 * quads, lax.shift_right_logical(qi, 1), two_blocks, 0)

    @pl.when(lax.bitwise_and(qi, 1) == 1)
    def _():
        softmax_stage(qi - 1, s0_ref, bm0_ref, False)

    a0 = acc_ref[0]
    a1 = acc_ref[1]
    o_t = jnp.concatenate([a0[0:d, :] / a0[d:d + 1, :], a1[d:, :] / a1[0:1, :]], axis=0)
    o_ref[...] = o_t.T.astype(o_ref.dtype)


def _fox_attention(qkv, fp_col, fp_row):
    t = ATTN_TILE
    nq = SEQ // t
    pairs = FOX_HEADS // 2
    return pl.pallas_call(
        functools.partial(_fox_kernel, t=t),
        grid=(BATCH, pairs, nq),
        in_specs=[pl.BlockSpec((t, LANES), lambda b, h, i: (b * nq + i, h)),
                  pl.BlockSpec((SEQ, LANES), lambda b, h, i: (b, pairs + h)),
                  pl.BlockSpec((SEQ, LANES), lambda b, h, i: (b, 2 * pairs + h)),
                  pl.BlockSpec((SEQ, LANES), lambda b, h, i: (b, 0)),
                  pl.BlockSpec((1, 4 * FOX_HEADS, t), lambda b, h, i: (b, 0, i))],
        out_specs=pl.BlockSpec((t, LANES), lambda b, h, i: (b * nq + i, h)),
        out_shape=jax.ShapeDtypeStruct((TOKENS, FOX_DIM), BF16),
        scratch_shapes=[pltpu.VMEM((2, SEQ, LANES), BF16),
                        pltpu.VMEM((2, LANES, SEQ), BF16),
                        pltpu.VMEM((2, 2 * LANES, t), BF16),
                        pltpu.VMEM((2, t, t), F32),
                        pltpu.VMEM((2, t, t), F32),
                        pltpu.VMEM((2, 1, t), F32),
                        pltpu.VMEM((2, 1, t), F32),
                        pltpu.VMEM((2, 1, t), F32),
                        pltpu.VMEM((2, LANES, t), F32)],
        compiler_params=_params(("arbitrary", "arbitrary", "arbitrary")),
        name="fox_attention",
    )(qkv, qkv, qkv, fp_col, fp_row)


def _even_mix(first_tile, gb_ref, gc_ref, hh_ref, yb_ref, wc_ref, wa_ref, wb_ref, chalo_ref):
    @pl.when(first_tile)
    def _():
        chalo_ref[...] = jnp.zeros_like(chalo_ref)

    u = gc_ref[...].astype(F32) * hh_ref[...].astype(F32)
    prev = chalo_ref[...]
    chalo_ref[...] = u[u.shape[0] - HALO:, :]
    ya = (gb_ref[...].astype(F32) * _causal_taps_rolled(u, prev, wc_ref[...])).astype(BF16)
    mix = jnp.dot(ya, wa_ref[...], preferred_element_type=F32)
    return mix + jnp.dot(yb_ref[...], wb_ref[...], preferred_element_type=F32)


def _odd_mix(first_tile, u_ref, w_ref):
    return jnp.dot(u_ref[...], w_ref[...], preferred_element_type=F32)


def _tail_kernel(*refs, mix_fn, n_mix, tm, tiles_per_seq):
    mix_refs = refs[:n_mix]
    (h_ref, p_ref, g1_ref, b1_ref, wup_ref, cw_ref, cb_ref, wdn_ref, g_ref, b_ref, wproj_ref, wgate_ref,
     bgate_ref, o_ref, halo_ref, act_ref, acc_ref) = refs[n_mix:n_mix + 17]
    mix_scratch = refs[n_mix + 17:]
    i = pl.program_id(0)
    first_tile = lax.rem(i, tiles_per_seq) == 0

    @pl.when(first_tile)
    def _():
        halo_ref[...] = jnp.zeros_like(halo_ref)

    mix = mix_fn(first_tile, *mix_refs, *mix_scratch)
    x = _layer_norm(ALPHA * h_ref[...] + mix, g1_ref[...], b1_ref[...])
    xb = x.astype(BF16)
    tf = FFN_CHUNK
    n_chunks = D_FF // tf

    def up(c):
        return [jnp.dot(xb, wup_ref[:, part * D_FF + c * tf:part * D_FF + (c + 1) * tf],
                        preferred_element_type=F32) for part in (0, 1)]

    us = up(0)
    group_start = 0
    for c in range(n_chunks):
        us_next = up(c + 1) if c + 1 < n_chunks else None
        branches = []
        for part in (0, 1):
            c0 = part * D_FF + c * tf
            u = us[part]
            prev = halo_ref[:, c0:c0 + tf]
            halo_ref[:, c0:c0 + tf] = u[tm - HALO:, :]
            branches.append(_causal_taps_rolled(u, prev, cw_ref[:, c0:c0 + tf]) + cb_ref[:, c0:c0 + tf])
        us = us_next
        act_ref[:, c * tf:(c + 1) * tf] = (_silu_of_half(branches[0]) * branches[1]).astype(BF16)
        if (c + 1) in FFN_DOWN_GROUP_ENDS:
            k0, k1 = group_start * tf, (c + 1) * tf
            d = jnp.dot(act_ref[:, k0:k1], wdn_ref[k0:k1, :], preferred_element_type=F32)
            if group_start == 0:
                acc_ref[...] = d
            else:
                acc_ref[...] += d
            group_start = c + 1
    h2 = _layer_norm(ALPHA * x + acc_ref[...], g_ref[...], b_ref[...])
    gate_logit = jnp.dot(h2.astype(BF16), wgate_ref[...], preferred_element_type=F32) + bgate_ref[...]
    emb = jnp.dot(p_ref[...].astype(BF16), wproj_ref[...], preferred_element_type=F32)
    o_ref[...] = h2 + _sigmoid(gate_logit) * emb


def _layer_tail(name, layer, mix_fn, mix_tiles, mix_weights, mix_scratch, h, p_all, stacked):
    tm = FFN_ROW_TILE
    tiles_per_layer = TOKENS // tm
    tile_specs = [pl.BlockSpec((tm, width), functools.partial(lambda i, col: (i, col), col=col))
                  for _, width, col in mix_tiles]
    mix_arrays = [a for a, _, _ in mix_tiles] + list(mix_weights)
    rest = [h, p_all, *stacked]
    return pl.pallas_call(
        functools.partial(_tail_kernel, mix_fn=mix_fn, n_mix=len(mix_arrays), tm=tm,
                          tiles_per_seq=SEQ // tm),
        grid=(TOKENS // tm,),
        in_specs=(tile_specs + [_resident(w.shape) for w in mix_weights]
                  + [pl.BlockSpec((tm, D_MODEL), lambda i: (i, 0)),
                     pl.BlockSpec((tm, PLE_DIM), lambda i: (layer * tiles_per_layer + i, 0))]
                  + [_resident_layer(a.shape, layer) for a in stacked]),
        out_specs=pl.BlockSpec((tm, D_MODEL), lambda i: (i, 0)),
        out_shape=jax.ShapeDtypeStruct((TOKENS, D_MODEL), F32),
        scratch_shapes=[pltpu.VMEM((HALO, 2 * D_FF), F32),
                        pltpu.VMEM((tm, D_FF), BF16),
                        pltpu.VMEM((tm, D_MODEL), F32)] + list(mix_scratch),
        compiler_params=_params(("arbitrary",)),
        name=name,
    )(*mix_arrays, *rest)


def _shift_select(q, taps):
    r = jnp.arange((taps - 1) * q)[:, None]
    c = jnp.arange(2 * q)[None, :]
    return (c == q + r % q - (r // q + 1)).astype(BF16)


def _ssd_kernel(z_ref, xbc_ref, dt_ref, sel_ref, cw_ref, cb_ref, dtb_ref, alog_ref, dsk_ref, ng_ref,
                o_ref, xprev_ref, xc_ref, s_ref, u_ref):
    q = SSM_CHUNK
    n = SSM_STATE

    @pl.when(pl.program_id(1) == 0)
    def _():
        xprev_ref[...] = jnp.zeros_like(xprev_ref)
        s_ref[...] = jnp.zeros_like(s_ref)

    cw = cw_ref[...]
    taps = SSM_CONV_WIDTH
    sel = sel_ref[...]
    row = lax.broadcasted_iota(jnp.int32, (q, q), 0)
    col = lax.broadcasted_iota(jnp.int32, (q, q), 1)
    causal = row >= col
    lo = lax.broadcasted_iota(jnp.int32, (q, LANES), 1) < SSM_HEAD_DIM
    heads_per_group = SSM_HEADS // SSM_GROUPS
    group_w = heads_per_group * SSM_HEAD_DIM

    for sub in range(SSD_CHUNKS_PER_STEP):
        rows = slice(sub * q, (sub + 1) * q)
        prev_rows = slice((sub - 1) * q, sub * q)

        for c0 in range(0, SSM_CONV_CH, SSD_CONV_COLS):
            cols = slice(c0, c0 + SSD_CONV_COLS)
            x_cur = xbc_ref[rows, cols]
            x_prev = xprev_ref[:, cols] if sub == 0 else xbc_ref[prev_rows, cols]
            shifted = jnp.dot(sel, jnp.concatenate([x_prev, x_cur], axis=0),
                              preferred_element_type=F32)
            conv = cw[taps - 1:taps, cols] * x_cur.astype(F32) + cb_ref[:, cols]
            for k in range(1, taps):
                conv = conv + cw[taps - 1 - k:taps - k, cols] * shifted[(k - 1) * q:k * q, :]
            xc_ref[rows, cols] = _silu_of_half(conv)

        dt = _softplus(dt_ref[rows, :] + dtb_ref[...])
        a = dt * (-LOG2E * jnp.exp(alog_ref[...]))
        acs = jnp.dot(causal.astype(F32), a, preferred_element_type=F32,
                      precision=lax.Precision.HIGHEST)
        dt_t = dt.T
        acs_t = acs.T
        tot = acs_t[:, q - 1:q]
        w_t = dt_t * jnp.exp2(tot - acs_t)
        src_t = acs_t - jnp.log2(dt_t)
        eacs = jnp.exp2(acs)
        dec = jnp.exp2(tot)

        for g in range(SSM_GROUPS):
            b0 = SSM_INNER + g * n
            c0 = SSM_INNER + SSM_GROUPS * n + g * n
            bg = xc_ref[rows, b0:b0 + n]
            cg = xc_ref[rows, c0:c0 + n]
            cb = lax.dot_general(cg.astype(BF16), bg.astype(BF16), (((1,), (1,)), ((), ())),
                                 preferred_element_type=F32)
            bg_t = bg.T
            for pr in range(heads_per_group // 2):
                j = g * (heads_per_group // 2) + pr
                sl = slice(j * LANES, (j + 1) * LANES)
                x = xc_ref[rows, sl]
                xb = x.astype(BF16)
                s_prev = s_ref[:, sl]
                rhs = jnp.concatenate([xb, s_prev.astype(BF16)], axis=0)
                ys, news, decs = [], [], []
                for e in (0, 1):
                    h = 2 * j + e
                    seg = acs[:, h:h + 1] - src_t[h:h + 1, :]
                    m_h = (cb * jnp.exp2(jnp.where(causal, seg, NEG))).astype(BF16)
                    w2 = (cg * eacs[:, h:h + 1]).astype(BF16)
                    lhs = jnp.concatenate([m_h, w2], axis=1)
                    ys.append(jnp.dot(lhs, rhs, preferred_element_type=F32))
                    bw_t = (bg_t * w_t[h:h + 1, :]).astype(BF16)
                    news.append(jnp.dot(bw_t, xb, preferred_element_type=F32))
                    decs.append(jnp.broadcast_to(dec[h:h + 1, :], (n, LANES)))
                y = jnp.where(lo, ys[0], ys[1])
                s_ref[:, sl] = (s_prev * jnp.where(lo, decs[0], decs[1])
                                + jnp.where(lo, news[0], news[1]))
                y = y + dsk_ref[:, sl] * x
                u_ref[rows, sl] = y * _silu_of_half(z_ref[rows, sl].astype(F32))
            gs = slice(g * group_w, (g + 1) * group_w)
            ug = u_ref[rows, gs]
            ms = jnp.mean(ug * ug, axis=-1, keepdims=True)
            o_ref[rows, gs] = (ug * lax.rsqrt(ms + RMS_EPS) * ng_ref[:, gs]).astype(o_ref.dtype)

    xprev_ref[...] = xbc_ref[(SSD_CHUNKS_PER_STEP - 1) * q:, :]


def _ssd(z, xbc, dt_raw, conv_w, conv_b, dt_bias, a_log, d_skip, norm_g):
    q = SSM_CHUNK
    rows = SSD_CHUNKS_PER_STEP * q
    steps = SEQ // rows
    row = lambda b, c: (b * steps + c, 0)
    sel = _shift_select(q, SSM_CONV_WIDTH)
    return pl.pallas_call(
        _ssd_kernel,
        grid=(BATCH, steps),
        in_specs=[pl.BlockSpec((rows, SSM_INNER), row),
                  pl.BlockSpec((rows, SSM_CONV_CH), row),
                  pl.BlockSpec((rows, LANES), row),
                  _resident(sel.shape),
                  _resident(conv_w.shape), _resident(conv_b.shape), _resident(dt_bias.shape),
                  _resident(a_log.shape), _resident(d_skip.shape), _resident(norm_g.shape)],
        out_specs=pl.BlockSpec((rows, SSM_INNER), row),
        out_shape=jax.ShapeDtypeStruct((TOKENS, SSM_INNER), BF16),
        scratch_shapes=[pltpu.VMEM((q, SSM_CONV_CH), BF16),
                        pltpu.VMEM((rows, SSM_CONV_CH), F32),
                        pltpu.VMEM((SSM_STATE, SSM_INNER), F32),
                        pltpu.VMEM((rows, SSM_INNER), F32)],
        compiler_params=_params(("arbitrary", "arbitrary")),
        name="mamba2_ssd",
    )(z, xbc, dt_raw, sel, conv_w, conv_b, dt_bias, a_log, d_skip, norm_g)


def _pad_cols(w, n):
    return jnp.pad(w, ((0, 0), (0, n - w.shape[1])))


def _pad_rows(w, n):
    return jnp.pad(w, ((0, n - w.shape[0]), (0, 0)))


def _row(v):
    return v.reshape(1, -1)


def _even_mixer(h, w_in, b_f, w_conv, w_out):
    c_end = 3 * CONV_DIM
    a_end = c_end + 3 * FOX_DIM
    w_t = w_in.T
    w_c = w_t[:c_end].astype(BF16)
    w_qkv = w_t[c_end:a_end].astype(BF16)
    w_f = _pad_rows(w_t[a_end:], LANES)
    q_scale = (FOX_HEAD_DIM ** -0.5) * LOG2E
    conv_in, qkv, f_logit = _proj(h, [w_c, w_qkv, w_f], [BF16, BF16, F32], "even_in_proj",
                                  lead_scales=[(0, 1.0), (FOX_DIM, q_scale), (0, 1.0)])
    fp_col, fp_row = _fcum(f_logit, _pad_cols(_row(b_f), LANES))
    yb = _fox_attention(qkv, fp_col, fp_row)
    tiles = [(conv_in, CONV_DIM, 0), (conv_in, CONV_DIM, 1), (conv_in, CONV_DIM, 2), (yb, FOX_DIM, 0)]
    weights = [w_conv, w_out[:CONV_DIM].astype(BF16), w_out[CONV_DIM:].astype(BF16)]
    return _even_mix, tiles, weights, [pltpu.VMEM((HALO, CONV_DIM), F32)]


def _odd_mixer(h, w_in, conv_w, conv_b, dt_bias, a_log, d_skip, norm_g, w_out):
    x_end = SSM_INNER + SSM_CONV_CH
    w_t = w_in.T
    w_z = (0.5 * w_t[:SSM_INNER]).astype(BF16)
    w_x = w_t[SSM_INNER:x_end].astype(BF16)
    w_dt = _pad_rows(w_t[x_end:], LANES)
    z, xbc, dt_raw = _proj(h, [w_z, w_x, w_dt], [BF16, BF16, F32], "odd_in_proj")
    u = _ssd(z, xbc, dt_raw, 0.5 * conv_w, _row(0.5 * conv_b), _pad_cols(_row(dt_bias), LANES),
             _pad_cols(_row(a_log), LANES), _row(jnp.repeat(d_skip, SSM_HEAD_DIM)), _row(norm_g))
    return _odd_mix, [(u, SSM_INNER, 0)], [w_out.astype(BF16)], []


def kernel(x, p, even_w_in, even_b_f, even_conv_w, even_w_out, odd_w_in, odd_conv_w, odd_conv_b,
           odd_dt_bias, odd_a_log, odd_d_skip, odd_norm_g, odd_w_out, ln_mix_g, ln_mix_b, ffn_w_up,
           ffn_conv_w, ffn_conv_b, ffn_w_down, ln_ffn_g, ln_ffn_b, ple_w_proj, ple_w_gate,
           ple_b_gate):
    h = x.reshape(TOKENS, D_MODEL)
    rows = lambda v: v.reshape(DEPTH, 1, -1)
    gate_half = jnp.where(jnp.arange(2 * D_FF) < D_FF, 0.5, 1.0)
    stacked = (rows(ln_mix_g), rows(ln_mix_b), ffn_w_up.astype(BF16), ffn_conv_w * gate_half,
               rows(ffn_conv_b * gate_half),
               ffn_w_down.astype(BF16), rows(ln_ffn_g), rows(ln_ffn_b), ple_w_proj.astype(BF16),
               ple_w_gate.astype(BF16), rows(ple_b_gate))
    p_all = p.reshape(DEPTH * TOKENS, PLE_DIM)
    for i in range(DEPTH):
        j = i // 2
        if i % 2 == 0:
            name = "even_layer_tail"
            mix = _even_mixer(h, even_w_in[j], even_b_f[j], even_conv_w[j], even_w_out[j])
        else:
            name = "odd_layer_tail"
            mix = _odd_mixer(h, odd_w_in[j], odd_conv_w[j], odd_conv_b[j], odd_dt_bias[j],
                             odd_a_log[j], odd_d_skip[j], odd_norm_g[j], odd_w_out[j])
        h = _layer_tail(name, i, *mix, h, p_all, stacked)
    return h.reshape(BATCH, SEQ, D_MODEL)
```

```python
import functools

import jax
import jax.numpy as jnp
from jax import lax
from jax.experimental import pallas as pl
from jax.experimental.pallas import tpu as pltpu

F32 = jnp.float32
BF16 = jnp.bfloat16

D_MODEL = 1024
BATCH = 2
SEQ = 8192
DEPTH = 2
TOKENS = BATCH * SEQ

CONV_DIM = 512
CONV_WIDTH = 3
FOX_HEADS = 8
FOX_HEAD_DIM = 64
FOX_DIM = FOX_HEADS * FOX_HEAD_DIM
SSM_INNER = 2 * D_MODEL
SSM_HEAD_DIM = 64
SSM_HEADS = SSM_INNER // SSM_HEAD_DIM
SSM_GROUPS = 4
SSM_STATE = 128
SSM_CONV_WIDTH = 4
SSM_CHUNK = 128
SSM_CONV_CH = SSM_INNER + 2 * SSM_GROUPS * SSM_STATE
D_FF = 2816
FFN_CONV_WIDTH = 3
PLE_DIM = 256
LN_EPS = 1e-5
RMS_EPS = 1e-5
ALPHA = (2.0 * DEPTH) ** 0.25

LANES = 128
HALO = 8
NEG = -1e30
VMEM_LIMIT = 56 * 1024 * 1024

ROW_TILE = 512
SSD_CONV_COLS = 512
SSD_CHUNKS_PER_STEP = 4
FFN_ROW_TILE = 512
FFN_CHUNK = 256
FFN_DOWN_GROUP_ENDS = (6, 11)
ATTN_TILE = 512
LOG2E = 1.4426950408889634


def _resident(shape):
    nd = len(shape)
    return pl.BlockSpec(shape, lambda *_: (0,) * nd, pipeline_mode=pl.Buffered(1))


def _resident_layer(shape, layer):
    nd = len(shape)
    return pl.BlockSpec((None,) + tuple(shape[1:]), lambda *_: (layer,) + (0,) * (nd - 1),
                        pipeline_mode=pl.Buffered(1))


def _params(sem, flags=None):
    return pltpu.CompilerParams(dimension_semantics=sem, vmem_limit_bytes=VMEM_LIMIT, flags=flags)


def _sigmoid(x):
    return 0.5 + 0.5 * jnp.tanh(0.5 * x)


def _silu_of_half(h):
    return h + h * jnp.tanh(h)


def _silu(x):
    return _silu_of_half(0.5 * x)


def _softplus(x):
    return jnp.maximum(x, 0.0) + jnp.log1p(jnp.exp(-jnp.abs(x)))


def _log_sigmoid(x):
    return jnp.minimum(x, 0.0) - jnp.log1p(jnp.exp(-jnp.abs(x)))


def _layer_norm(r, g, b):
    mu = jnp.mean(r, axis=-1, keepdims=True)
    d = r - mu
    var = jnp.mean(d * d, axis=-1, keepdims=True)
    return d * lax.rsqrt(var + LN_EPS) * g + b


def _causal_taps_rolled(cur, prev, w):
    k_taps = w.shape[0]
    sub = lax.broadcasted_iota(jnp.int32, prev.shape, 0)
    out = w[k_taps - 1:k_taps, :] * cur
    for k in range(k_taps - 1):
        shift = k_taps - 1 - k
        rolled = pltpu.roll(cur, shift, axis=0)
        head = jnp.where(sub < shift, pltpu.roll(prev, shift, axis=0), rolled[0:HALO, :])
        out = out + w[k:k + 1, :] * jnp.concatenate([head, rolled[HALO:, :]], axis=0)
    return out


def _dot_nt(a, b_t):
    return lax.dot_general(a, b_t, (((1,), (1,)), ((), ())), preferred_element_type=F32)


def _proj_kernel(x_ref, *refs, n_out, n_cast, chunk, lead_scales, precise):
    w_refs = refs[:n_out]
    cast_in = refs[n_out:n_out + n_cast]
    o_refs = refs[n_out + n_cast:2 * n_out + n_cast]
    cast_out = refs[2 * n_out + n_cast:]
    for src_ref, dst_ref in zip(cast_in, cast_out):
        dst_ref[...] = src_ref[...].astype(BF16)
    x = x_ref[...]
    xb = x.astype(BF16)
    for w_ref, o_ref, (lead_cols, lead_scale), hi_lo in zip(w_refs, o_refs, lead_scales, precise):
        n = o_ref.shape[1]
        if hi_lo:
            x_lo = (x - xb.astype(F32)).astype(BF16)
            both = _dot_nt(xb, w_ref[...])
            cross = _dot_nt(x_lo, w_ref[0:n, :])
            o_ref[...] = (both[:, 0:n] + both[:, n:] + cross).astype(o_ref.dtype)
            continue
        for c0 in range(0, n, chunk):
            c1 = min(c0 + chunk, n)
            acc = _dot_nt(xb, w_ref[c0:c1, :])
            if c1 <= lead_cols:
                acc = acc * lead_scale
            o_ref[:, c0:c1] = acc.astype(o_ref.dtype)


def _split_bf16(w_t):
    w_hi = w_t.astype(BF16)
    w_lo = (w_t - w_hi.astype(F32)).astype(BF16)
    return jnp.concatenate([w_hi, w_lo], axis=0)


def _proj(x, ws_t, out_dtypes, name, lead_scales=None, casts=()):
    m, k = x.shape
    tm = ROW_TILE
    steps = m // tm
    chunk = 512
    if lead_scales is None:
        lead_scales = [(0, 1.0)] * len(ws_t)
    assert all(cols % chunk == 0 for cols, _ in lead_scales)
    precise = tuple(w.dtype == F32 for w in ws_t)
    widths = [w.shape[0] for w in ws_t]
    ws = [_split_bf16(w) if p else w for w, p in zip(ws_t, precise)]
    bf16_rows = 2 * HALO
    assert all(c.shape[0] % (steps * bf16_rows) == 0 for c in casts)
    slab_specs = [pl.BlockSpec((c.shape[0] // steps, c.shape[1]), lambda i: (i, 0)) for c in casts]
    return pl.pallas_call(
        functools.partial(_proj_kernel, n_out=len(ws), n_cast=len(casts), chunk=chunk,
                          lead_scales=tuple(lead_scales), precise=precise),
        grid=(steps,),
        in_specs=([pl.BlockSpec((tm, k), lambda i: (i, 0))] + [_resident(w.shape) for w in ws]
                  + slab_specs),
        out_specs=[pl.BlockSpec((tm, n), lambda i: (i, 0)) for n in widths] + slab_specs,
        out_shape=([jax.ShapeDtypeStruct((m, n), dt) for n, dt in zip(widths, out_dtypes)]
                   + [jax.ShapeDtypeStruct(c.shape, BF16) for c in casts]),
        compiler_params=_params(("arbitrary",)),
        name=name,
    )(x, *ws, *casts)


def _fcum_kernel(f_ref, b_ref, col_ref, row_ref, carry_ref, *, rows):
    j = pl.program_id(1)

    @pl.when(j == 0)
    def _():
        carry_ref[...] = jnp.zeros_like(carry_ref)

    r = lax.broadcasted_iota(jnp.int32, (LANES, LANES), 0)
    c = lax.broadcasted_iota(jnp.int32, (LANES, LANES), 1)
    tri = (r >= c).astype(F32)
    nh = FOX_HEADS
    carry = carry_ref[0:1, :]
    for blk in range(rows // LANES):
        sl = slice(blk * LANES, (blk + 1) * LANES)
        lf = _log_sigmoid(f_ref[sl, :] + b_ref[...])
        cs = jnp.dot(tri, lf, preferred_element_type=F32, precision=lax.Precision.HIGHEST)
        cum = cs + carry
        carry = cum[LANES - 1:LANES, :]
        f2 = cum * LOG2E
        hi = f2.astype(BF16).astype(F32)
        mid = (f2 - hi).astype(BF16).astype(F32)
        lo = (f2 - hi - mid).astype(BF16).astype(F32)
        packed = jnp.where(c < nh, hi,
                           jnp.where(c < 2 * nh, pltpu.roll(mid, nh, axis=1),
                                     jnp.where(c < 3 * nh, pltpu.roll(lo, 2 * nh, axis=1), 0.0)))
        col_ref[sl, :] = packed.astype(BF16)
        row_ref[0, :, sl] = jnp.concatenate([hi.T[0:nh, :], mid.T[0:nh, :], lo.T[0:nh, :],
                                             jnp.zeros((nh, LANES), F32)], axis=0)
    carry_ref[...] = jnp.broadcast_to(carry, carry_ref.shape)


def _fcum(f_logit, b_f):
    rows = ROW_TILE
    nblk = SEQ // rows
    return pl.pallas_call(
        functools.partial(_fcum_kernel, rows=rows),
        grid=(BATCH, nblk),
        in_specs=[pl.BlockSpec((rows, LANES), lambda b, j: (b * nblk + j, 0)),
                  pl.BlockSpec((1, LANES), lambda b, j: (0, 0))],
        out_specs=[pl.BlockSpec((rows, LANES), lambda b, j: (b * nblk + j, 0)),
                   pl.BlockSpec((1, 4 * FOX_HEADS, rows), lambda b, j: (b, 0, j))],
        out_shape=[jax.ShapeDtypeStruct((TOKENS, LANES), BF16),
                   jax.ShapeDtypeStruct((BATCH, 4 * FOX_HEADS, SEQ), F32)],
        scratch_shapes=[pltpu.VMEM((HALO, LANES), F32)],
        compiler_params=_params(("arbitrary", "arbitrary")),
        name="fox_forget_cumsum",
    )(f_logit, b_f)


def _fox_kernel(q_ref, k_ref, v_ref, fpc_ref, fpr_ref, o_ref, kaug_ref, vt_ref, qaug_ref,
                s0_ref, s1_ref, bm0_ref, bm1_ref, m_ref, acc_ref, *, t):
    hp = pl.program_id(1)
    qi = pl.program_id(2)
    d = FOX_HEAD_DIM
    nh = FOX_HEADS
    prep_rows = 1024

    @pl.when(qi == 0)
    def _():
        r = lax.broadcasted_iota(jnp.int32, (LANES, LANES), 0)
        c = lax.broadcasted_iota(jnp.int32, (LANES, LANES), 1)
        lane = lax.broadcasted_iota(jnp.int32, (1, LANES), 1)
        ones_lanes = jnp.where(lane < 3, 1.0, 0.0)
        sub = lax.broadcasted_iota(jnp.int32, (LANES, prep_rows), 0)
        for e in (0, 1):
            h = 2 * hp + e
            pick = jnp.where((c >= 3) & (c < 6) & (r == (c - 3) * nh + h), -1.0, 0.0).astype(BF16)
            for blk in range(SEQ // prep_rows):
                rows = slice(blk * prep_rows, (blk + 1) * prep_rows)
                aug = jnp.dot(fpc_ref[rows, :], pick, preferred_element_type=F32) + ones_lanes
                kaug_ref[e, rows, :] = aug.astype(BF16)
                v_t = v_ref[rows, :].astype(F32).T
                own = (sub < d) if e == 0 else (sub >= d)
                vt_ref[e, :, rows] = jnp.where(own, v_t, 1.0).astype(BF16)

    q_t = q_ref[...].astype(F32).T
    sub_q = lax.broadcasted_iota(jnp.int32, (LANES, t), 0)
    sub8 = lax.broadcasted_iota(jnp.int32, (HALO, t), 0)
    q_aug = []
    for e in (0, 1):
        h = 2 * hp + e
        own = (sub_q < d) if e == 0 else (sub_q >= d)
        f_hi = fpr_ref[0, pl.ds(h, 1), :]
        f_mid = fpr_ref[0, pl.ds(nh + h, 1), :]
        f_lo = fpr_ref[0, pl.ds(2 * nh + h, 1), :]
        top = jnp.where(sub8 == 0, f_hi,
                        jnp.where(sub8 == 1, f_mid,
                                  jnp.where(sub8 == 2, f_lo, jnp.where(sub8 < 6, 1.0, 0.0))))
        q_aug.append(jnp.concatenate(
            [jnp.where(own, q_t, 0.0), top, jnp.zeros((LANES - HALO, t), F32)], axis=0).astype(BF16))

    for e in (0, 1):
        qaug_ref[e] = q_aug[e]
        m_ref[e] = jnp.full((1, t), NEG, F32)
        acc_ref[e] = jnp.zeros((LANES, t), F32)

    key_idx = lax.broadcasted_iota(jnp.int32, (t, t), 0)
    qry_idx = lax.broadcasted_iota(jnp.int32, (t, t), 1)
    causal = key_idx <= qry_idx

    def score_stage(j, s_ref, bm_ref):
        start = pl.multiple_of(j * t, t)
        k2 = k_ref[pl.ds(start, t), :]
        for e in (0, 1):
            k_aug = jnp.concatenate([k2, kaug_ref[e, pl.ds(start, t), :]], axis=1)
            s_t = jnp.dot(k_aug, qaug_ref[e], preferred_element_type=F32)
            s_ref[e] = s_t
            bm_ref[e] = jnp.max(s_t, axis=0, keepdims=True)

    def softmax_stage(j, s_ref, bm_ref, masked):
        start = pl.multiple_of(j * t, t)
        for e in (0, 1):
            s_t = s_ref[e]
            if masked:
                s_t = jnp.where(causal, s_t, NEG)
                bm = jnp.max(s_t, axis=0, keepdims=True)
            else:
                bm = bm_ref[e]
            m_prev = m_ref[e]
            m_new = jnp.maximum(m_prev, bm)
            m_ref[e] = m_new
            p_t = jnp.exp2(s_t - m_new).astype(BF16)
            pv = jnp.dot(vt_ref[e, :, pl.ds(start, t)], p_t, preferred_element_type=F32)
            acc_ref[e] = jnp.exp2(m_prev - m_new) * acc_ref[e] + pv

    score_stage(qi, s1_ref, bm1_ref)
    score_stage(0, s0_ref, bm0_ref)
    softmax_stage(qi, s1_ref, bm1_ref, True)

    def two_blocks(i, _):
        j = 2 * i
        score_stage(j + 1, s1_ref, bm1_ref)
        softmax_stage(j, s0_ref, bm0_ref, False)
        score_stage(j + 2, s0_ref, bm0_ref)
        softmax_stage(j + 1, s1_ref, bm1_ref, False)
        return 0

    def pairs_per_trip(n_pairs):
        def body(i, _):
            for pair in range(n_pairs):
                two_blocks(n_pairs * i + pair, 0)
            return 0
        return body

    octs = lax.shift_right_logical(qi, 3)
    quads = lax.shift_right_logical(qi, 2)
    lax.fori_loop(0, octs, pairs_per_trip(4), 0)
    lax.fori_loop(2 * octs, quads, pairs_per_trip(2), 0)
    lax.fori_loop(2 * quads, lax.shift_right_logical(qi, 1), two_blocks, 0)

    @pl.when(lax.bitwise_and(qi, 1) == 1)
    def _():
        softmax_stage(qi - 1, s0_ref, bm0_ref, False)

    a0 = acc_ref[0]
    a1 = acc_ref[1]
    o_t = jnp.concatenate([a0[0:d, :] / a0[d:d + 1, :], a1[d:, :] / a1[0:1, :]], axis=0)
    o_ref[...] = o_t.T.astype(o_ref.dtype)


def _fox_attention(qkv, fp_col, fp_row):
    t = ATTN_TILE
    nq = SEQ // t
    pairs = FOX_HEADS // 2
    return pl.pallas_call(
        functools.partial(_fox_kernel, t=t),
        grid=(BATCH, pairs, nq),
        in_specs=[pl.BlockSpec((t, LANES), lambda b, h, i: (b * nq + i, h)),
                  pl.BlockSpec((SEQ, LANES), lambda b, h, i: (b, pairs + h)),
                  pl.BlockSpec((SEQ, LANES), lambda b, h, i: (b, 2 * pairs + h)),
                  pl.BlockSpec((SEQ, LANES), lambda b, h, i: (b, 0)),
                  pl.BlockSpec((1, 4 * FOX_HEADS, t), lambda b, h, i: (b, 0, i))],
        out_specs=pl.BlockSpec((t, LANES), lambda b, h, i: (b * nq + i, h)),
        out_shape=jax.ShapeDtypeStruct((TOKENS, FOX_DIM), BF16),
        scratch_shapes=[pltpu.VMEM((2, SEQ, LANES), BF16),
                        pltpu.VMEM((2, LANES, SEQ), BF16),
                        pltpu.VMEM((2, 2 * LANES, t), BF16),
                        pltpu.VMEM((2, t, t), F32),
                        pltpu.VMEM((2, t, t), F32),
                        pltpu.VMEM((2, 1, t), F32),
                        pltpu.VMEM((2, 1, t), F32),
                        pltpu.VMEM((2, 1, t), F32),
                        pltpu.VMEM((2, LANES, t), F32)],
        compiler_params=_params(("arbitrary", "arbitrary", "arbitrary")),
        name="fox_attention",
    )(qkv, qkv, qkv, fp_col, fp_row)


def _even_mix(first_tile, gb_ref, gc_ref, hh_ref, yb_ref, wc_ref, w_ref, chalo_ref):
    @pl.when(first_tile)
    def _():
        chalo_ref[...] = jnp.zeros_like(chalo_ref)

    u = gc_ref[...].astype(F32) * hh_ref[...].astype(F32)
    prev = chalo_ref[...]
    chalo_ref[...] = u[u.shape[0] - HALO:, :]
    ya = (gb_ref[...].astype(F32) * _causal_taps_rolled(u, prev, wc_ref[...])).astype(BF16)
    mix = jnp.dot(ya, w_ref[0:CONV_DIM, :], preferred_element_type=F32)
    return mix + jnp.dot(yb_ref[...], w_ref[CONV_DIM:, :], preferred_element_type=F32)


def _odd_mix(first_tile, u_ref, w_ref):
    return jnp.dot(u_ref[...], w_ref[...], preferred_element_type=F32)


def _tail_kernel(*refs, mix_fn, n_mix, tm, tiles_per_seq):
    mix_refs = refs[:n_mix]
    (h_ref, p_ref, g1_ref, b1_ref, wup_ref, cw_ref, cb_ref, wdn_ref, g_ref, b_ref, wproj_ref, wgate_ref,
     bgate_ref, o_ref, halo_ref, act_ref, acc_ref) = refs[n_mix:n_mix + 17]
    mix_scratch = refs[n_mix + 17:]
    i = pl.program_id(0)
    first_tile = lax.rem(i, tiles_per_seq) == 0

    @pl.when(first_tile)
    def _():
        halo_ref[...] = jnp.zeros_like(halo_ref)

    mix = mix_fn(first_tile, *mix_refs, *mix_scratch)
    x = _layer_norm(ALPHA * h_ref[...] + mix, g1_ref[...], b1_ref[...])
    xb = x.astype(BF16)
    tf = FFN_CHUNK
    n_chunks = D_FF // tf

    def up(c):
        return [jnp.dot(xb, wup_ref[:, part * D_FF + c * tf:part * D_FF + (c + 1) * tf],
                        preferred_element_type=F32) for part in (0, 1)]

    us = up(0)
    group_start = 0
    for c in range(n_chunks):
        us_next = up(c + 1) if c + 1 < n_chunks else None
        branches = []
        for part in (0, 1):
            c0 = part * D_FF + c * tf
            u = us[part]
            prev = halo_ref[:, c0:c0 + tf]
            halo_ref[:, c0:c0 + tf] = u[tm - HALO:, :]
            branches.append(_causal_taps_rolled(u, prev, cw_ref[:, c0:c0 + tf]) + cb_ref[:, c0:c0 + tf])
        us = us_next
        act_ref[:, c * tf:(c + 1) * tf] = (_silu_of_half(branches[0]) * branches[1]).astype(BF16)
        if (c + 1) in FFN_DOWN_GROUP_ENDS:
            k0, k1 = group_start * tf, (c + 1) * tf
            d = jnp.dot(act_ref[:, k0:k1], wdn_ref[k0:k1, :], preferred_element_type=F32)
            if group_start == 0:
                acc_ref[...] = d
            else:
                acc_ref[...] += d
            group_start = c + 1
    h2 = _layer_norm(ALPHA * x + acc_ref[...], g_ref[...], b_ref[...])
    gate_logit = jnp.dot(h2.astype(BF16), wgate_ref[...], preferred_element_type=F32) + bgate_ref[...]
    emb = jnp.dot(p_ref[...].astype(BF16), wproj_ref[...], preferred_element_type=F32)
    o_ref[...] = h2 + _sigmoid(gate_logit) * emb


def _layer_tail(name, layer, mix_fn, mix_tiles, mix_weights, mix_scratch, h, p_all, stacked):
    tm = FFN_ROW_TILE
    tiles_per_layer = TOKENS // tm
    tile_specs = [pl.BlockSpec((tm, width), functools.partial(lambda i, col: (i, col), col=col))
                  for _, width, col in mix_tiles]
    mix_arrays = [a for a, _, _ in mix_tiles] + list(mix_weights)
    rest = [h, p_all, *stacked]
    return pl.pallas_call(
        functools.partial(_tail_kernel, mix_fn=mix_fn, n_mix=len(mix_arrays), tm=tm,
                          tiles_per_seq=SEQ // tm),
        grid=(TOKENS // tm,),
        in_specs=(tile_specs + [_resident(w.shape) for w in mix_weights]
                  + [pl.BlockSpec((tm, D_MODEL), lambda i: (i, 0)),
                     pl.BlockSpec((tm, PLE_DIM), lambda i: (layer * tiles_per_layer + i, 0))]
                  + [_resident_layer(a.shape, layer) for a in stacked]),
        out_specs=pl.BlockSpec((tm, D_MODEL), lambda i: (i, 0)),
        out_shape=jax.ShapeDtypeStruct((TOKENS, D_MODEL), F32),
        scratch_shapes=[pltpu.VMEM((HALO, 2 * D_FF), F32),
                        pltpu.VMEM((tm, D_FF), BF16),
                        pltpu.VMEM((tm, D_MODEL), F32)] + list(mix_scratch),
        compiler_params=_params(("arbitrary",)),
        name=name,
    )(*mix_arrays, *rest)


def _shift_select(q, taps):
    r = jnp.arange((taps - 1) * q)[:, None]
    c = jnp.arange(2 * q)[None, :]
    return (c == q + r % q - (r // q + 1)).astype(BF16)


def _ssd_kernel(z_ref, xbc_ref, dt_ref, sel_ref, cw_ref, cb_ref, dtb_ref, alog_ref, dsk_ref, ng_ref,
                o_ref, xprev_ref, xc_ref, s_ref, u_ref):
    q = SSM_CHUNK
    n = SSM_STATE

    @pl.when(pl.program_id(1) == 0)
    def _():
        xprev_ref[...] = jnp.zeros_like(xprev_ref)
        s_ref[...] = jnp.zeros_like(s_ref)

    cw = cw_ref[...]
    taps = SSM_CONV_WIDTH
    sel = sel_ref[...]
    row = lax.broadcasted_iota(jnp.int32, (q, q), 0)
    col = lax.broadcasted_iota(jnp.int32, (q, q), 1)
    causal = row >= col
    lo = lax.broadcasted_iota(jnp.int32, (q, LANES), 1) < SSM_HEAD_DIM
    heads_per_group = SSM_HEADS // SSM_GROUPS
    group_w = heads_per_group * SSM_HEAD_DIM

    for sub in range(SSD_CHUNKS_PER_STEP):
        rows = slice(sub * q, (sub + 1) * q)
        prev_rows = slice((sub - 1) * q, sub * q)

        for c0 in range(0, SSM_CONV_CH, SSD_CONV_COLS):
            cols = slice(c0, c0 + SSD_CONV_COLS)
            x_cur = xbc_ref[rows, cols]
            x_prev = xprev_ref[:, cols] if sub == 0 else xbc_ref[prev_rows, cols]
            shifted = jnp.dot(sel, jnp.concatenate([x_prev, x_cur], axis=0),
                              preferred_element_type=F32)
            conv = cw[taps - 1:taps, cols] * x_cur.astype(F32) + cb_ref[:, cols]
            for k in range(1, taps):
                conv = conv + cw[taps - 1 - k:taps - k, cols] * shifted[(k - 1) * q:k * q, :]
            xc_ref[rows, cols] = _silu_of_half(conv)

        dt = _softplus(dt_ref[rows, :] + dtb_ref[...])
        a = dt * (-LOG2E * jnp.exp(alog_ref[...]))
        acs = jnp.dot(causal.astype(F32), a, preferred_element_type=F32,
                      precision=lax.Precision.HIGHEST)
        dt_t = dt.T
        acs_t = acs.T
        tot = acs_t[:, q - 1:q]
        w_t = dt_t * jnp.exp2(tot - acs_t)
        src_t = acs_t - jnp.log2(dt_t)
        eacs = jnp.exp2(acs)
        dec = jnp.exp2(tot)

        for g in range(SSM_GROUPS):
            b0 = SSM_INNER + g * n
            c0 = SSM_INNER + SSM_GROUPS * n + g * n
            bg = xc_ref[rows, b0:b0 + n]
            cg = xc_ref[rows, c0:c0 + n]
            cb = lax.dot_general(cg.astype(BF16), bg.astype(BF16), (((1,), (1,)), ((), ())),
                                 preferred_element_type=F32)
            bg_t = bg.T
            for pr in range(heads_per_group // 2):
                j = g * (heads_per_group // 2) + pr
                sl = slice(j * LANES, (j + 1) * LANES)
                x = xc_ref[rows, sl]
                xb = x.astype(BF16)
                s_prev = s_ref[:, sl]
                rhs = jnp.concatenate([xb, s_prev.astype(BF16)], axis=0)
                ys, news, decs = [], [], []
                for e in (0, 1):
                    h = 2 * j + e
                    seg = acs[:, h:h + 1] - src_t[h:h + 1, :]
                    m_h = (cb * jnp.exp2(jnp.where(causal, seg, NEG))).astype(BF16)
                    w2 = (cg * eacs[:, h:h + 1]).astype(BF16)
                    lhs = jnp.concatenate([m_h, w2], axis=1)
                    ys.append(jnp.dot(lhs, rhs, preferred_element_type=F32))
                    bw_t = (bg_t * w_t[h:h + 1, :]).astype(BF16)
                    news.append(jnp.dot(bw_t, xb, preferred_element_type=F32))
                    decs.append(jnp.broadcast_to(dec[h:h + 1, :], (n, LANES)))
                y = jnp.where(lo, ys[0], ys[1])
                s_ref[:, sl] = (s_prev * jnp.where(lo, decs[0], decs[1])
                                + jnp.where(lo, news[0], news[1]))
                y = y + dsk_ref[:, sl] * x
                u_ref[rows, sl] = y * _silu_of_half(z_ref[rows, sl].astype(F32))
            gs = slice(g * group_w, (g + 1) * group_w)
            ug = u_ref[rows, gs]
            ms = jnp.mean(ug * ug, axis=-1, keepdims=True)
            o_ref[rows, gs] = (ug * lax.rsqrt(ms + RMS_EPS) * ng_ref[:, gs]).astype(o_ref.dtype)

    xprev_ref[...] = xbc_ref[(SSD_CHUNKS_PER_STEP - 1) * q:, :]


def _ssd(z, xbc, dt_raw, conv_w, conv_b, dt_bias, a_log, d_skip, norm_g):
    q = SSM_CHUNK
    rows = SSD_CHUNKS_PER_STEP * q
    steps = SEQ // rows
    row = lambda b, c: (b * steps + c, 0)
    sel = _shift_select(q, SSM_CONV_WIDTH)
    return pl.pallas_call(
        _ssd_kernel,
        grid=(BATCH, steps),
        in_specs=[pl.BlockSpec((rows, SSM_INNER), row),
                  pl.BlockSpec((rows, SSM_CONV_CH), row),
                  pl.BlockSpec((rows, LANES), row),
                  _resident(sel.shape),
                  _resident(conv_w.shape), _resident(conv_b.shape), _resident(dt_bias.shape),
                  _resident(a_log.shape), _resident(d_skip.shape), _resident(norm_g.shape)],
        out_specs=pl.BlockSpec((rows, SSM_INNER), row),
        out_shape=jax.ShapeDtypeStruct((TOKENS, SSM_INNER), BF16),
        scratch_shapes=[pltpu.VMEM((q, SSM_CONV_CH), BF16),
                        pltpu.VMEM((rows, SSM_CONV_CH), F32),
                        pltpu.VMEM((SSM_STATE, SSM_INNER), F32),
                        pltpu.VMEM((rows, SSM_INNER), F32)],
        compiler_params=_params(("arbitrary", "arbitrary")),
        name="mamba2_ssd",
    )(z, xbc, dt_raw, sel, conv_w, conv_b, dt_bias, a_log, d_skip, norm_g)


def _pad_cols(w, n):
    return jnp.pad(w, ((0, 0), (0, n - w.shape[1])))


def _pad_rows(w, n):
    return jnp.pad(w, ((0, n - w.shape[0]), (0, 0)))


def _row(v):
    return v.reshape(1, -1)


def _even_in_proj(h, w_in, casts):
    c_end = 3 * CONV_DIM
    a_end = c_end + 3 * FOX_DIM
    w_t = w_in.T
    w_c = w_t[:c_end].astype(BF16)
    w_qkv = w_t[c_end:a_end].astype(BF16)
    w_f = _pad_rows(w_t[a_end:], LANES)
    q_scale = (FOX_HEAD_DIM ** -0.5) * LOG2E
    return _proj(h, [w_c, w_qkv, w_f], [BF16, BF16, F32], "even_in_proj",
                 lead_scales=[(0, 1.0), (FOX_DIM, q_scale), (0, 1.0)], casts=casts)


def _even_mixer(conv_in, qkv, f_logit, b_f, w_conv, w_out_bf16):
    fp_col, fp_row = _fcum(f_logit, _pad_cols(_row(b_f), LANES))
    yb = _fox_attention(qkv, fp_col, fp_row)
    tiles = [(conv_in, CONV_DIM, 0), (conv_in, CONV_DIM, 1), (conv_in, CONV_DIM, 2), (yb, FOX_DIM, 0)]
    return _even_mix, tiles, [w_conv, w_out_bf16], [pltpu.VMEM((HALO, CONV_DIM), F32)]


def _odd_mixer(h, w_in, conv_w, conv_b, dt_bias, a_log, d_skip, norm_g, w_out_bf16):
    x_end = SSM_INNER + SSM_CONV_CH
    w_t = w_in.T
    w_z = (0.5 * w_t[:SSM_INNER]).astype(BF16)
    w_x = w_t[SSM_INNER:x_end].astype(BF16)
    w_dt = _pad_rows(w_t[x_end:], LANES)
    z, xbc, dt_raw = _proj(h, [w_z, w_x, w_dt], [BF16, BF16, F32], "odd_in_proj")
    u = _ssd(z, xbc, dt_raw, 0.5 * conv_w, _row(0.5 * conv_b), _pad_cols(_row(dt_bias), LANES),
             _pad_cols(_row(a_log), LANES), _row(jnp.repeat(d_skip, SSM_HEAD_DIM)), _row(norm_g))
    return _odd_mix, [(u, SSM_INNER, 0)], [w_out_bf16], []


def kernel(x, p, even_w_in, even_b_f, even_conv_w, even_w_out, odd_w_in, odd_conv_w, odd_conv_b,
           odd_dt_bias, odd_a_log, odd_d_skip, odd_norm_g, odd_w_out, ln_mix_g, ln_mix_b, ffn_w_up,
           ffn_conv_w, ffn_conv_b, ffn_w_down, ln_ffn_g, ln_ffn_b, ple_w_proj, ple_w_gate,
           ple_b_gate):
    h = x.reshape(TOKENS, D_MODEL)
    rows = lambda v: v.reshape(DEPTH, 1, -1)
    flat = lambda w: w.reshape(-1, w.shape[-1])
    weights = (ffn_w_up, ffn_w_down, ple_w_proj, ple_w_gate, even_w_out, odd_w_out)
    conv_in, qkv, f_logit, *converted = _even_in_proj(h, even_w_in[0], [flat(w) for w in weights])
    w_up, w_down, w_proj, w_gate, w_out_even, w_out_odd = [
        c.reshape(w.shape) for c, w in zip(converted, weights)]
    gate_half = jnp.where(jnp.arange(2 * D_FF) < D_FF, 0.5, 1.0)
    stacked = (rows(ln_mix_g), rows(ln_mix_b), w_up, ffn_conv_w * gate_half,
               rows(ffn_conv_b * gate_half), w_down, rows(ln_ffn_g), rows(ln_ffn_b), w_proj, w_gate,
               rows(ple_b_gate))
    p_all = p.reshape(DEPTH * TOKENS, PLE_DIM)
    for i in range(DEPTH):
        j = i // 2
        if i % 2 == 0:
            name = "even_layer_tail"
            if i > 0:
                conv_in, qkv, f_logit = _even_in_proj(h, even_w_in[j], [])
            mix = _even_mixer(conv_in, qkv, f_logit, even_b_f[j], even_conv_w[j], w_out_even[j])
        else:
            name = "odd_layer_tail"
            mix = _odd_mixer(h, odd_w_in[j], odd_conv_w[j], odd_conv_b[j], odd_dt_bias[j],
                             odd_a_log[j], odd_d_skip[j], odd_norm_g[j], w_out_odd[j])
        h = _layer_tail(name, i, *mix, h, p_all, stacked)
    return h.reshape(BATCH, SEQ, D_MODEL)
```

```python
import functools

import jax
import jax.numpy as jnp
from jax import lax
from jax.experimental import pallas as pl
from jax.experimental.pallas import tpu as pltpu

F32 = jnp.float32
BF16 = jnp.bfloat16

D_MODEL = 1024
BATCH = 2
SEQ = 8192
DEPTH = 2
TOKENS = BATCH * SEQ

CONV_DIM = 512
CONV_WIDTH = 3
FOX_HEADS = 8
FOX_HEAD_DIM = 64
FOX_DIM = FOX_HEADS * FOX_HEAD_DIM
SSM_INNER = 2 * D_MODEL
SSM_HEAD_DIM = 64
SSM_HEADS = SSM_INNER // SSM_HEAD_DIM
SSM_GROUPS = 4
SSM_STATE = 128
SSM_CONV_WIDTH = 4
SSM_CHUNK = 128
SSM_CONV_CH = SSM_INNER + 2 * SSM_GROUPS * SSM_STATE
D_FF = 2816
FFN_CONV_WIDTH = 3
PLE_DIM = 256
LN_EPS = 1e-5
RMS_EPS = 1e-5
ALPHA = (2.0 * DEPTH) ** 0.25

LANES = 128
HALO = 8
NEG = -1e30
VMEM_LIMIT = 56 * 1024 * 1024

ROW_TILE = 512
SSD_CONV_COLS = 512
SSD_CHUNKS_PER_STEP = 4
FFN_ROW_TILE = 512
FFN_CHUNK = 256
FFN_DOWN_GROUP_ENDS = (6, 11)
ATTN_TILE = 512
LOG2E = 1.4426950408889634


def _resident(shape):
    nd = len(shape)
    return pl.BlockSpec(shape, lambda *_: (0,) * nd, pipeline_mode=pl.Buffered(1))


def _resident_layer(shape, layer):
    nd = len(shape)
    return pl.BlockSpec((None,) + tuple(shape[1:]), lambda *_: (layer,) + (0,) * (nd - 1),
                        pipeline_mode=pl.Buffered(1))


def _params(sem, flags=None):
    return pltpu.CompilerParams(dimension_semantics=sem, vmem_limit_bytes=VMEM_LIMIT, flags=flags)


def _sigmoid(x):
    return 0.5 + 0.5 * jnp.tanh(0.5 * x)


def _silu_of_half(h):
    return h + h * jnp.tanh(h)


def _silu(x):
    return _silu_of_half(0.5 * x)


def _softplus(x):
    return jnp.maximum(x, 0.0) + jnp.log1p(jnp.exp(-jnp.abs(x)))


def _log_sigmoid(x):
    return jnp.minimum(x, 0.0) - jnp.log1p(jnp.exp(-jnp.abs(x)))


def _layer_norm(r, g, b):
    mu = jnp.mean(r, axis=-1, keepdims=True)
    d = r - mu
    var = jnp.mean(d * d, axis=-1, keepdims=True)
    return d * lax.rsqrt(var + LN_EPS) * g + b


def _causal_taps_rolled(cur, prev, w):
    k_taps = w.shape[0]
    sub = lax.broadcasted_iota(jnp.int32, prev.shape, 0)
    out = w[k_taps - 1:k_taps, :] * cur
    for k in range(k_taps - 1):
        shift = k_taps - 1 - k
        rolled = pltpu.roll(cur, shift, axis=0)
        head = jnp.where(sub < shift, pltpu.roll(prev, shift, axis=0), rolled[0:HALO, :])
        out = out + w[k:k + 1, :] * jnp.concatenate([head, rolled[HALO:, :]], axis=0)
    return out


def _dot_nt(a, b_t):
    return lax.dot_general(a, b_t, (((1,), (1,)), ((), ())), preferred_element_type=F32)


def _proj_kernel(x_ref, *refs, n_out, n_cast, chunk, lead_scales, precise, row0s):
    w_refs = refs[:n_out]
    cast_in = refs[n_out:n_out + n_cast]
    o_refs = refs[n_out + n_cast:2 * n_out + n_cast]
    cast_out = refs[2 * n_out + n_cast:]
    for src_ref, dst_ref in zip(cast_in, cast_out):
        dst_ref[...] = src_ref[...].astype(BF16)
    x = x_ref[...]
    xb = x.astype(BF16)
    for w_ref, o_ref, (lead_cols, lead_scale), hi_lo, row0 in zip(w_refs, o_refs, lead_scales, precise,
                                                                   row0s):
        n = o_ref.shape[1]
        if hi_lo:
            x_lo = (x - xb.astype(F32)).astype(BF16)
            both = _dot_nt(xb, w_ref[...])
            cross = _dot_nt(x_lo, w_ref[0:n, :])
            o_ref[...] = (both[:, 0:n] + both[:, n:] + cross).astype(o_ref.dtype)
            continue
        for c0 in range(0, n, chunk):
            c1 = min(c0 + chunk, n)
            acc = _dot_nt(xb, w_ref[row0 + c0:row0 + c1, :])
            if c1 <= lead_cols:
                acc = acc * lead_scale
            o_ref[:, c0:c1] = acc.astype(o_ref.dtype)


def _split_bf16(w_t):
    w_hi = w_t.astype(BF16)
    w_lo = (w_t - w_hi.astype(F32)).astype(BF16)
    return jnp.concatenate([w_hi, w_lo], axis=0)


def _proj(x, ws_t, out_dtypes, name, lead_scales=None, casts=(), views=None):
    m, k = x.shape
    tm = ROW_TILE
    steps = m // tm
    chunk = 512
    if lead_scales is None:
        lead_scales = [(0, 1.0)] * len(ws_t)
    assert all(cols % chunk == 0 for cols, _ in lead_scales)
    precise = tuple(w.dtype == F32 for w in ws_t)
    if views is None:
        views = [(0, w.shape[0]) for w in ws_t]
    widths = [rows for _, rows in views]
    ws = [_split_bf16(w) if p else w for w, p in zip(ws_t, precise)]
    bf16_rows = 2 * HALO
    cast_rows = [c.shape[0] // (steps * bf16_rows) * (steps * bf16_rows) for c in casts]
    slab_specs = [pl.BlockSpec((r // steps, c.shape[1]), lambda i: (i, 0))
                  for c, r in zip(casts, cast_rows)]
    return pl.pallas_call(
        functools.partial(_proj_kernel, n_out=len(ws), n_cast=len(casts), chunk=chunk,
                          lead_scales=tuple(lead_scales), precise=precise,
                          row0s=tuple(r0 for r0, _ in views)),
        grid=(steps,),
        in_specs=([pl.BlockSpec((tm, k), lambda i: (i, 0))] + [_resident(w.shape) for w in ws]
                  + slab_specs),
        out_specs=[pl.BlockSpec((tm, n), lambda i: (i, 0)) for n in widths] + slab_specs,
        out_shape=([jax.ShapeDtypeStruct((m, n), dt) for n, dt in zip(widths, out_dtypes)]
                   + [jax.ShapeDtypeStruct((r, c.shape[1]), BF16) for c, r in zip(casts, cast_rows)]),
        compiler_params=_params(("arbitrary",)),
        name=name,
    )(x, *ws, *casts)


def _fcum_kernel(f_ref, b_ref, col_ref, row_ref, carry_ref, *, rows):
    j = pl.program_id(1)

    @pl.when(j == 0)
    def _():
        carry_ref[...] = jnp.zeros_like(carry_ref)

    r = lax.broadcasted_iota(jnp.int32, (LANES, LANES), 0)
    c = lax.broadcasted_iota(jnp.int32, (LANES, LANES), 1)
    tri = (r >= c).astype(F32)
    nh = FOX_HEADS
    carry = carry_ref[0:1, :]
    for blk in range(rows // LANES):
        sl = slice(blk * LANES, (blk + 1) * LANES)
        lf = _log_sigmoid(f_ref[sl, :] + b_ref[...])
        cs = jnp.dot(tri, lf, preferred_element_type=F32, precision=lax.Precision.HIGHEST)
        cum = cs + carry
        carry = cum[LANES - 1:LANES, :]
        f2 = cum * LOG2E
        hi = f2.astype(BF16).astype(F32)
        mid = (f2 - hi).astype(BF16).astype(F32)
        lo = (f2 - hi - mid).astype(BF16).astype(F32)
        packed = jnp.where(c < nh, hi,
                           jnp.where(c < 2 * nh, pltpu.roll(mid, nh, axis=1),
                                     jnp.where(c < 3 * nh, pltpu.roll(lo, 2 * nh, axis=1), 0.0)))
        col_ref[sl, :] = packed.astype(BF16)
        row_ref[0, :, sl] = jnp.concatenate([hi.T[0:nh, :], mid.T[0:nh, :], lo.T[0:nh, :],
                                             jnp.zeros((nh, LANES), F32)], axis=0)
    carry_ref[...] = jnp.broadcast_to(carry, carry_ref.shape)


def _fcum(f_logit, b_f):
    rows = ROW_TILE
    nblk = SEQ // rows
    return pl.pallas_call(
        functools.partial(_fcum_kernel, rows=rows),
        grid=(BATCH, nblk),
        in_specs=[pl.BlockSpec((rows, LANES), lambda b, j: (b * nblk + j, 0)),
                  pl.BlockSpec((1, LANES), lambda b, j: (0, 0))],
        out_specs=[pl.BlockSpec((rows, LANES), lambda b, j: (b * nblk + j, 0)),
                   pl.BlockSpec((1, 4 * FOX_HEADS, rows), lambda b, j: (b, 0, j))],
        out_shape=[jax.ShapeDtypeStruct((TOKENS, LANES), BF16),
                   jax.ShapeDtypeStruct((BATCH, 4 * FOX_HEADS, SEQ), F32)],
        scratch_shapes=[pltpu.VMEM((HALO, LANES), F32)],
        compiler_params=_params(("arbitrary", "arbitrary")),
        name="fox_forget_cumsum",
    )(f_logit, b_f)


def _fox_kernel(q_ref, k_ref, v_ref, fpc_ref, fpr_ref, o_ref, kaug_ref, vt_ref, qaug_ref,
                s0_ref, s1_ref, bm0_ref, bm1_ref, m_ref, acc_ref, *, t):
    hp = pl.program_id(1)
    qi = pl.program_id(2)
    d = FOX_HEAD_DIM
    nh = FOX_HEADS
    prep_rows = 1024

    @pl.when(qi == 0)
    def _():
        r = lax.broadcasted_iota(jnp.int32, (LANES, LANES), 0)
        c = lax.broadcasted_iota(jnp.int32, (LANES, LANES), 1)
        lane = lax.broadcasted_iota(jnp.int32, (1, LANES), 1)
        ones_lanes = jnp.where(lane < 3, 1.0, 0.0)
        sub = lax.broadcasted_iota(jnp.int32, (LANES, prep_rows), 0)
        for e in (0, 1):
            h = 2 * hp + e
            pick = jnp.where((c >= 3) & (c < 6) & (r == (c - 3) * nh + h), -1.0, 0.0).astype(BF16)
            for blk in range(SEQ // prep_rows):
                rows = slice(blk * prep_rows, (blk + 1) * prep_rows)
                aug = jnp.dot(fpc_ref[rows, :], pick, preferred_element_type=F32) + ones_lanes
                kaug_ref[e, rows, :] = aug.astype(BF16)
                v_t = v_ref[rows, :].astype(F32).T
                own = (sub < d) if e == 0 else (sub >= d)
                vt_ref[e, :, rows] = jnp.where(own, v_t, 1.0).astype(BF16)

    q_t = q_ref[...].astype(F32).T
    sub_q = lax.broadcasted_iota(jnp.int32, (LANES, t), 0)
    sub8 = lax.broadcasted_iota(jnp.int32, (HALO, t), 0)
    q_aug = []
    for e in (0, 1):
        h = 2 * hp + e
        own = (sub_q < d) if e == 0 else (sub_q >= d)
        f_hi = fpr_ref[0, pl.ds(h, 1), :]
        f_mid = fpr_ref[0, pl.ds(nh + h, 1), :]
        f_lo = fpr_ref[0, pl.ds(2 * nh + h, 1), :]
        top = jnp.where(sub8 == 0, f_hi,
                        jnp.where(sub8 == 1, f_mid,
                                  jnp.where(sub8 == 2, f_lo, jnp.where(sub8 < 6, 1.0, 0.0))))
        q_aug.append(jnp.concatenate(
            [jnp.where(own, q_t, 0.0), top, jnp.zeros((LANES - HALO, t), F32)], axis=0).astype(BF16))

    for e in (0, 1):
        qaug_ref[e] = q_aug[e]
        m_ref[e] = jnp.full((1, t), NEG, F32)
        acc_ref[e] = jnp.zeros((LANES, t), F32)

    key_idx = lax.broadcasted_iota(jnp.int32, (t, t), 0)
    qry_idx = lax.broadcasted_iota(jnp.int32, (t, t), 1)
    causal = key_idx <= qry_idx

    def score_stage(j, s_ref, bm_ref):
        start = pl.multiple_of(j * t, t)
        k2 = k_ref[pl.ds(start, t), :]
        for e in (0, 1):
            k_aug = jnp.concatenate([k2, kaug_ref[e, pl.ds(start, t), :]], axis=1)
            s_t = jnp.dot(k_aug, qaug_ref[e], preferred_element_type=F32)
            s_ref[e] = s_t
            bm_ref[e] = jnp.max(s_t, axis=0, keepdims=True)

    def softmax_stage(j, s_ref, bm_ref, masked):
        start = pl.multiple_of(j * t, t)
        for e in (0, 1):
            s_t = s_ref[e]
            if masked:
                s_t = jnp.where(causal, s_t, NEG)
                bm = jnp.max(s_t, axis=0, keepdims=True)
            else:
                bm = bm_ref[e]
            m_prev = m_ref[e]
            m_new = jnp.maximum(m_prev, bm)
            m_ref[e] = m_new
            p_t = jnp.exp2(s_t - m_new).astype(BF16)
            pv = jnp.dot(vt_ref[e, :, pl.ds(start, t)], p_t, preferred_element_type=F32)
            acc_ref[e] = jnp.exp2(m_prev - m_new) * acc_ref[e] + pv

    score_stage(qi, s1_ref, bm1_ref)
    score_stage(0, s0_ref, bm0_ref)
    softmax_stage(qi, s1_ref, bm1_ref, True)

    def two_blocks(i, _):
        j = 2 * i
        score_stage(j + 1, s1_ref, bm1_ref)
        softmax_stage(j, s0_ref, bm0_ref, False)
        score_stage(j + 2, s0_ref, bm0_ref)
        softmax_stage(j + 1, s1_ref, bm1_ref, False)
        return 0

    def pairs_per_trip(n_pairs):
        def body(i, _):
            for pair in range(n_pairs):
                two_blocks(n_pairs * i + pair, 0)
            return 0
        return body

    octs = lax.shift_right_logical(qi, 3)
    quads = lax.shift_right_logical(qi, 2)
    lax.fori_loop(0, octs, pairs_per_trip(4), 0)
    lax.fori_loop(2 * octs, quads, pairs_per_trip(2), 0)
    lax.fori_loop(2 * quads, lax.shift_right_logical(qi, 1), two_blocks, 0)

    @pl.when(lax.bitwise_and(qi, 1) == 1)
    def _():
        softmax_stage(qi - 1, s0_ref, bm0_ref, False)

    a0 = acc_ref[0]
    a1 = acc_ref[1]
    o_t = jnp.concatenate([a0[0:d, :] / a0[d:d + 1, :], a1[d:, :] / a1[0:1, :]], axis=0)
    o_ref[...] = o_t.T.astype(o_ref.dtype)


def _fox_attention(qkv, fp_col, fp_row):
    t = ATTN_TILE
    nq = SEQ // t
    pairs = FOX_HEADS // 2
    return pl.pallas_call(
        functools.partial(_fox_kernel, t=t),
        grid=(BATCH, pairs, nq),
        in_specs=[pl.BlockSpec((t, LANES), lambda b, h, i: (b * nq + i, h)),
                  pl.BlockSpec((SEQ, LANES), lambda b, h, i: (b, pairs + h)),
                  pl.BlockSpec((SEQ, LANES), lambda b, h, i: (b, 2 * pairs + h)),
                  pl.BlockSpec((SEQ, LANES), lambda b, h, i: (b, 0)),
                  pl.BlockSpec((1, 4 * FOX_HEADS, t), lambda b, h, i: (b, 0, i))],
        out_specs=pl.BlockSpec((t, LANES), lambda b, h, i: (b * nq + i, h)),
        out_shape=jax.ShapeDtypeStruct((TOKENS, FOX_DIM), BF16),
        scratch_shapes=[pltpu.VMEM((2, SEQ, LANES), BF16),
                        pltpu.VMEM((2, LANES, SEQ), BF16),
                        pltpu.VMEM((2, 2 * LANES, t), BF16),
                        pltpu.VMEM((2, t, t), F32),
                        pltpu.VMEM((2, t, t), F32),
                        pltpu.VMEM((2, 1, t), F32),
                        pltpu.VMEM((2, 1, t), F32),
                        pltpu.VMEM((2, 1, t), F32),
                        pltpu.VMEM((2, LANES, t), F32)],
        compiler_params=_params(("arbitrary", "arbitrary", "arbitrary")),
        name="fox_attention",
    )(qkv, qkv, qkv, fp_col, fp_row)


def _even_mix(first_tile, gb_ref, gc_ref, hh_ref, yb_ref, wc_ref, w_ref, chalo_ref):
    @pl.when(first_tile)
    def _():
        chalo_ref[...] = jnp.zeros_like(chalo_ref)

    u = gc_ref[...].astype(F32) * hh_ref[...].astype(F32)
    prev = chalo_ref[...]
    chalo_ref[...] = u[u.shape[0] - HALO:, :]
    ya = (gb_ref[...].astype(F32) * _causal_taps_rolled(u, prev, wc_ref[...])).astype(BF16)
    mix = jnp.dot(ya, w_ref[0:CONV_DIM, :], preferred_element_type=F32)
    return mix + jnp.dot(yb_ref[...], w_ref[CONV_DIM:, :], preferred_element_type=F32)


def _odd_mix(first_tile, u_ref, w_ref):
    return jnp.dot(u_ref[...], w_ref[...], preferred_element_type=F32)


def _tail_kernel(*refs, mix_fn, n_mix, tm, tiles_per_seq):
    mix_refs = refs[:n_mix]
    (h_ref, p_ref, g1_ref, b1_ref, wup_ref, cw_ref, cb_ref, wdn_ref, g_ref, b_ref, wproj_ref, wgate_ref,
     bgate_ref, o_ref, halo_ref, act_ref, acc_ref) = refs[n_mix:n_mix + 17]
    mix_scratch = refs[n_mix + 17:]
    i = pl.program_id(0)
    first_tile = lax.rem(i, tiles_per_seq) == 0

    @pl.when(first_tile)
    def _():
        halo_ref[...] = jnp.zeros_like(halo_ref)

    mix = mix_fn(first_tile, *mix_refs, *mix_scratch)
    x = _layer_norm(ALPHA * h_ref[...] + mix, g1_ref[...], b1_ref[...])
    xb = x.astype(BF16)
    tf = FFN_CHUNK
    n_chunks = D_FF // tf

    def up(c):
        return [jnp.dot(xb, wup_ref[:, part * D_FF + c * tf:part * D_FF + (c + 1) * tf],
                        preferred_element_type=F32) for part in (0, 1)]

    us = up(0)
    group_start = 0
    for c in range(n_chunks):
        us_next = up(c + 1) if c + 1 < n_chunks else None
        branches = []
        for part in (0, 1):
            c0 = part * D_FF + c * tf
            u = us[part]
            prev = halo_ref[:, c0:c0 + tf]
            halo_ref[:, c0:c0 + tf] = u[tm - HALO:, :]
            branches.append(_causal_taps_rolled(u, prev, cw_ref[:, c0:c0 + tf]) + cb_ref[:, c0:c0 + tf])
        us = us_next
        act_ref[:, c * tf:(c + 1) * tf] = (_silu_of_half(branches[0]) * branches[1]).astype(BF16)
        if (c + 1) in FFN_DOWN_GROUP_ENDS:
            k0, k1 = group_start * tf, (c + 1) * tf
            d = jnp.dot(act_ref[:, k0:k1], wdn_ref[k0:k1, :], preferred_element_type=F32)
            if group_start == 0:
                acc_ref[...] = d
            else:
                acc_ref[...] += d
            group_start = c + 1
    h2 = _layer_norm(ALPHA * x + acc_ref[...], g_ref[...], b_ref[...])
    gate_logit = jnp.dot(h2.astype(BF16), wgate_ref[...], preferred_element_type=F32) + bgate_ref[...]
    emb = jnp.dot(p_ref[...].astype(BF16), wproj_ref[...], preferred_element_type=F32)
    o_ref[...] = h2 + _sigmoid(gate_logit) * emb


def _layer_tail(name, layer, mix_fn, mix_tiles, mix_weights, mix_scratch, h, p_all, stacked):
    tm = FFN_ROW_TILE
    tiles_per_layer = TOKENS // tm
    tile_specs = [pl.BlockSpec((tm, width), functools.partial(lambda i, col: (i, col), col=col))
                  for _, width, col in mix_tiles]
    mix_arrays = [a for a, _, _ in mix_tiles] + list(mix_weights)
    rest = [h, p_all, *stacked]
    return pl.pallas_call(
        functools.partial(_tail_kernel, mix_fn=mix_fn, n_mix=len(mix_arrays), tm=tm,
                          tiles_per_seq=SEQ // tm),
        grid=(TOKENS // tm,),
        in_specs=(tile_specs + [_resident(w.shape) for w in mix_weights]
                  + [pl.BlockSpec((tm, D_MODEL), lambda i: (i, 0)),
                     pl.BlockSpec((tm, PLE_DIM), lambda i: (layer * tiles_per_layer + i, 0))]
                  + [_resident_layer(a.shape, layer) for a in stacked]),
        out_specs=pl.BlockSpec((tm, D_MODEL), lambda i: (i, 0)),
        out_shape=jax.ShapeDtypeStruct((TOKENS, D_MODEL), F32),
        scratch_shapes=[pltpu.VMEM((HALO, 2 * D_FF), F32),
                        pltpu.VMEM((tm, D_FF), BF16),
                        pltpu.VMEM((tm, D_MODEL), F32)] + list(mix_scratch),
        compiler_params=_params(("arbitrary",)),
        name=name,
    )(*mix_arrays, *rest)


def _shift_select(q, taps):
    r = jnp.arange((taps - 1) * q)[:, None]
    c = jnp.arange(2 * q)[None, :]
    return (c == q + r % q - (r // q + 1)).astype(BF16)


def _ssd_kernel(z_ref, xbc_ref, dt_ref, sel_ref, cw_ref, cb_ref, dtb_ref, alog_ref, dsk_ref, ng_ref,
                o_ref, xprev_ref, xc_ref, s_ref, u_ref):
    q = SSM_CHUNK
    n = SSM_STATE

    @pl.when(pl.program_id(1) == 0)
    def _():
        xprev_ref[...] = jnp.zeros_like(xprev_ref)
        s_ref[...] = jnp.zeros_like(s_ref)

    cw = cw_ref[...]
    taps = SSM_CONV_WIDTH
    sel = sel_ref[...]
    row = lax.broadcasted_iota(jnp.int32, (q, q), 0)
    col = lax.broadcasted_iota(jnp.int32, (q, q), 1)
    causal = row >= col
    lo = lax.broadcasted_iota(jnp.int32, (q, LANES), 1) < SSM_HEAD_DIM
    heads_per_group = SSM_HEADS // SSM_GROUPS
    group_w = heads_per_group * SSM_HEAD_DIM

    for sub in range(SSD_CHUNKS_PER_STEP):
        rows = slice(sub * q, (sub + 1) * q)
        prev_rows = slice((sub - 1) * q, sub * q)

        for c0 in range(0, SSM_CONV_CH, SSD_CONV_COLS):
            cols = slice(c0, c0 + SSD_CONV_COLS)
            x_cur = xbc_ref[rows, cols]
            x_prev = xprev_ref[:, cols] if sub == 0 else xbc_ref[prev_rows, cols]
            shifted = jnp.dot(sel, jnp.concatenate([x_prev, x_cur], axis=0),
                              preferred_element_type=F32)
            conv = cw[taps - 1:taps, cols] * x_cur.astype(F32) + cb_ref[:, cols]
            for k in range(1, taps):
                conv = conv + cw[taps - 1 - k:taps - k, cols] * shifted[(k - 1) * q:k * q, :]
            xc_ref[rows, cols] = _silu_of_half(conv)

        dt = _softplus(dt_ref[rows, :] + dtb_ref[...])
        a = dt * (-LOG2E * jnp.exp(alog_ref[...]))
        acs = jnp.dot(causal.astype(F32), a, preferred_element_type=F32,
                      precision=lax.Precision.HIGHEST)
        dt_t = dt.T
        acs_t = acs.T
        tot = acs_t[:, q - 1:q]
        w_t = dt_t * jnp.exp2(tot - acs_t)
        src_t = acs_t - jnp.log2(dt_t)
        eacs = jnp.exp2(acs)
        dec = jnp.exp2(tot)

        for g in range(SSM_GROUPS):
            b0 = SSM_INNER + g * n
            c0 = SSM_INNER + SSM_GROUPS * n + g * n
            bg = xc_ref[rows, b0:b0 + n]
            cg = xc_ref[rows, c0:c0 + n]
            cb = lax.dot_general(cg.astype(BF16), bg.astype(BF16), (((1,), (1,)), ((), ())),
                                 preferred_element_type=F32)
            bg_t = bg.T
            for pr in range(heads_per_group // 2):
                j = g * (heads_per_group // 2) + pr
                sl = slice(j * LANES, (j + 1) * LANES)
                x = xc_ref[rows, sl]
                xb = x.astype(BF16)
                s_prev = s_ref[:, sl]
                rhs = jnp.concatenate([xb, s_prev.astype(BF16)], axis=0)
                ys, news, decs = [], [], []
                for e in (0, 1):
                    h = 2 * j + e
                    seg = acs[:, h:h + 1] - src_t[h:h + 1, :]
                    m_h = (cb * jnp.exp2(jnp.where(causal, seg, NEG))).astype(BF16)
                    w2 = (cg * eacs[:, h:h + 1]).astype(BF16)
                    lhs = jnp.concatenate([m_h, w2], axis=1)
                    ys.append(jnp.dot(lhs, rhs, preferred_element_type=F32))
                    bw_t = (bg_t * w_t[h:h + 1, :]).astype(BF16)
                    news.append(jnp.dot(bw_t, xb, preferred_element_type=F32))
                    decs.append(jnp.broadcast_to(dec[h:h + 1, :], (n, LANES)))
                y = jnp.where(lo, ys[0], ys[1])
                s_ref[:, sl] = (s_prev * jnp.where(lo, decs[0], decs[1])
                                + jnp.where(lo, news[0], news[1]))
                y = y + dsk_ref[:, sl] * x
                u_ref[rows, sl] = y * _silu_of_half(z_ref[rows, sl].astype(F32))
            gs = slice(g * group_w, (g + 1) * group_w)
            ug = u_ref[rows, gs]
            ms = jnp.mean(ug * ug, axis=-1, keepdims=True)
            o_ref[rows, gs] = (ug * lax.rsqrt(ms + RMS_EPS) * ng_ref[:, gs]).astype(o_ref.dtype)

    xprev_ref[...] = xbc_ref[(SSD_CHUNKS_PER_STEP - 1) * q:, :]


def _ssd(z, xbc, dt_raw, conv_w, conv_b, dt_bias, a_log, d_skip, norm_g):
    q = SSM_CHUNK
    rows = SSD_CHUNKS_PER_STEP * q
    steps = SEQ // rows
    row = lambda b, c: (b * steps + c, 0)
    sel = _shift_select(q, SSM_CONV_WIDTH)
    return pl.pallas_call(
        _ssd_kernel,
        grid=(BATCH, steps),
        in_specs=[pl.BlockSpec((rows, SSM_INNER), row),
                  pl.BlockSpec((rows, SSM_CONV_CH), row),
                  pl.BlockSpec((rows, LANES), row),
                  _resident(sel.shape),
                  _resident(conv_w.shape), _resident(conv_b.shape), _resident(dt_bias.shape),
                  _resident(a_log.shape), _resident(d_skip.shape), _resident(norm_g.shape)],
        out_specs=pl.BlockSpec((rows, SSM_INNER), row),
        out_shape=jax.ShapeDtypeStruct((TOKENS, SSM_INNER), BF16),
        scratch_shapes=[pltpu.VMEM((q, SSM_CONV_CH), BF16),
                        pltpu.VMEM((rows, SSM_CONV_CH), F32),
                        pltpu.VMEM((SSM_STATE, SSM_INNER), F32),
                        pltpu.VMEM((rows, SSM_INNER), F32)],
        compiler_params=_params(("arbitrary", "arbitrary")),
        name="mamba2_ssd",
    )(z, xbc, dt_raw, sel, conv_w, conv_b, dt_bias, a_log, d_skip, norm_g)


def _pad_cols(w, n):
    return jnp.pad(w, ((0, 0), (0, n - w.shape[1])))


def _pad_rows(w, n):
    return jnp.pad(w, ((0, n - w.shape[0]), (0, 0)))


def _row(v):
    return v.reshape(1, -1)


def _even_in_proj(h, w_in, casts):
    c_end = 3 * CONV_DIM
    a_end = c_end + 3 * FOX_DIM
    w_t = w_in.T
    w_c = w_t[:c_end].astype(BF16)
    w_qkv = w_t[c_end:a_end].astype(BF16)
    w_f = _pad_rows(w_t[a_end:], LANES)
    q_scale = (FOX_HEAD_DIM ** -0.5) * LOG2E
    return _proj(h, [w_c, w_qkv, w_f], [BF16, BF16, F32], "even_in_proj",
                 lead_scales=[(0, 1.0), (FOX_DIM, q_scale), (0, 1.0)], casts=casts)


def _even_mixer(conv_in, qkv, f_logit, b_f, w_conv, w_out_bf16):
    fp_col, fp_row = _fcum(f_logit, _pad_cols(_row(b_f), LANES))
    yb = _fox_attention(qkv, fp_col, fp_row)
    tiles = [(conv_in, CONV_DIM, 0), (conv_in, CONV_DIM, 1), (conv_in, CONV_DIM, 2), (yb, FOX_DIM, 0)]
    return _even_mix, tiles, [w_conv, w_out_bf16], [pltpu.VMEM((HALO, CONV_DIM), F32)]


def _odd_mixer(h, w_in, w_zx_bf16, conv_w, conv_b, dt_bias, a_log, d_skip, norm_g, w_out_bf16):
    x_end = SSM_INNER + SSM_CONV_CH
    assert w_zx_bf16.shape == (x_end, D_MODEL)
    w_dt = _pad_rows(w_in.T[x_end:], LANES)
    z, xbc, dt_raw = _proj(h, [w_zx_bf16, w_zx_bf16, w_dt], [BF16, BF16, F32], "odd_in_proj",
                           lead_scales=[(SSM_INNER, 0.5), (0, 1.0), (0, 1.0)],
                           views=[(0, SSM_INNER), (SSM_INNER, SSM_CONV_CH), (0, LANES)])
    u = _ssd(z, xbc, dt_raw, 0.5 * conv_w, _row(0.5 * conv_b), _pad_cols(_row(dt_bias), LANES),
             _pad_cols(_row(a_log), LANES), _row(jnp.repeat(d_skip, SSM_HEAD_DIM)), _row(norm_g))
    return _odd_mix, [(u, SSM_INNER, 0)], [w_out_bf16], []


def kernel(x, p, even_w_in, even_b_f, even_conv_w, even_w_out, odd_w_in, odd_conv_w, odd_conv_b,
           odd_dt_bias, odd_a_log, odd_d_skip, odd_norm_g, odd_w_out, ln_mix_g, ln_mix_b, ffn_w_up,
           ffn_conv_w, ffn_conv_b, ffn_w_down, ln_ffn_g, ln_ffn_b, ple_w_proj, ple_w_gate,
           ple_b_gate):
    h = x.reshape(TOKENS, D_MODEL)
    rows = lambda v: v.reshape(DEPTH, 1, -1)
    flat = lambda w: w.reshape(-1, w.shape[-1])
    weights = (ffn_w_up, ffn_w_down, ple_w_proj, ple_w_gate, even_w_out, odd_w_out)
    odd_in_t = [w.T for w in odd_w_in]
    conv_in, qkv, f_logit, *converted = _even_in_proj(h, even_w_in[0],
                                                      [flat(w) for w in weights] + odd_in_t)
    w_up, w_down, w_proj, w_gate, w_out_even, w_out_odd = [
        c.reshape(w.shape) for c, w in zip(converted, weights)]
    w_zx_odd = converted[len(weights):]
    gate_half = jnp.where(jnp.arange(2 * D_FF) < D_FF, 0.5, 1.0)
    stacked = (rows(ln_mix_g), rows(ln_mix_b), w_up, ffn_conv_w * gate_half,
               rows(ffn_conv_b * gate_half), w_down, rows(ln_ffn_g), rows(ln_ffn_b), w_proj, w_gate,
               rows(ple_b_gate))
    p_all = p.reshape(DEPTH * TOKENS, PLE_DIM)
    for i in range(DEPTH):
        j = i // 2
        if i % 2 == 0:
            name = "even_layer_tail"
            if i > 0:
                conv_in, qkv, f_logit = _even_in_proj(h, even_w_in[j], [])
            mix = _even_mixer(conv_in, qkv, f_logit, even_b_f[j], even_conv_w[j], w_out_even[j])
        else:
            name = "odd_layer_tail"
            mix = _odd_mixer(h, odd_w_in[j], w_zx_odd[j], odd_conv_w[j], odd_conv_b[j], odd_dt_bias[j],
                             odd_a_log[j], odd_d_skip[j], odd_norm_g[j], w_out_odd[j])
        h = _layer_tail(name, i, *mix, h, p_all, stacked)
    return h.reshape(BATCH, SEQ, D_MODEL)
```

```python
import functools

import jax
import jax.numpy as jnp
from jax import lax
from jax.experimental import pallas as pl
from jax.experimental.pallas import tpu as pltpu

F32 = jnp.float32
BF16 = jnp.bfloat16

D_MODEL = 1024
BATCH = 2
SEQ = 8192
DEPTH = 2
TOKENS = BATCH * SEQ

CONV_DIM = 512
CONV_WIDTH = 3
FOX_HEADS = 8
FOX_HEAD_DIM = 64
FOX_DIM = FOX_HEADS * FOX_HEAD_DIM
SSM_INNER = 2 * D_MODEL
SSM_HEAD_DIM = 64
SSM_HEADS = SSM_INNER // SSM_HEAD_DIM
SSM_GROUPS = 4
SSM_STATE = 128
SSM_CONV_WIDTH = 4
SSM_CHUNK = 128
SSM_CONV_CH = SSM_INNER + 2 * SSM_GROUPS * SSM_STATE
D_FF = 2816
FFN_CONV_WIDTH = 3
PLE_DIM = 256
LN_EPS = 1e-5
RMS_EPS = 1e-5
ALPHA = (2.0 * DEPTH) ** 0.25

LANES = 128
HALO = 8
NEG = -1e30
VMEM_LIMIT = 56 * 1024 * 1024

ROW_TILE = 512
SSD_CONV_COLS = 512
SSD_CHUNKS_PER_STEP = 4
FFN_ROW_TILE = 512
FFN_CHUNK = 256
FFN_DOWN_GROUP_ENDS = (6, 11)
ATTN_TILE = 512
LOG2E = 1.4426950408889634


def _resident(shape):
    nd = len(shape)
    return pl.BlockSpec(shape, lambda *_: (0,) * nd, pipeline_mode=pl.Buffered(1))


def _resident_layer(shape, layer):
    nd = len(shape)
    return pl.BlockSpec((None,) + tuple(shape[1:]), lambda *_: (layer,) + (0,) * (nd - 1),
                        pipeline_mode=pl.Buffered(1))


def _params(sem, flags=None):
    return pltpu.CompilerParams(dimension_semantics=sem, vmem_limit_bytes=VMEM_LIMIT, flags=flags)


def _sigmoid(x):
    return 0.5 + 0.5 * jnp.tanh(0.5 * x)


def _silu_of_half(h):
    return h + h * jnp.tanh(h)


def _silu(x):
    return _silu_of_half(0.5 * x)


def _softplus(x):
    return jnp.maximum(x, 0.0) + jnp.log1p(jnp.exp(-jnp.abs(x)))


def _log_sigmoid(x):
    return jnp.minimum(x, 0.0) - jnp.log1p(jnp.exp(-jnp.abs(x)))


def _layer_norm(r, g, b):
    mu = jnp.mean(r, axis=-1, keepdims=True)
    d = r - mu
    var = jnp.mean(d * d, axis=-1, keepdims=True)
    return d * lax.rsqrt(var + LN_EPS) * g + b


def _causal_taps_rolled(cur, prev, w):
    k_taps = w.shape[0]
    sub = lax.broadcasted_iota(jnp.int32, prev.shape, 0)
    out = w[k_taps - 1:k_taps, :] * cur
    for k in range(k_taps - 1):
        shift = k_taps - 1 - k
        rolled = pltpu.roll(cur, shift, axis=0)
        head = jnp.where(sub < shift, pltpu.roll(prev, shift, axis=0), rolled[0:HALO, :])
        out = out + w[k:k + 1, :] * jnp.concatenate([head, rolled[HALO:, :]], axis=0)
    return out


def _dot_nt(a, b_t):
    return lax.dot_general(a, b_t, (((1,), (1,)), ((), ())), preferred_element_type=F32)


def _proj_kernel(x_ref, *refs, n_out, n_cast, chunk, lead_scales, precise, row0s):
    w_refs = refs[:n_out]
    cast_in = refs[n_out:n_out + n_cast]
    o_refs = refs[n_out + n_cast:2 * n_out + n_cast]
    cast_out = refs[2 * n_out + n_cast:]
    for src_ref, dst_ref in zip(cast_in, cast_out):
        dst_ref[...] = src_ref[...].astype(BF16)
    x = x_ref[...]
    xb = x.astype(BF16)
    for w_ref, o_ref, (lead_cols, lead_scale), hi_lo, row0 in zip(w_refs, o_refs, lead_scales, precise,
                                                                   row0s):
        n = o_ref.shape[1]
        if hi_lo:
            x_lo = (x - xb.astype(F32)).astype(BF16)
            both = _dot_nt(xb, w_ref[...])
            cross = _dot_nt(x_lo, w_ref[0:n, :])
            o_ref[...] = (both[:, 0:n] + both[:, n:] + cross).astype(o_ref.dtype)
            continue
        for c0 in range(0, n, chunk):
            c1 = min(c0 + chunk, n)
            acc = _dot_nt(xb, w_ref[row0 + c0:row0 + c1, :])
            if c1 <= lead_cols:
                acc = acc * lead_scale
            o_ref[:, c0:c1] = acc.astype(o_ref.dtype)


def _split_bf16(w_t):
    w_hi = w_t.astype(BF16)
    w_lo = (w_t - w_hi.astype(F32)).astype(BF16)
    return jnp.concatenate([w_hi, w_lo], axis=0)


def _proj(x, ws_t, out_dtypes, name, lead_scales=None, casts=(), views=None):
    m, k = x.shape
    tm = ROW_TILE
    steps = m // tm
    chunk = 512
    if lead_scales is None:
        lead_scales = [(0, 1.0)] * len(ws_t)
    assert all(cols % chunk == 0 for cols, _ in lead_scales)
    precise = tuple(w.dtype == F32 for w in ws_t)
    if views is None:
        views = [(0, w.shape[0]) for w in ws_t]
    widths = [rows for _, rows in views]
    ws = [_split_bf16(w) if p else w for w, p in zip(ws_t, precise)]
    bf16_rows = 2 * HALO
    cast_rows = [c.shape[0] // (steps * bf16_rows) * (steps * bf16_rows) for c in casts]
    slab_specs = [pl.BlockSpec((r // steps, c.shape[1]), lambda i: (i, 0))
                  for c, r in zip(casts, cast_rows)]
    return pl.pallas_call(
        functools.partial(_proj_kernel, n_out=len(ws), n_cast=len(casts), chunk=chunk,
                          lead_scales=tuple(lead_scales), precise=precise,
                          row0s=tuple(r0 for r0, _ in views)),
        grid=(steps,),
        in_specs=([pl.BlockSpec((tm, k), lambda i: (i, 0))] + [_resident(w.shape) for w in ws]
                  + slab_specs),
        out_specs=[pl.BlockSpec((tm, n), lambda i: (i, 0)) for n in widths] + slab_specs,
        out_shape=([jax.ShapeDtypeStruct((m, n), dt) for n, dt in zip(widths, out_dtypes)]
                   + [jax.ShapeDtypeStruct((r, c.shape[1]), BF16) for c, r in zip(casts, cast_rows)]),
        compiler_params=_params(("arbitrary",)),
        name=name,
    )(x, *ws, *casts)


def _fcum_kernel(f_ref, b_ref, col_ref, row_ref, carry_ref, *, rows):
    j = pl.program_id(1)

    @pl.when(j == 0)
    def _():
        carry_ref[...] = jnp.zeros_like(carry_ref)

    r = lax.broadcasted_iota(jnp.int32, (LANES, LANES), 0)
    c = lax.broadcasted_iota(jnp.int32, (LANES, LANES), 1)
    tri = (r >= c).astype(BF16)
    nh = FOX_HEADS
    carry = carry_ref[0:1, :]
    for blk in range(rows // LANES):
        sl = slice(blk * LANES, (blk + 1) * LANES)
        lf = _log_sigmoid(f_ref[sl, :] + b_ref[...])
        l_hi = lf.astype(BF16)
        l_mid = (lf - l_hi.astype(F32)).astype(BF16)
        l_lo = (lf - l_hi.astype(F32) - l_mid.astype(F32)).astype(BF16)
        parts = jnp.dot(tri, jnp.concatenate([l_hi, l_mid, l_lo], axis=1), preferred_element_type=F32)
        cs = parts[:, 0:LANES] + parts[:, LANES:2 * LANES] + parts[:, 2 * LANES:]
        cum = cs + carry
        carry = cum[LANES - 1:LANES, :]
        f2 = cum * LOG2E
        hi = f2.astype(BF16).astype(F32)
        mid = (f2 - hi).astype(BF16).astype(F32)
        lo = (f2 - hi - mid).astype(BF16).astype(F32)
        packed = jnp.where(c < nh, hi,
                           jnp.where(c < 2 * nh, pltpu.roll(mid, nh, axis=1),
                                     jnp.where(c < 3 * nh, pltpu.roll(lo, 2 * nh, axis=1), 0.0)))
        col_ref[sl, :] = packed.astype(BF16)
        row_ref[0, :, sl] = jnp.concatenate([hi.T[0:nh, :], mid.T[0:nh, :], lo.T[0:nh, :],
                                             jnp.zeros((nh, LANES), F32)], axis=0)
    carry_ref[...] = jnp.broadcast_to(carry, carry_ref.shape)


def _fcum(f_logit, b_f):
    rows = ROW_TILE
    nblk = SEQ // rows
    return pl.pallas_call(
        functools.partial(_fcum_kernel, rows=rows),
        grid=(BATCH, nblk),
        in_specs=[pl.BlockSpec((rows, LANES), lambda b, j: (b * nblk + j, 0)),
                  pl.BlockSpec((1, LANES), lambda b, j: (0, 0))],
        out_specs=[pl.BlockSpec((rows, LANES), lambda b, j: (b * nblk + j, 0)),
                   pl.BlockSpec((1, 4 * FOX_HEADS, rows), lambda b, j: (b, 0, j))],
        out_shape=[jax.ShapeDtypeStruct((TOKENS, LANES), BF16),
                   jax.ShapeDtypeStruct((BATCH, 4 * FOX_HEADS, SEQ), F32)],
        scratch_shapes=[pltpu.VMEM((HALO, LANES), F32)],
        compiler_params=_params(("arbitrary", "arbitrary")),
        name="fox_forget_cumsum",
    )(f_logit, b_f)


def _fox_kernel(q_ref, k_ref, v_ref, fpc_ref, fpr_ref, o_ref, kaug_ref, vt_ref, qaug_ref,
                s0_ref, s1_ref, bm0_ref, bm1_ref, m_ref, acc_ref, *, t):
    hp = pl.program_id(1)
    qi = pl.program_id(2)
    d = FOX_HEAD_DIM
    nh = FOX_HEADS
    prep_rows = 1024

    @pl.when(qi == 0)
    def _():
        r = lax.broadcasted_iota(jnp.int32, (LANES, LANES), 0)
        c = lax.broadcasted_iota(jnp.int32, (LANES, LANES), 1)
        lane = lax.broadcasted_iota(jnp.int32, (1, LANES), 1)
        ones_lanes = jnp.where(lane < 3, 1.0, 0.0)
        sub = lax.broadcasted_iota(jnp.int32, (LANES, prep_rows), 0)
        for e in (0, 1):
            h = 2 * hp + e
            pick = jnp.where((c >= 3) & (c < 6) & (r == (c - 3) * nh + h), -1.0, 0.0).astype(BF16)
            for blk in range(SEQ // prep_rows):
                rows = slice(blk * prep_rows, (blk + 1) * prep_rows)
                aug = jnp.dot(fpc_ref[rows, :], pick, preferred_element_type=F32) + ones_lanes
                kaug_ref[e, rows, :] = aug.astype(BF16)
                v_t = v_ref[rows, :].astype(F32).T
                own = (sub < d) if e == 0 else (sub >= d)
                vt_ref[e, :, rows] = jnp.where(own, v_t, 1.0).astype(BF16)

    q_t = q_ref[...].astype(F32).T
    sub_q = lax.broadcasted_iota(jnp.int32, (LANES, t), 0)
    sub8 = lax.broadcasted_iota(jnp.int32, (HALO, t), 0)
    q_aug = []
    for e in (0, 1):
        h = 2 * hp + e
        own = (sub_q < d) if e == 0 else (sub_q >= d)
        f_hi = fpr_ref[0, pl.ds(h, 1), :]
        f_mid = fpr_ref[0, pl.ds(nh + h, 1), :]
        f_lo = fpr_ref[0, pl.ds(2 * nh + h, 1), :]
        top = jnp.where(sub8 == 0, f_hi,
                        jnp.where(sub8 == 1, f_mid,
                                  jnp.where(sub8 == 2, f_lo, jnp.where(sub8 < 6, 1.0, 0.0))))
        q_aug.append(jnp.concatenate(
            [jnp.where(own, q_t, 0.0), top, jnp.zeros((LANES - HALO, t), F32)], axis=0).astype(BF16))

    for e in (0, 1):
        qaug_ref[e] = q_aug[e]
        m_ref[e] = jnp.full((1, t), NEG, F32)
        acc_ref[e] = jnp.zeros((LANES, t), F32)

    key_idx = lax.broadcasted_iota(jnp.int32, (t, t), 0)
    qry_idx = lax.broadcasted_iota(jnp.int32, (t, t), 1)
    causal = key_idx <= qry_idx

    def score_stage(j, s_ref, bm_ref):
        start = pl.multiple_of(j * t, t)
        k2 = k_ref[pl.ds(start, t), :]
        for e in (0, 1):
            k_aug = jnp.concatenate([k2, kaug_ref[e, pl.ds(start, t), :]], axis=1)
            s_t = jnp.dot(k_aug, qaug_ref[e], preferred_element_type=F32)
            s_ref[e] = s_t
            bm_ref[e] = jnp.max(s_t, axis=0, keepdims=True)

    def softmax_stage(j, s_ref, bm_ref, masked):
        start = pl.multiple_of(j * t, t)
        for e in (0, 1):
            s_t = s_ref[e]
            if masked:
                s_t = jnp.where(causal, s_t, NEG)
                bm = jnp.max(s_t, axis=0, keepdims=True)
            else:
                bm = bm_ref[e]
            m_prev = m_ref[e]
            m_new = jnp.maximum(m_prev, bm)
            m_ref[e] = m_new
            p_t = jnp.exp2(s_t - m_new).astype(BF16)
            pv = jnp.dot(vt_ref[e, :, pl.ds(start, t)], p_t, preferred_element_type=F32)
            acc_ref[e] = jnp.exp2(m_prev - m_new) * acc_ref[e] + pv

    score_stage(qi, s1_ref, bm1_ref)
    score_stage(0, s0_ref, bm0_ref)
    softmax_stage(qi, s1_ref, bm1_ref, True)

    def two_blocks(i, _):
        j = 2 * i
        score_stage(j + 1, s1_ref, bm1_ref)
        softmax_stage(j, s0_ref, bm0_ref, False)
        score_stage(j + 2, s0_ref, bm0_ref)
        softmax_stage(j + 1, s1_ref, bm1_ref, False)
        return 0

    def pairs_per_trip(n_pairs):
        def body(i, _):
            for pair in range(n_pairs):
                two_blocks(n_pairs * i + pair, 0)
            return 0
        return body

    octs = lax.shift_right_logical(qi, 3)
    quads = lax.shift_right_logical(qi, 2)
    lax.fori_loop(0, octs, pairs_per_trip(4), 0)
    lax.fori_loop(2 * octs, quads, pairs_per_trip(2), 0)
    lax.fori_loop(2 * quads, lax.shift_right_logical(qi, 1), two_blocks, 0)

    @pl.when(lax.bitwise_and(qi, 1) == 1)
    def _():
        softmax_stage(qi - 1, s0_ref, bm0_ref, False)

    a0 = acc_ref[0]
    a1 = acc_ref[1]
    o_t = jnp.concatenate([a0[0:d, :] / a0[d:d + 1, :], a1[d:, :] / a1[0:1, :]], axis=0)
    o_ref[...] = o_t.T.astype(o_ref.dtype)


def _fox_attention(qkv, fp_col, fp_row):
    t = ATTN_TILE
    nq = SEQ // t
    pairs = FOX_HEADS // 2
    return pl.pallas_call(
        functools.partial(_fox_kernel, t=t),
        grid=(BATCH, pairs, nq),
        in_specs=[pl.BlockSpec((t, LANES), lambda b, h, i: (b * nq + i, h)),
                  pl.BlockSpec((SEQ, LANES), lambda b, h, i: (b, pairs + h)),
                  pl.BlockSpec((SEQ, LANES), lambda b, h, i: (b, 2 * pairs + h)),
                  pl.BlockSpec((SEQ, LANES), lambda b, h, i: (b, 0)),
                  pl.BlockSpec((1, 4 * FOX_HEADS, t), lambda b, h, i: (b, 0, i))],
        out_specs=pl.BlockSpec((t, LANES), lambda b, h, i: (b * nq + i, h)),
        out_shape=jax.ShapeDtypeStruct((TOKENS, FOX_DIM), BF16),
        scratch_shapes=[pltpu.VMEM((2, SEQ, LANES), BF16),
                        pltpu.VMEM((2, LANES, SEQ), BF16),
                        pltpu.VMEM((2, 2 * LANES, t), BF16),
                        pltpu.VMEM((2, t, t), F32),
                        pltpu.VMEM((2, t, t), F32),
                        pltpu.VMEM((2, 1, t), F32),
                        pltpu.VMEM((2, 1, t), F32),
                        pltpu.VMEM((2, 1, t), F32),
                        pltpu.VMEM((2, LANES, t), F32)],
        compiler_params=_params(("arbitrary", "arbitrary", "arbitrary")),
        name="fox_attention",
    )(qkv, qkv, qkv, fp_col, fp_row)


def _even_mix(first_tile, gb_ref, gc_ref, hh_ref, yb_ref, wc_ref, w_ref, chalo_ref):
    @pl.when(first_tile)
    def _():
        chalo_ref[...] = jnp.zeros_like(chalo_ref)

    u = gc_ref[...].astype(F32) * hh_ref[...].astype(F32)
    prev = chalo_ref[...]
    chalo_ref[...] = u[u.shape[0] - HALO:, :]
    ya = (gb_ref[...].astype(F32) * _causal_taps_rolled(u, prev, wc_ref[...])).astype(BF16)
    mix = jnp.dot(ya, w_ref[0:CONV_DIM, :], preferred_element_type=F32)
    return mix + jnp.dot(yb_ref[...], w_ref[CONV_DIM:, :], preferred_element_type=F32)


def _odd_mix(first_tile, u_ref, w_ref):
    return jnp.dot(u_ref[...], w_ref[...], preferred_element_type=F32)


def _tail_kernel(*refs, mix_fn, n_mix, tm, tiles_per_seq):
    mix_refs = refs[:n_mix]
    (h_ref, p_ref, g1_ref, b1_ref, wup_ref, cw_ref, cb_ref, wdn_ref, g_ref, b_ref, wproj_ref, wgate_ref,
     bgate_ref, o_ref, halo_ref, act_ref, acc_ref) = refs[n_mix:n_mix + 17]
    mix_scratch = refs[n_mix + 17:]
    i = pl.program_id(0)
    first_tile = lax.rem(i, tiles_per_seq) == 0

    @pl.when(first_tile)
    def _():
        halo_ref[...] = jnp.zeros_like(halo_ref)

    mix = mix_fn(first_tile, *mix_refs, *mix_scratch)
    x = _layer_norm(ALPHA * h_ref[...] + mix, g1_ref[...], b1_ref[...])
    xb = x.astype(BF16)
    tf = FFN_CHUNK
    n_chunks = D_FF // tf

    def up(c):
        return [jnp.dot(xb, wup_ref[:, part * D_FF + c * tf:part * D_FF + (c + 1) * tf],
                        preferred_element_type=F32) for part in (0, 1)]

    us = up(0)
    group_start = 0
    for c in range(n_chunks):
        us_next = up(c + 1) if c + 1 < n_chunks else None
        branches = []
        for part in (0, 1):
            c0 = part * D_FF + c * tf
            u = us[part]
            prev = halo_ref[:, c0:c0 + tf]
            halo_ref[:, c0:c0 + tf] = u[tm - HALO:, :]
            branches.append(_causal_taps_rolled(u, prev, cw_ref[:, c0:c0 + tf]) + cb_ref[:, c0:c0 + tf])
        us = us_next
        act_ref[:, c * tf:(c + 1) * tf] = (_silu_of_half(branches[0]) * branches[1]).astype(BF16)
        if (c + 1) in FFN_DOWN_GROUP_ENDS:
            k0, k1 = group_start * tf, (c + 1) * tf
            d = jnp.dot(act_ref[:, k0:k1], wdn_ref[k0:k1, :], preferred_element_type=F32)
            if group_start == 0:
                acc_ref[...] = d
            else:
                acc_ref[...] += d
            group_start = c + 1
    h2 = _layer_norm(ALPHA * x + acc_ref[...], g_ref[...], b_ref[...])
    gate_logit = jnp.dot(h2.astype(BF16), wgate_ref[...], preferred_element_type=F32) + bgate_ref[...]
    emb = jnp.dot(p_ref[...].astype(BF16), wproj_ref[...], preferred_element_type=F32)
    o_ref[...] = h2 + _sigmoid(gate_logit) * emb


def _layer_tail(name, layer, mix_fn, mix_tiles, mix_weights, mix_scratch, h, p_all, stacked):
    tm = FFN_ROW_TILE
    tiles_per_layer = TOKENS // tm
    tile_specs = [pl.BlockSpec((tm, width), functools.partial(lambda i, col: (i, col), col=col))
                  for _, width, col in mix_tiles]
    mix_arrays = [a for a, _, _ in mix_tiles] + list(mix_weights)
    rest = [h, p_all, *stacked]
    return pl.pallas_call(
        functools.partial(_tail_kernel, mix_fn=mix_fn, n_mix=len(mix_arrays), tm=tm,
                          tiles_per_seq=SEQ // tm),
        grid=(TOKENS // tm,),
        in_specs=(tile_specs + [_resident(w.shape) for w in mix_weights]
                  + [pl.BlockSpec((tm, D_MODEL), lambda i: (i, 0)),
                     pl.BlockSpec((tm, PLE_DIM), lambda i: (layer * tiles_per_layer + i, 0))]
                  + [_resident_layer(a.shape, layer) for a in stacked]),
        out_specs=pl.BlockSpec((tm, D_MODEL), lambda i: (i, 0)),
        out_shape=jax.ShapeDtypeStruct((TOKENS, D_MODEL), F32),
        scratch_shapes=[pltpu.VMEM((HALO, 2 * D_FF), F32),
                        pltpu.VMEM((tm, D_FF), BF16),
                        pltpu.VMEM((tm, D_MODEL), F32)] + list(mix_scratch),
        compiler_params=_params(("arbitrary",)),
        name=name,
    )(*mix_arrays, *rest)


def _shift_select(q, taps):
    r = jnp.arange((taps - 1) * q)[:, None]
    c = jnp.arange(2 * q)[None, :]
    return (c == q + r % q - (r // q + 1)).astype(BF16)


def _ssd_kernel(z_ref, xbc_ref, dt_ref, sel_ref, cw_ref, cb_ref, dtb_ref, alog_ref, dsk_ref, ng_ref,
                o_ref, xprev_ref, xc_ref, s_ref, u_ref):
    q = SSM_CHUNK
    n = SSM_STATE

    @pl.when(pl.program_id(1) == 0)
    def _():
        xprev_ref[...] = jnp.zeros_like(xprev_ref)
        s_ref[...] = jnp.zeros_like(s_ref)

    cw = cw_ref[...]
    taps = SSM_CONV_WIDTH
    sel = sel_ref[...]
    row = lax.broadcasted_iota(jnp.int32, (q, q), 0)
    col = lax.broadcasted_iota(jnp.int32, (q, q), 1)
    causal = row >= col
    lo = lax.broadcasted_iota(jnp.int32, (q, LANES), 1) < SSM_HEAD_DIM
    heads_per_group = SSM_HEADS // SSM_GROUPS
    group_w = heads_per_group * SSM_HEAD_DIM

    for sub in range(SSD_CHUNKS_PER_STEP):
        rows = slice(sub * q, (sub + 1) * q)
        prev_rows = slice((sub - 1) * q, sub * q)

        for c0 in range(0, SSM_CONV_CH, SSD_CONV_COLS):
            cols = slice(c0, c0 + SSD_CONV_COLS)
            x_cur = xbc_ref[rows, cols]
            x_prev = xprev_ref[:, cols] if sub == 0 else xbc_ref[prev_rows, cols]
            shifted = jnp.dot(sel, jnp.concatenate([x_prev, x_cur], axis=0),
                              preferred_element_type=F32)
            conv = cw[taps - 1:taps, cols] * x_cur.astype(F32) + cb_ref[:, cols]
            for k in range(1, taps):
                conv = conv + cw[taps - 1 - k:taps - k, cols] * shifted[(k - 1) * q:k * q, :]
            xc_ref[rows, cols] = _silu_of_half(conv)

        dt = _softplus(dt_ref[rows, :] + dtb_ref[...])
        a = dt * (-LOG2E * jnp.exp(alog_ref[...]))
        acs = jnp.dot(causal.astype(F32), a, preferred_element_type=F32,
                      precision=lax.Precision.HIGHEST)
        dt_t = dt.T
        acs_t = acs.T
        tot = acs_t[:, q - 1:q]
        w_t = dt_t * jnp.exp2(tot - acs_t)
        src_t = acs_t - jnp.log2(dt_t)
        eacs = jnp.exp2(acs)
        dec = jnp.exp2(tot)

        for g in range(SSM_GROUPS):
            b0 = SSM_INNER + g * n
            c0 = SSM_INNER + SSM_GROUPS * n + g * n
            bg = xc_ref[rows, b0:b0 + n]
            cg = xc_ref[rows, c0:c0 + n]
            cb = lax.dot_general(cg.astype(BF16), bg.astype(BF16), (((1,), (1,)), ((), ())),
                                 preferred_element_type=F32)
            bg_t = bg.T
            for pr in range(heads_per_group // 2):
                j = g * (heads_per_group // 2) + pr
                sl = slice(j * LANES, (j + 1) * LANES)
                x = xc_ref[rows, sl]
                xb = x.astype(BF16)
                s_prev = s_ref[:, sl]
                rhs = jnp.concatenate([xb, s_prev.astype(BF16)], axis=0)
                ys, news, decs = [], [], []
                for e in (0, 1):
                    h = 2 * j + e
                    seg = acs[:, h:h + 1] - src_t[h:h + 1, :]
                    m_h = (cb * jnp.exp2(jnp.where(causal, seg, NEG))).astype(BF16)
                    w2 = (cg * eacs[:, h:h + 1]).astype(BF16)
                    lhs = jnp.concatenate([m_h, w2], axis=1)
                    ys.append(jnp.dot(lhs, rhs, preferred_element_type=F32))
                    bw_t = (bg_t * w_t[h:h + 1, :]).astype(BF16)
                    news.append(jnp.dot(bw_t, xb, preferred_element_type=F32))
                    decs.append(jnp.broadcast_to(dec[h:h + 1, :], (n, LANES)))
                y = jnp.where(lo, ys[0], ys[1])
                s_ref[:, sl] = (s_prev * jnp.where(lo, decs[0], decs[1])
                                + jnp.where(lo, news[0], news[1]))
                y = y + dsk_ref[:, sl] * x
                u_ref[rows, sl] = y * _silu_of_half(z_ref[rows, sl].astype(F32))
            gs = slice(g * group_w, (g + 1) * group_w)
            ug = u_ref[rows, gs]
            ms = jnp.mean(ug * ug, axis=-1, keepdims=True)
            o_ref[rows, gs] = (ug * lax.rsqrt(ms + RMS_EPS) * ng_ref[:, gs]).astype(o_ref.dtype)

    xprev_ref[...] = xbc_ref[(SSD_CHUNKS_PER_STEP - 1) * q:, :]


def _ssd(z, xbc, dt_raw, conv_w, conv_b, dt_bias, a_log, d_skip, norm_g):
    q = SSM_CHUNK
    rows = SSD_CHUNKS_PER_STEP * q
    steps = SEQ // rows
    row = lambda b, c: (b * steps + c, 0)
    sel = _shift_select(q, SSM_CONV_WIDTH)
    return pl.pallas_call(
        _ssd_kernel,
        grid=(BATCH, steps),
        in_specs=[pl.BlockSpec((rows, SSM_INNER), row),
                  pl.BlockSpec((rows, SSM_CONV_CH), row),
                  pl.BlockSpec((rows, LANES), row),
                  _resident(sel.shape),
                  _resident(conv_w.shape), _resident(conv_b.shape), _resident(dt_bias.shape),
                  _resident(a_log.shape), _resident(d_skip.shape), _resident(norm_g.shape)],
        out_specs=pl.BlockSpec((rows, SSM_INNER), row),
        out_shape=jax.ShapeDtypeStruct((TOKENS, SSM_INNER), BF16),
        scratch_shapes=[pltpu.VMEM((q, SSM_CONV_CH), BF16),
                        pltpu.VMEM((rows, SSM_CONV_CH), F32),
                        pltpu.VMEM((SSM_STATE, SSM_INNER), F32),
                        pltpu.VMEM((rows, SSM_INNER), F32)],
        compiler_params=_params(("arbitrary", "arbitrary")),
        name="mamba2_ssd",
    )(z, xbc, dt_raw, sel, conv_w, conv_b, dt_bias, a_log, d_skip, norm_g)


def _pad_cols(w, n):
    return jnp.pad(w, ((0, 0), (0, n - w.shape[1])))


def _pad_rows(w, n):
    return jnp.pad(w, ((0, n - w.shape[0]), (0, 0)))


def _row(v):
    return v.reshape(1, -1)


def _even_in_proj(h, w_in, casts):
    c_end = 3 * CONV_DIM
    a_end = c_end + 3 * FOX_DIM
    w_t = w_in.T
    w_c = w_t[:c_end].astype(BF16)
    w_qkv = w_t[c_end:a_end].astype(BF16)
    w_f = _pad_rows(w_t[a_end:], LANES)
    q_scale = (FOX_HEAD_DIM ** -0.5) * LOG2E
    return _proj(h, [w_c, w_qkv, w_f], [BF16, BF16, F32], "even_in_proj",
                 lead_scales=[(0, 1.0), (FOX_DIM, q_scale), (0, 1.0)], casts=casts)


def _even_mixer(conv_in, qkv, f_logit, b_f, w_conv, w_out_bf16):
    fp_col, fp_row = _fcum(f_logit, _pad_cols(_row(b_f), LANES))
    yb = _fox_attention(qkv, fp_col, fp_row)
    tiles = [(conv_in, CONV_DIM, 0), (conv_in, CONV_DIM, 1), (conv_in, CONV_DIM, 2), (yb, FOX_DIM, 0)]
    return _even_mix, tiles, [w_conv, w_out_bf16], [pltpu.VMEM((HALO, CONV_DIM), F32)]


def _odd_mixer(h, w_in, w_zx_bf16, conv_w, conv_b, dt_bias, a_log, d_skip, norm_g, w_out_bf16):
    x_end = SSM_INNER + SSM_CONV_CH
    assert w_zx_bf16.shape == (x_end, D_MODEL)
    w_dt = _pad_rows(w_in.T[x_end:], LANES)
    z, xbc, dt_raw = _proj(h, [w_zx_bf16, w_zx_bf16, w_dt], [BF16, BF16, F32], "odd_in_proj",
                           lead_scales=[(SSM_INNER, 0.5), (0, 1.0), (0, 1.0)],
                           views=[(0, SSM_INNER), (SSM_INNER, SSM_CONV_CH), (0, LANES)])
    u = _ssd(z, xbc, dt_raw, 0.5 * conv_w, _row(0.5 * conv_b), _pad_cols(_row(dt_bias), LANES),
             _pad_cols(_row(a_log), LANES), _row(jnp.repeat(d_skip, SSM_HEAD_DIM)), _row(norm_g))
    return _odd_mix, [(u, SSM_INNER, 0)], [w_out_bf16], []


def kernel(x, p, even_w_in, even_b_f, even_conv_w, even_w_out, odd_w_in, odd_conv_w, odd_conv_b,
           odd_dt_bias, odd_a_log, odd_d_skip, odd_norm_g, odd_w_out, ln_mix_g, ln_mix_b, ffn_w_up,
           ffn_conv_w, ffn_conv_b, ffn_w_down, ln_ffn_g, ln_ffn_b, ple_w_proj, ple_w_gate,
           ple_b_gate):
    h = x.reshape(TOKENS, D_MODEL)
    rows = lambda v: v.reshape(DEPTH, 1, -1)
    flat = lambda w: w.reshape(-1, w.shape[-1])
    weights = (ffn_w_up, ffn_w_down, ple_w_proj, ple_w_gate, even_w_out, odd_w_out)
    odd_in_t = [w.T for w in odd_w_in]
    conv_in, qkv, f_logit, *converted = _even_in_proj(h, even_w_in[0],
                                                      [flat(w) for w in weights] + odd_in_t)
    w_up, w_down, w_proj, w_gate, w_out_even, w_out_odd = [
        c.reshape(w.shape) for c, w in zip(converted, weights)]
    w_zx_odd = converted[len(weights):]
    gate_half = jnp.where(jnp.arange(2 * D_FF) < D_FF, 0.5, 1.0)
    stacked = (rows(ln_mix_g), rows(ln_mix_b), w_up, ffn_conv_w * gate_half,
               rows(ffn_conv_b * gate_half), w_down, rows(ln_ffn_g), rows(ln_ffn_b), w_proj, w_gate,
               rows(ple_b_gate))
    p_all = p.reshape(DEPTH * TOKENS, PLE_DIM)
    for i in range(DEPTH):
        j = i // 2
        if i % 2 == 0:
            name = "even_layer_tail"
            if i > 0:
                conv_in, qkv, f_logit = _even_in_proj(h, even_w_in[j], [])
            mix = _even_mixer(conv_in, qkv, f_logit, even_b_f[j], even_conv_w[j], w_out_even[j])
        else:
            name = "odd_layer_tail"
            mix = _odd_mixer(h, odd_w_in[j], w_zx_odd[j], odd_conv_w[j], odd_conv_b[j], odd_dt_bias[j],
                             odd_a_log[j], odd_d_skip[j], odd_norm_g[j], w_out_odd[j])
        h = _layer_tail(name, i, *mix, h, p_all, stacked)
    return h.reshape(BATCH, SEQ, D_MODEL)
```
